```python
import jax, jax.numpy as jnp
from jax import lax
import numpy as np

D_MODEL = 1024
BATCH = 8
SEQ = 2048
DEPTH = 2

GRID_W = 64
D_MIX = D_MODEL
HG_DIM = D_MIX // 2
HG_EXPAND = 128
HG_HEADS = HG_DIM // HG_EXPAND
HG_HEAD_DIM = HG_DIM // HG_HEADS
HG_CHUNK = 32
NA_DIM = D_MIX // 4
NA_HEAD_DIM = 64
NA_HEADS = NA_DIM // NA_HEAD_DIM
NA_KH = 8
NA_KW = 16
NA_QB_W = 16
NA_KB_W = NA_QB_W + NA_KW
NA_NCB = GRID_W // NA_QB_W
CONV_DIM = D_MIX - HG_DIM - NA_DIM
CONV_WIDTH = 31
IN_SPLITS = (HG_DIM, HG_DIM, HG_DIM, HG_DIM, HG_DIM,
             NA_DIM, NA_DIM, NA_DIM,
             CONV_DIM, CONV_DIM)
D_IN = sum(IN_SPLITS)
FFN_DIM = -(-8 * D_MODEL // (3 * 256)) * 256
EPS = 1e-6
NEG_INF = -1e30

kernel_name = "hymba_style_hgrn2_natten_conformer_encoder"


def _rmsnorm(x, g):
    xf = x.astype(jnp.float32)
    y = xf * lax.rsqrt(jnp.mean(xf * xf, axis=-1, keepdims=True) + EPS)
    return (y * g.astype(jnp.float32)).astype(x.dtype)


def _chunk_gated_recurrence(q, k, v, g):
    B, H, T, dk = q.shape
    dv = v.shape[-1]
    n = T // HG_CHUNK
    cs = lambda t: t.reshape(B, H, n, HG_CHUNK, t.shape[-1])
    q, k, v, g = cs(q), cs(k), cs(v), cs(g)
    b = jnp.cumsum(g, axis=3)
    q_in = q * jnp.exp(b)
    k_in = k * jnp.exp(-b)
    scores = jnp.einsum('bhncd,bhnsd->bhncs', q_in, k_in)
    mask = np.tril(np.ones((HG_CHUNK, HG_CHUNK), dtype=bool))
    scores = jnp.where(mask, scores, 0.0)
    o_intra = jnp.einsum('bhncs,bhnsv->bhncv', scores, v)
    b_last = b[:, :, :, -1:, :]
    d_state = jnp.einsum('bhncd,bhncv->bhndv', k * jnp.exp(b_last - b), v)
    decay = jnp.exp(b_last[:, :, :, 0, :])

    def step(S, inp):
        ds_n, dec_n, q_n = inp
        o_n = jnp.einsum('bhcd,bhdv->bhcv', q_n, S)
        return dec_n[..., None] * S + ds_n, o_n

    S0 = jnp.zeros((B, H, dk, dv), jnp.float32)
    _, o_inter = lax.scan(step, S0, (jnp.moveaxis(d_state, 2, 0), jnp.moveaxis(decay, 2, 0),
                                     jnp.moveaxis(q_in, 2, 0)))
    o = o_intra + jnp.moveaxis(o_inter, 0, 2)
    return o.reshape(B, H, T, dv)


def _hgrn2(q_raw, f_fwd_raw, f_bwd_raw, i_raw, g_raw, lb, norm_g):
    B, T, _ = q_raw.shape
    heads = lambda t: t.reshape(B, T, HG_HEADS, -1).transpose(0, 2, 1, 3)
    q = heads(jax.nn.silu(q_raw.astype(jnp.float32)))
    v = heads(i_raw.astype(jnp.float32))

    def gates(z, lb_d):
        z = z.astype(jnp.float32)
        f = lb_d + (1.0 - lb_d) * jax.nn.sigmoid(z)
        k = (1.0 - lb_d) * jax.nn.sigmoid(-z)
        return heads(k), heads(jnp.log(f))

    k_f, g_f = gates(f_fwd_raw, lb[0])
    k_b, g_b = gates(f_bwd_raw, lb[1])
    flip = lambda t: jnp.flip(t, axis=2)
    o = _chunk_gated_recurrence(q, k_f, v, g_f) + \
        flip(_chunk_gated_recurrence(flip(q), flip(k_b), flip(v), flip(g_b)))
    o = o * lax.rsqrt(jnp.mean(o * o, axis=-1, keepdims=True) + EPS) * norm_g.astype(jnp.float32)
    o = o.transpose(0, 2, 1, 3).reshape(B, T, HG_DIM)
    return o * jax.nn.silu(g_raw.astype(jnp.float32))


def _neighbourhood_attention(q_raw, k_raw, v_raw, rpb):
    B, T, _ = q_raw.shape
    rows = T // GRID_W
    kh = min(NA_KH, rows)
    nk = kh * NA_KB_W
    heads = lambda t: t.astype(jnp.float32).reshape(B, T, NA_HEADS, NA_HEAD_DIM).transpose(0, 2, 1, 3)
    q, k, v = heads(q_raw), heads(k_raw), heads(v_raw)

    r = np.arange(rows)
    key_rows = np.clip(r - kh // 2, 0, rows - kh)[:, None] + np.arange(kh)[None, :]
    c0 = np.arange(NA_NCB) * NA_QB_W
    key_cols = np.clip(c0 - NA_KW // 2, 0, GRID_W - NA_KB_W)[:, None] + np.arange(NA_KB_W)[None, :]
    tok = (key_rows[:, None, :, None] * GRID_W + key_cols[None, :, None, :]).reshape(rows, NA_NCB, nk)
    kg = k[:, :, tok]
    vg = v[:, :, tok]

    q_col = c0[:, None] + np.arange(NA_QB_W)[None, :]
    q_start = np.clip(q_col - NA_KW // 2, 0, GRID_W - NA_KW)
    k_col = np.tile(key_cols, (1, kh))
    valid = (k_col[:, None, :] >= q_start[:, :, None]) & (k_col[:, None, :] < q_start[:, :, None] + NA_KW)
    col_idx = np.clip(k_col[:, None, :] - q_col[:, :, None] + NA_KW - 1, 0, 2 * NA_KW - 2)
    row_idx = np.repeat(key_rows - r[:, None], NA_KB_W, axis=1) + NA_KH - 1
    bias = rpb.astype(jnp.float32)[:, row_idx[:, None, None, :], col_idx[None, :, :, :]]

    qb = q.reshape(B, NA_HEADS, rows, NA_NCB, NA_QB_W, NA_HEAD_DIM)
    s = jnp.einsum('bhrjqd,bhrjkd->bhrjqk', qb, kg) * (NA_HEAD_DIM ** -0.5) + bias[None]
    s = jnp.where(valid, s, NEG_INF)
    p = jax.nn.softmax(s, axis=-1)
    o = jnp.einsum('bhrjqk,bhrjkd->bhrjqd', p, vg)
    return o.reshape(B, NA_HEADS, T, NA_HEAD_DIM).transpose(0, 2, 1, 3).reshape(B, T, NA_DIM)


def _conformer_conv(a, gate, w_dw, b_dw, ln_g, ln_b):
    u = a.astype(jnp.float32) * jax.nn.sigmoid(gate.astype(jnp.float32))
    u = lax.conv_general_dilated(u, w_dw.astype(jnp.float32)[:, None, :], window_strides=(1,),
                                 padding=[(CONV_WIDTH // 2, CONV_WIDTH // 2)],
                                 dimension_numbers=('NWC', 'WIO', 'NWC'),
                                 feature_group_count=CONV_DIM) + b_dw.astype(jnp.float32)
    mu = jnp.mean(u, axis=-1, keepdims=True)
    var = jnp.mean(jnp.square(u - mu), axis=-1, keepdims=True)
    u = (u - mu) * lax.rsqrt(var + EPS) * ln_g.astype(jnp.float32) + ln_b.astype(jnp.float32)
    return jax.nn.silu(u)


def setup_inputs(seed: int = 0) -> dict:
    key = jax.random.key(seed)
    ks = jax.random.split(key, 16)
    nrm = lambda k, shape, scale: jax.random.normal(k, shape, jnp.float32) * scale
    return {
        "x": nrm(ks[0], (BATCH, SEQ, D_MODEL), 1.0),
        "mix_norm_g": 1.0 + nrm(ks[1], (DEPTH, D_MODEL), 0.02),
        "w_in": nrm(ks[2], (DEPTH, D_MODEL, D_IN), D_MODEL ** -0.5),
        "hg_lower_bounds": nrm(ks[3], (DEPTH, 2, HG_DIM), 0.1),
        "hg_norm_g": 1.0 + nrm(ks[4], (DEPTH, HG_HEAD_DIM), 0.02),
        "na_rpb": nrm(ks[5], (DEPTH, NA_HEADS, 2 * NA_KH - 1, 2 * NA_KW - 1), 0.02),
        "conv_w": nrm(ks[6], (DEPTH, CONV_WIDTH, CONV_DIM), CONV_WIDTH ** -0.5),
        "conv_b": nrm(ks[7], (DEPTH, CONV_DIM), 0.02),
        "conv_ln_g": 1.0 + nrm(ks[8], (DEPTH, CONV_DIM), 0.02),
        "conv_ln_b": nrm(ks[9], (DEPTH, CONV_DIM), 0.02),
        "w_out": nrm(ks[10], (DEPTH, D_MIX, D_MODEL), D_MIX ** -0.5),
        "ffn_norm_g": 1.0 + nrm(ks[11], (DEPTH, D_MODEL), 0.02),
        "w_gate_up": nrm(ks[12], (DEPTH, D_MODEL, 2 * FFN_DIM), D_MODEL ** -0.5),
        "w_down": nrm(ks[13], (DEPTH, FFN_DIM, D_MODEL), FFN_DIM ** -0.5),
        "final_norm_g": 1.0 + nrm(ks[14], (D_MODEL,), 0.02),
    }


def reference(x, mix_norm_g, w_in, hg_lower_bounds, hg_norm_g, na_rpb, conv_w, conv_b,
              conv_ln_g, conv_ln_b, w_out, ffn_norm_g, w_gate_up, w_down, final_norm_g):
    lbs = jax.nn.softmax(hg_lower_bounds.astype(jnp.float32), axis=0)
    lbs = jnp.cumsum(lbs, axis=0) - lbs[0:1]
    split_idx = []
    acc = 0
    for s in IN_SPLITS[:-1]:
        acc += s
        split_idx.append(acc)
    for l in range(DEPTH):
        h = _rmsnorm(x, mix_norm_g[l])
        p = h @ w_in[l]
        hq, hff, hfb, hi, hg, nq, nk, nv, ca, cg = jnp.split(p, split_idx, axis=-1)
        y_hg = _hgrn2(hq, hff, hfb, hi, hg, lbs[l], hg_norm_g[l])
        y_na = _neighbourhood_attention(nq, nk, nv, na_rpb[l])
        y_cv = _conformer_conv(ca, cg, conv_w[l], conv_b[l], conv_ln_g[l], conv_ln_b[l])
        mixed = jnp.concatenate([y_hg, y_na, y_cv], axis=-1).astype(x.dtype)
        x = x + mixed @ w_out[l]
        h = _rmsnorm(x, ffn_norm_g[l])
        gt, up = jnp.split(h @ w_gate_up[l], 2, axis=-1)
        x = x + (jax.nn.silu(gt) * up) @ w_down[l]
    return _rmsnorm(x, final_norm_g)
```

```python
import functools

import numpy as np
import jax
import jax.numpy as jnp
from jax import lax
from jax.experimental import pallas as pl
from jax.experimental.pallas import tpu as pltpu

D_MODEL = 1024
SEQ = 2048
DEPTH = 2
GRID_W = 64
ROWS = SEQ // GRID_W
HG_DIM = 512
HG_HEADS = 4
HG_HEAD_DIM = 128
NA_DIM = 256
NA_HEADS = 4
NA_HEAD_DIM = 64
NA_KH = 8
NA_KW = 16
CONV_DIM = 256
CONV_WIDTH = 31
D_IN = 5 * HG_DIM + 3 * NA_DIM + 2 * CONV_DIM
FFN_DIM = 2816
EPS = 1e-6
NEG_INF = -1e30

LANES = 128
VMEM_LIMIT_BYTES = 60 * 1024 * 1024

ROW_TILE = 256
HG_CHUNK = 64
HG_NCHUNK = SEQ // HG_CHUNK
HG_COLS = 5 * HG_HEAD_DIM
NA_COL0 = 5 * HG_DIM
CV_COL0 = NA_COL0 + 3 * NA_DIM
NA_KEYS = NA_KH * GRID_W
CONV_TILE = 64
CONV_PAD = 16
P_COLS = 3 * NA_DIM
FFN_TILE = 512
FFN_CHUNK = 256
FFN_NCHUNK = FFN_DIM // FFN_CHUNK

F32 = jnp.float32
BF16 = jnp.bfloat16


def _sigmoid(x):
    return 1.0 / (1.0 + jnp.exp(-x))


def _dot(a, b):
    return jnp.dot(a, b, preferred_element_type=F32)


def _dot_nt(a, b):
    return lax.dot_general(a, b, (((1,), (1,)), ((), ())), preferred_element_type=F32)


def _dot_tn(a, b):
    return lax.dot_general(a, b, (((0,), (0,)), ((), ())), preferred_element_type=F32)


def _split3(x):
    hi = x.astype(BF16)
    r1 = x - hi.astype(F32)
    mid = r1.astype(BF16)
    lo = (r1 - mid.astype(F32)).astype(BF16)
    return hi, mid, lo


def _mixer_kernel(layer, x_ref, ng_ref, win_ref, hlb_ref, hgn_ref, bias_ref, cw_ref, cb_ref, lng_ref, lnb_ref,
                  wout_ref, o_ref, h_scr, y_scr, p_scr, of_scr, ob_scr, sf_scr, sb_scr, upad_scr):
    n_row_tiles = SEQ // ROW_TILE

    def norm_body(i, c):
        r = pl.multiple_of(i * ROW_TILE, ROW_TILE)
        xt = x_ref[pl.ds(r, ROW_TILE), :]
        ms = jnp.mean(xt * xt, axis=-1, keepdims=True)
        h_scr[pl.ds(r, ROW_TILE), :] = (xt * lax.rsqrt(ms + EPS) * ng_ref[...]).astype(BF16)
        return c

    lax.fori_loop(0, n_row_tiles, norm_body, 0)

    def project(col0, ncols):
        def body(i, c):
            r = pl.multiple_of(i * ROW_TILE, ROW_TILE)
            p_scr[pl.ds(r, ROW_TILE), 0:ncols] = _dot(h_scr[pl.ds(r, ROW_TILE), :], win_ref[:, col0:col0 + ncols])
            return c
        lax.fori_loop(0, n_row_tiles, body, 0)

    hlb = [hlb_ref[d] for d in range(DEPTH)]
    hmax = functools.reduce(jnp.maximum, hlb)
    hexp = [jnp.exp(v - hmax) for v in hlb]
    hsum = functools.reduce(lambda a, b: a + b, hexp)
    hsm = [e / hsum for e in hexp]
    lbs = functools.reduce(lambda a, b: a + b, hsm[:layer + 1]) - hsm[0]

    L = HG_CHUNK
    row_i = lax.broadcasted_iota(jnp.int32, (L, L), 0)
    col_i = lax.broadcasted_iota(jnp.int32, (L, L), 1)
    lower = row_i >= col_i
    upper = col_i >= row_i
    lower_b = jnp.where(lower, 1.0, 0.0).astype(BF16)
    upper_b = jnp.where(upper, 1.0, 0.0).astype(BF16)

    def recurrence_chunk(r, zcol, lb, mask, cum_op, ref_row, last_row, s_scr, out_scr):
        q_raw = p_scr[pl.ds(r, L), 0:HG_HEAD_DIM]
        z = p_scr[pl.ds(r, L), zcol:zcol + HG_HEAD_DIM]
        v = p_scr[pl.ds(r, L), 3 * HG_HEAD_DIM:4 * HG_HEAD_DIM]
        q = q_raw * _sigmoid(q_raw)
        f = lb + (1.0 - lb) * _sigmoid(z)
        k = (1.0 - lb) * _sigmoid(-z)
        g = jnp.log(f)
        g3 = jnp.concatenate(_split3(g), axis=1)
        c3 = _dot(cum_op, g3)
        cum = c3[:, 0:LANES] + c3[:, LANES:2 * LANES] + c3[:, 2 * LANES:3 * LANES]
        cum_ref = cum[ref_row:ref_row + 1, :]
        cum_last = cum[last_row:last_row + 1, :]
        q_in = q * jnp.exp(cum - cum_ref)
        k_in = k * jnp.exp(cum_ref - cum)
        k_d = k_in * jnp.exp(cum_last - cum_ref)
        q_abs = q_in * jnp.exp(cum_ref)
        decay = jnp.exp(cum_last)
        vb = v.astype(BF16)
        scores = jnp.where(mask, _dot_nt(q_in.astype(BF16), k_in.astype(BF16)), 0.0)
        s_prev = s_scr[...]
        o = _dot(scores.astype(BF16), vb) + _dot_nt(q_abs.astype(BF16), s_prev.astype(BF16))
        out_scr[pl.ds(r, L), :] = o
        s_scr[...] = s_prev * decay + _dot_tn(vb, k_d.astype(BF16))

    for hd in range(HG_HEADS):
        project(hd * HG_COLS, HG_COLS)
        sl = slice(hd * HG_HEAD_DIM, (hd + 1) * HG_HEAD_DIM)
        lb_f = lbs[0:1, sl]
        lb_b = lbs[1:2, sl]
        sf_scr[...] = jnp.zeros_like(sf_scr)
        sb_scr[...] = jnp.zeros_like(sb_scr)

        def chunk_body(n, c):
            rf = pl.multiple_of(n * L, L)
            rb = pl.multiple_of((HG_NCHUNK - 1 - n) * L, L)
            recurrence_chunk(rf, HG_HEAD_DIM, lb_f, lower, lower_b, L // 2 - 1, L - 1, sf_scr, of_scr)
            recurrence_chunk(rb, 2 * HG_HEAD_DIM, lb_b, upper, upper_b, L // 2, 0, sb_scr, ob_scr)
            return c

        lax.fori_loop(0, HG_NCHUNK, chunk_body, 0)

        def hg_out_body(i, c):
            r = pl.multiple_of(i * ROW_TILE, ROW_TILE)
            o = of_scr[pl.ds(r, ROW_TILE), :] + ob_scr[pl.ds(r, ROW_TILE), :]
            o = o * lax.rsqrt(jnp.mean(o * o, axis=-1, keepdims=True) + EPS) * hgn_ref[...]
            gate = p_scr[pl.ds(r, ROW_TILE), 4 * HG_HEAD_DIM:5 * HG_HEAD_DIM]
            y_scr[pl.ds(r, ROW_TILE), sl] = (o * (gate * _sigmoid(gate))).astype(BF16)
            return c

        lax.fori_loop(0, n_row_tiles, hg_out_body, 0)

    project(NA_COL0, 3 * NA_DIM)
    lane = lax.broadcasted_iota(jnp.int32, (1, LANES), 1)
    first_half = lane < NA_HEAD_DIM
    scale = NA_HEAD_DIM ** -0.5

    def na_body(r, c):
        r0 = jnp.clip(r - NA_KH // 2, 0, ROWS - NA_KH)
        var = r - r0
        qrow = pl.multiple_of(r * GRID_W, GRID_W)
        krow = pl.multiple_of(r0 * GRID_W, GRID_W)
        for hp in range(NA_HEADS // 2):
            c0 = hp * LANES
            qt = p_scr[pl.ds(qrow, GRID_W), c0:c0 + LANES] * scale
            kt = p_scr[pl.ds(krow, NA_KEYS), NA_DIM + c0:NA_DIM + c0 + LANES].astype(BF16)
            vt = p_scr[pl.ds(krow, NA_KEYS), 2 * NA_DIM + c0:2 * NA_DIM + c0 + LANES].astype(BF16)
            outs = []
            for hh in range(2):
                sel = first_half if hh == 0 else jnp.logical_not(first_half)
                qm = jnp.where(sel, qt, 0.0).astype(BF16)
                bt = bias_ref[2 * hp + hh, var]
                s = jnp.where(bt > 0.5 * NEG_INF, _dot_nt(qm, kt) + bt, NEG_INF)
                m = jnp.max(s, axis=-1, keepdims=True)
                e = jnp.exp(s - m)
                den = jnp.sum(e, axis=-1, keepdims=True)
                outs.append(_dot(e.astype(BF16), vt) / den)
            y_scr[pl.ds(qrow, GRID_W), HG_DIM + c0:HG_DIM + c0 + LANES] = jnp.where(first_half, outs[0], outs[1]).astype(BF16)
        return c

    lax.fori_loop(0, ROWS, na_body, 0)

    project(CV_COL0, 2 * CONV_DIM)
    upad_scr[0:CONV_PAD, :] = jnp.zeros((CONV_PAD, CONV_DIM), F32)
    upad_scr[CONV_PAD + SEQ:CONV_PAD + SEQ + CONV_PAD, :] = jnp.zeros((CONV_PAD, CONV_DIM), F32)

    def glu_body(i, c):
        r = pl.multiple_of(i * ROW_TILE, ROW_TILE)
        a = p_scr[pl.ds(r, ROW_TILE), 0:CONV_DIM]
        gate = p_scr[pl.ds(r, ROW_TILE), CONV_DIM:2 * CONV_DIM]
        upad_scr[pl.ds(CONV_PAD + r, ROW_TILE), :] = a * _sigmoid(gate)
        return c

    lax.fori_loop(0, n_row_tiles, glu_body, 0)

    def conv_body(i, c):
        r = pl.multiple_of(i * CONV_TILE, CONV_TILE)
        win = upad_scr[pl.ds(r, CONV_TILE + 2 * CONV_PAD), :]
        off = CONV_PAD - CONV_WIDTH // 2
        acc = jnp.zeros((CONV_TILE, CONV_DIM), F32) + cb_ref[...]
        for j in range(CONV_WIDTH):
            acc = acc + win[off + j:off + j + CONV_TILE, :] * cw_ref[j:j + 1, :]
        mu = jnp.mean(acc, axis=-1, keepdims=True)
        d = acc - mu
        var = jnp.mean(d * d, axis=-1, keepdims=True)
        u = d * lax.rsqrt(var + EPS) * lng_ref[...] + lnb_ref[...]
        y_scr[pl.ds(r, CONV_TILE), HG_DIM + NA_DIM:D_MODEL] = (u * _sigmoid(u)).astype(BF16)
        return c

    lax.fori_loop(0, SEQ // CONV_TILE, conv_body, 0)

    def out_body(i, c):
        r = pl.multiple_of(i * ROW_TILE, ROW_TILE)
        o_ref[pl.ds(r, ROW_TILE), :] = x_ref[pl.ds(r, ROW_TILE), :] + _dot(y_scr[pl.ds(r, ROW_TILE), :], wout_ref[...])
        return c

    lax.fori_loop(0, n_row_tiles, out_body, 0)


def _ffn_kernel(final, x_ref, ng_ref, wgu_ref, wd_ref, fg_ref, o_ref, h_scr, acc_scr):
    x = x_ref[...]
    ms = jnp.mean(x * x, axis=-1, keepdims=True)
    h_scr[...] = (x * lax.rsqrt(ms + EPS) * ng_ref[...]).astype(BF16)
    for c in range(FFN_NCHUNK):
        gu = _dot(h_scr[...], wgu_ref[:, 2 * FFN_CHUNK * c:2 * FFN_CHUNK * (c + 1)])
        gt = gu[:, 0:FFN_CHUNK]
        up = gu[:, FFN_CHUNK:2 * FFN_CHUNK]
        a = (gt * _sigmoid(gt) * up).astype(BF16)
        d = _dot(a, wd_ref[FFN_CHUNK * c:FFN_CHUNK * (c + 1), :])
        if c == 0:
            acc_scr[...] = d
        else:
            acc_scr[...] += d
    y = x_ref[...] + acc_scr[...]
    if final:
        y = y * lax.rsqrt(jnp.mean(y * y, axis=-1, keepdims=True) + EPS) * fg_ref[...]
    o_ref[...] = y


def _resident(shape):
    nd = len(shape)
    return pl.BlockSpec(shape, lambda *_: (0,) * nd, pipeline_mode=pl.Buffered(1))


def _mixer_call(layer, x, ng, win, hlb, hgn, bias, cw, cb, lng, lnb, wout):
    batch = x.shape[0]
    seq_block = pl.BlockSpec((None, SEQ, D_MODEL), lambda b: (b, 0, 0), pipeline_mode=pl.Buffered(1))
    return pl.pallas_call(
        functools.partial(_mixer_kernel, layer),
        grid=(batch,),
        in_specs=[seq_block] + [_resident(a.shape) for a in (ng, win, hlb, hgn, bias, cw, cb, lng, lnb, wout)],
        out_specs=pl.BlockSpec((None, SEQ, D_MODEL), lambda b: (b, 0, 0), pipeline_mode=pl.Buffered(1)),
        out_shape=jax.ShapeDtypeStruct(x.shape, x.dtype),
        scratch_shapes=[
            pltpu.VMEM((SEQ, D_MODEL), BF16),
            pltpu.VMEM((SEQ, D_MODEL), BF16),
            pltpu.VMEM((SEQ, P_COLS), F32),
            pltpu.VMEM((SEQ, HG_HEAD_DIM), F32),
            pltpu.VMEM((SEQ, HG_HEAD_DIM), F32),
            pltpu.VMEM((HG_HEAD_DIM, HG_HEAD_DIM), F32),
            pltpu.VMEM((HG_HEAD_DIM, HG_HEAD_DIM), F32),
            pltpu.VMEM((SEQ + 2 * CONV_PAD, CONV_DIM), F32),
        ],
        compiler_params=pltpu.CompilerParams(dimension_semantics=("arbitrary",), vmem_limit_bytes=VMEM_LIMIT_BYTES),
        name=f"mixer_l{layer}",
    )(x, ng, win, hlb, hgn, bias, cw, cb, lng, lnb, wout)


def _ffn_call(layer, final, x2d, ng, wgu, wd, fg):
    n_tok = x2d.shape[0]
    tile = pl.BlockSpec((FFN_TILE, D_MODEL), lambda i: (i, 0))
    return pl.pallas_call(
        functools.partial(_ffn_kernel, final),
        grid=(n_tok // FFN_TILE,),
        in_specs=[tile] + [_resident(a.shape) for a in (ng, wgu, wd, fg)],
        out_specs=tile,
        out_shape=jax.ShapeDtypeStruct(x2d.shape, x2d.dtype),
        scratch_shapes=[pltpu.VMEM((FFN_TILE, D_MODEL), BF16), pltpu.VMEM((FFN_TILE, D_MODEL), F32)],
        compiler_params=pltpu.CompilerParams(dimension_semantics=("arbitrary",), vmem_limit_bytes=VMEM_LIMIT_BYTES),
        name=f"ffn_l{layer}",
    )(x2d, ng, wgu, wd, fg)


def _w_in_column_order():
    cols = []
    for hd in range(HG_HEADS):
        for grp in range(5):
            start = grp * HG_DIM + hd * HG_HEAD_DIM
            cols.append(np.arange(start, start + HG_HEAD_DIM))
    cols.append(np.arange(NA_COL0, D_IN))
    return np.concatenate(cols)


def _w_gate_up_column_order():
    cols = []
    for c in range(FFN_NCHUNK):
        cols.append(np.arange(c * FFN_CHUNK, (c + 1) * FFN_CHUNK))
        cols.append(FFN_DIM + np.arange(c * FFN_CHUNK, (c + 1) * FFN_CHUNK))
    return np.concatenate(cols)


def _na_bias_index():
    var = np.arange(NA_KH)[:, None, None, None]
    q = np.arange(GRID_W)[None, :, None, None]
    kr = np.arange(NA_KH)[None, None, :, None]
    kc = np.arange(GRID_W)[None, None, None, :]
    q_start = np.clip(q - NA_KW // 2, 0, GRID_W - NA_KW)
    valid = (kc >= q_start) & (kc < q_start + NA_KW)
    row_idx = np.broadcast_to(kr - var + NA_KH - 1, (NA_KH, GRID_W, NA_KH, GRID_W))
    col_idx = np.broadcast_to(np.clip(kc - q + NA_KW - 1, 0, 2 * NA_KW - 2), (NA_KH, GRID_W, NA_KH, GRID_W))
    valid = np.broadcast_to(valid, (NA_KH, GRID_W, NA_KH, GRID_W))
    shape = (NA_KH, GRID_W, NA_KEYS)
    return row_idx.reshape(shape), col_idx.reshape(shape), valid.reshape(shape)


def kernel(x, mix_norm_g, w_in, hg_lower_bounds, hg_norm_g, na_rpb, conv_w, conv_b, conv_ln_g, conv_ln_b, w_out,
           ffn_norm_g, w_gate_up, w_down, final_norm_g):
    batch, seq, d = x.shape
    assert (seq, d) == (SEQ, D_MODEL) and w_in.shape == (DEPTH, D_MODEL, D_IN)
    in_order = _w_in_column_order()
    gu_order = _w_gate_up_column_order()
    row_idx, col_idx, valid = _na_bias_index()
    hlb = hg_lower_bounds.astype(F32)
    fg = final_norm_g.reshape(1, D_MODEL)
    for l in range(DEPTH):
        win = w_in[l][:, in_order].astype(BF16)
        wout = w_out[l].astype(BF16)
        wgu = w_gate_up[l][:, gu_order].astype(BF16)
        wd = w_down[l].astype(BF16)
        bias = jnp.where(valid[None], na_rpb[l].astype(F32)[:, row_idx, col_idx], NEG_INF)
        x = _mixer_call(l, x, mix_norm_g[l].reshape(1, D_MODEL), win, hlb, hg_norm_g[l].reshape(1, HG_HEAD_DIM), bias,
                        conv_w[l], conv_b[l].reshape(1, CONV_DIM), conv_ln_g[l].reshape(1, CONV_DIM),
                        conv_ln_b[l].reshape(1, CONV_DIM), wout)
        x2d = _ffn_call(l, l == DEPTH - 1, x.reshape(batch * seq, d), ffn_norm_g[l].reshape(1, D_MODEL), wgu, wd, fg)
        x = x2d.reshape(batch, seq, d)
    return x
```

```python
import functools

import numpy as np
import jax
import jax.numpy as jnp
from jax import lax
from jax.experimental import pallas as pl
from jax.experimental.pallas import tpu as pltpu

D_MODEL = 1024
SEQ = 2048
DEPTH = 2
GRID_W = 64
ROWS = SEQ // GRID_W
HG_DIM = 512
HG_HEADS = 4
HG_HEAD_DIM = 128
NA_DIM = 256
NA_HEADS = 4
NA_HEAD_DIM = 64
NA_KH = 8
NA_KW = 16
CONV_DIM = 256
CONV_WIDTH = 31
D_IN = 5 * HG_DIM + 3 * NA_DIM + 2 * CONV_DIM
FFN_DIM = 2816
EPS = 1e-6
NEG_INF = -1e30

LANES = 128
VMEM_LIMIT_BYTES = 60 * 1024 * 1024

ROW_TILE = 256
HG_CHUNK = 64
HG_NCHUNK = SEQ // HG_CHUNK
HG_COLS = 5 * HG_HEAD_DIM
NA_COL0 = 5 * HG_DIM
CV_COL0 = NA_COL0 + 3 * NA_DIM
NA_KEYS = NA_KH * GRID_W
CONV_TILE = 64
CONV_PAD = 16
P_COLS = 3 * NA_DIM
FFN_TILE = 512
FFN_CHUNK = 256
FFN_NCHUNK = FFN_DIM // FFN_CHUNK

F32 = jnp.float32
BF16 = jnp.bfloat16


def _sigmoid(x):
    return 1.0 / (1.0 + jnp.exp(-x))


def _dot(a, b):
    return jnp.dot(a, b, preferred_element_type=F32)


def _dot_nt(a, b):
    return lax.dot_general(a, b, (((1,), (1,)), ((), ())), preferred_element_type=F32)


def _dot_tn(a, b):
    return lax.dot_general(a, b, (((0,), (0,)), ((), ())), preferred_element_type=F32)


def _split3(x):
    hi = x.astype(BF16)
    r1 = x - hi.astype(F32)
    mid = r1.astype(BF16)
    lo = (r1 - mid.astype(F32)).astype(BF16)
    return hi, mid, lo


def _mixer_kernel(layer, x_ref, ng_ref, win_ref, hlb_ref, hgn_ref, bias_ref, cw_ref, cb_ref, lng_ref, lnb_ref,
                  wout_ref, o_ref, h_scr, y_scr, p_scr, of_scr, ob_scr, sf_scr, sb_scr, upad_scr):
    n_row_tiles = SEQ // ROW_TILE

    def norm_body(i, c):
        r = pl.multiple_of(i * ROW_TILE, ROW_TILE)
        xt = x_ref[pl.ds(r, ROW_TILE), :]
        ms = jnp.mean(xt * xt, axis=-1, keepdims=True)
        h_scr[pl.ds(r, ROW_TILE), :] = (xt * lax.rsqrt(ms + EPS) * ng_ref[...]).astype(BF16)
        return c

    lax.fori_loop(0, n_row_tiles, norm_body, 0)

    def project(col0, ncols):
        def body(i, c):
            r = pl.multiple_of(i * ROW_TILE, ROW_TILE)
            p_scr[pl.ds(r, ROW_TILE), 0:ncols] = _dot(h_scr[pl.ds(r, ROW_TILE), :], win_ref[:, col0:col0 + ncols])
            return c
        lax.fori_loop(0, n_row_tiles, body, 0)

    hlb = [hlb_ref[d] for d in range(DEPTH)]
    hmax = functools.reduce(jnp.maximum, hlb)
    hexp = [jnp.exp(v - hmax) for v in hlb]
    hsum = functools.reduce(lambda a, b: a + b, hexp)
    hsm = [e / hsum for e in hexp]
    lbs = functools.reduce(lambda a, b: a + b, hsm[:layer + 1]) - hsm[0]

    L = HG_CHUNK
    row_i = lax.broadcasted_iota(jnp.int32, (L, L), 0)
    col_i = lax.broadcasted_iota(jnp.int32, (L, L), 1)
    lower = row_i >= col_i
    upper = col_i >= row_i
    lower_b = jnp.where(lower, 1.0, 0.0).astype(BF16)
    upper_b = jnp.where(upper, 1.0, 0.0).astype(BF16)

    def recurrence_chunk(r, zcol, lb, mask, cum_op, ref_row, last_row, s_scr, out_scr):
        q_raw = p_scr[pl.ds(r, L), 0:HG_HEAD_DIM]
        z = p_scr[pl.ds(r, L), zcol:zcol + HG_HEAD_DIM]
        v = p_scr[pl.ds(r, L), 3 * HG_HEAD_DIM:4 * HG_HEAD_DIM]
        q = q_raw * _sigmoid(q_raw)
        f = lb + (1.0 - lb) * _sigmoid(z)
        k = (1.0 - lb) * _sigmoid(-z)
        g = jnp.log(f)
        g3 = jnp.concatenate(_split3(g), axis=1)
        c3 = _dot(cum_op, g3)
        cum = c3[:, 0:LANES] + c3[:, LANES:2 * LANES] + c3[:, 2 * LANES:3 * LANES]
        cum_ref = cum[ref_row:ref_row + 1, :]
        cum_last = cum[last_row:last_row + 1, :]
        q_in = q * jnp.exp(cum - cum_ref)
        k_in = k * jnp.exp(cum_ref - cum)
        k_d = k_in * jnp.exp(cum_last - cum_ref)
        q_abs = q_in * jnp.exp(cum_ref)
        decay = jnp.exp(cum_last)
        vb = v.astype(BF16)
        scores = jnp.where(mask, _dot_nt(q_in.astype(BF16), k_in.astype(BF16)), 0.0)
        s_prev = s_scr[...]
        o = _dot(scores.astype(BF16), vb) + _dot_nt(q_abs.astype(BF16), s_prev.astype(BF16))
        out_scr[pl.ds(r, L), :] = o
        s_scr[...] = s_prev * decay + _dot_tn(vb, k_d.astype(BF16))

    for hd in range(HG_HEADS):
        project(hd * HG_COLS, HG_COLS)
        sl = slice(hd * HG_HEAD_DIM, (hd + 1) * HG_HEAD_DIM)
        lb_f = lbs[0:1, sl]
        lb_b = lbs[1:2, sl]
        sf_scr[...] = jnp.zeros_like(sf_scr)
        sb_scr[...] = jnp.zeros_like(sb_scr)

        def chunk_body(n, c):
            rf = pl.multiple_of(n * L, L)
            rb = pl.multiple_of((HG_NCHUNK - 1 - n) * L, L)
            recurrence_chunk(rf, HG_HEAD_DIM, lb_f, lower, lower_b, L // 2 - 1, L - 1, sf_scr, of_scr)
            recurrence_chunk(rb, 2 * HG_HEAD_DIM, lb_b, upper, upper_b, L // 2, 0, sb_scr, ob_scr)
            return c

        lax.fori_loop(0, HG_NCHUNK, chunk_body, 0, unroll=2)

        def hg_out_body(i, c):
            r = pl.multiple_of(i * ROW_TILE, ROW_TILE)
            o = of_scr[pl.ds(r, ROW_TILE), :] + ob_scr[pl.ds(r, ROW_TILE), :]
            o = o * lax.rsqrt(jnp.mean(o * o, axis=-1, keepdims=True) + EPS) * hgn_ref[...]
            gate = p_scr[pl.ds(r, ROW_TILE), 4 * HG_HEAD_DIM:5 * HG_HEAD_DIM]
            y_scr[pl.ds(r, ROW_TILE), sl] = (o * (gate * _sigmoid(gate))).astype(BF16)
            return c

        lax.fori_loop(0, n_row_tiles, hg_out_body, 0)

    project(NA_COL0, 3 * NA_DIM)
    lane = lax.broadcasted_iota(jnp.int32, (1, LANES), 1)
    first_half = lane < NA_HEAD_DIM
    scale = NA_HEAD_DIM ** -0.5

    def na_body(r, c):
        r0 = jnp.clip(r - NA_KH // 2, 0, ROWS - NA_KH)
        var = r - r0
        qrow = pl.multiple_of(r * GRID_W, GRID_W)
        krow = pl.multiple_of(r0 * GRID_W, GRID_W)
        for hp in range(NA_HEADS // 2):
            c0 = hp * LANES
            qt = p_scr[pl.ds(qrow, GRID_W), c0:c0 + LANES] * scale
            kt = p_scr[pl.ds(krow, NA_KEYS), NA_DIM + c0:NA_DIM + c0 + LANES].astype(BF16)
            vt = p_scr[pl.ds(krow, NA_KEYS), 2 * NA_DIM + c0:2 * NA_DIM + c0 + LANES].astype(BF16)
            outs = []
            for hh in range(2):
                sel = first_half if hh == 0 else jnp.logical_not(first_half)
                qm = jnp.where(sel, qt, 0.0).astype(BF16)
                bt = bias_ref[2 * hp + hh, var]
                s = jnp.where(bt > 0.5 * NEG_INF, _dot_nt(qm, kt) + bt, NEG_INF)
                m = jnp.max(s, axis=-1, keepdims=True)
                e = jnp.exp(s - m)
                den = jnp.sum(e, axis=-1, keepdims=True)
                outs.append(_dot(e.astype(BF16), vt) / den)
            y_scr[pl.ds(qrow, GRID_W), HG_DIM + c0:HG_DIM + c0 + LANES] = jnp.where(first_half, outs[0], outs[1]).astype(BF16)
        return c

    lax.fori_loop(0, ROWS, na_body, 0)

    project(CV_COL0, 2 * CONV_DIM)
    upad_scr[0:CONV_PAD, :] = jnp.zeros((CONV_PAD, CONV_DIM), F32)
    upad_scr[CONV_PAD + SEQ:CONV_PAD + SEQ + CONV_PAD, :] = jnp.zeros((CONV_PAD, CONV_DIM), F32)

    def glu_body(i, c):
        r = pl.multiple_of(i * ROW_TILE, ROW_TILE)
        a = p_scr[pl.ds(r, ROW_TILE), 0:CONV_DIM]
        gate = p_scr[pl.ds(r, ROW_TILE), CONV_DIM:2 * CONV_DIM]
        upad_scr[pl.ds(CONV_PAD + r, ROW_TILE), :] = a * _sigmoid(gate)
        return c

    lax.fori_loop(0, n_row_tiles, glu_body, 0)

    def conv_body(i, c):
        r = pl.multiple_of(i * CONV_TILE, CONV_TILE)
        win = upad_scr[pl.ds(r, CONV_TILE + 2 * CONV_PAD), :]
        off = CONV_PAD - CONV_WIDTH // 2
        acc = jnp.zeros((CONV_TILE, CONV_DIM), F32) + cb_ref[...]
        for j in range(CONV_WIDTH):
            acc = acc + win[off + j:off + j + CONV_TILE, :] * cw_ref[j:j + 1, :]
        mu = jnp.mean(acc, axis=-1, keepdims=True)
        d = acc - mu
        var = jnp.mean(d * d, axis=-1, keepdims=True)
        u = d * lax.rsqrt(var + EPS) * lng_ref[...] + lnb_ref[...]
        y_scr[pl.ds(r, CONV_TILE), HG_DIM + NA_DIM:D_MODEL] = (u * _sigmoid(u)).astype(BF16)
        return c

    lax.fori_loop(0, SEQ // CONV_TILE, conv_body, 0)

    def out_body(i, c):
        r = pl.multiple_of(i * ROW_TILE, ROW_TILE)
        o_ref[pl.ds(r, ROW_TILE), :] = x_ref[pl.ds(r, ROW_TILE), :] + _dot(y_scr[pl.ds(r, ROW_TILE), :], wout_ref[...])
        return c

    lax.fori_loop(0, n_row_tiles, out_body, 0)


def _ffn_kernel(final, x_ref, ng_ref, wgu_ref, wd_ref, fg_ref, o_ref, h_scr, acc_scr):
    x = x_ref[...]
    ms = jnp.mean(x * x, axis=-1, keepdims=True)
    h_scr[...] = (x * lax.rsqrt(ms + EPS) * ng_ref[...]).astype(BF16)
    for c in range(FFN_NCHUNK):
        gu = _dot(h_scr[...], wgu_ref[:, 2 * FFN_CHUNK * c:2 * FFN_CHUNK * (c + 1)])
        gt = gu[:, 0:FFN_CHUNK]
        up = gu[:, FFN_CHUNK:2 * FFN_CHUNK]
        a = (gt * _sigmoid(gt) * up).astype(BF16)
        d = _dot(a, wd_ref[FFN_CHUNK * c:FFN_CHUNK * (c + 1), :])
        if c == 0:
            acc_scr[...] = d
        else:
            acc_scr[...] += d
    y = x_ref[...] + acc_scr[...]
    if final:
        y = y * lax.rsqrt(jnp.mean(y * y, axis=-1, keepdims=True) + EPS) * fg_ref[...]
    o_ref[...] = y


def _resident(shape):
    nd = len(shape)
    return pl.BlockSpec(shape, lambda *_: (0,) * nd, pipeline_mode=pl.Buffered(1))


def _mixer_call(layer, x, ng, win, hlb, hgn, bias, cw, cb, lng, lnb, wout):
    batch = x.shape[0]
    seq_block = pl.BlockSpec((None, SEQ, D_MODEL), lambda b: (b, 0, 0), pipeline_mode=pl.Buffered(1))
    return pl.pallas_call(
        functools.partial(_mixer_kernel, layer),
        grid=(batch,),
        in_specs=[seq_block] + [_resident(a.shape) for a in (ng, win, hlb, hgn, bias, cw, cb, lng, lnb, wout)],
        out_specs=pl.BlockSpec((None, SEQ, D_MODEL), lambda b: (b, 0, 0), pipeline_mode=pl.Buffered(1)),
        out_shape=jax.ShapeDtypeStruct(x.shape, x.dtype),
        scratch_shapes=[
            pltpu.VMEM((SEQ, D_MODEL), BF16),
            pltpu.VMEM((SEQ, D_MODEL), BF16),
            pltpu.VMEM((SEQ, P_COLS), F32),
            pltpu.VMEM((SEQ, HG_HEAD_DIM), F32),
            pltpu.VMEM((SEQ, HG_HEAD_DIM), F32),
            pltpu.VMEM((HG_HEAD_DIM, HG_HEAD_DIM), F32),
            pltpu.VMEM((HG_HEAD_DIM, HG_HEAD_DIM), F32),
            pltpu.VMEM((SEQ + 2 * CONV_PAD, CONV_DIM), F32),
        ],
        compiler_params=pltpu.CompilerParams(dimension_semantics=("arbitrary",), vmem_limit_bytes=VMEM_LIMIT_BYTES),
        name=f"mixer_l{layer}",
    )(x, ng, win, hlb, hgn, bias, cw, cb, lng, lnb, wout)


def _ffn_call(layer, final, x2d, ng, wgu, wd, fg):
    n_tok = x2d.shape[0]
    tile = pl.BlockSpec((FFN_TILE, D_MODEL), lambda i: (i, 0))
    return pl.pallas_call(
        functools.partial(_ffn_kernel, final),
        grid=(n_tok // FFN_TILE,),
        in_specs=[tile] + [_resident(a.shape) for a in (ng, wgu, wd, fg)],
        out_specs=tile,
        out_shape=jax.ShapeDtypeStruct(x2d.shape, x2d.dtype),
        scratch_shapes=[pltpu.VMEM((FFN_TILE, D_MODEL), BF16), pltpu.VMEM((FFN_TILE, D_MODEL), F32)],
        compiler_params=pltpu.CompilerParams(dimension_semantics=("arbitrary",), vmem_limit_bytes=VMEM_LIMIT_BYTES),
        name=f"ffn_l{layer}",
    )(x2d, ng, wgu, wd, fg)


def _regroup_w_in(w):
    hg = w[:, :NA_COL0].reshape(D_MODEL, 5, HG_HEADS, HG_HEAD_DIM).transpose(0, 2, 1, 3).reshape(D_MODEL, NA_COL0)
    return jnp.concatenate([hg, w[:, NA_COL0:]], axis=1)


def _regroup_w_gate_up(w):
    return w.reshape(D_MODEL, 2, FFN_NCHUNK, FFN_CHUNK).transpose(0, 2, 1, 3).reshape(D_MODEL, 2 * FFN_DIM)


def _na_visible():
    q = np.arange(GRID_W)[:, None]
    kc = np.arange(GRID_W)[None, :]
    q_start = np.clip(q - NA_KW // 2, 0, GRID_W - NA_KW)
    return np.tile((kc >= q_start) & (kc < q_start + NA_KW), (1, NA_KH))


def _na_bias_table(rpb):
    n_rel = 2 * NA_KW - 1
    lead = GRID_W - NA_KW
    v = jnp.pad(rpb, ((0, 0), (0, 0), (lead, 2 * GRID_W - lead - n_rel)))
    m = jnp.tile(v, (1, 1, GRID_W))[:, :, :GRID_W * (2 * GRID_W - 1)]
    m = m.reshape(NA_HEADS, 2 * NA_KH - 1, GRID_W, 2 * GRID_W - 1)
    toep = m[:, :, :, GRID_W - 1:2 * GRID_W - 1]
    per_variant = [toep[:, NA_KH - 1 - var:2 * NA_KH - 1 - var].transpose(0, 2, 1, 3).reshape(NA_HEADS, GRID_W, NA_KEYS)
                   for var in range(NA_KH)]
    return jnp.where(_na_visible()[None, None], jnp.stack(per_variant, axis=1), NEG_INF)


def kernel(x, mix_norm_g, w_in, hg_lower_bounds, hg_norm_g, na_rpb, conv_w, conv_b, conv_ln_g, conv_ln_b, w_out,
           ffn_norm_g, w_gate_up, w_down, final_norm_g):
    batch, seq, d = x.shape
    assert (seq, d) == (SEQ, D_MODEL) and w_in.shape == (DEPTH, D_MODEL, D_IN)
    hlb = hg_lower_bounds.astype(F32)
    fg = final_norm_g.reshape(1, D_MODEL)
    for l in range(DEPTH):
        win = _regroup_w_in(w_in[l].astype(BF16))
        wout = w_out[l].astype(BF16)
        wgu = _regroup_w_gate_up(w_gate_up[l].astype(BF16))
        wd = w_down[l].astype(BF16)
        bias = _na_bias_table(na_rpb[l].astype(F32))
        x = _mixer_call(l, x, mix_norm_g[l].reshape(1, D_MODEL), win, hlb, hg_norm_g[l].reshape(1, HG_HEAD_DIM), bias,
                        conv_w[l], conv_b[l].reshape(1, CONV_DIM), conv_ln_g[l].reshape(1, CONV_DIM),
                        conv_ln_b[l].reshape(1, CONV_DIM), wout)
        x2d = _ffn_call(l, l == DEPTH - 1, x.reshape(batch * seq, d), ffn_norm_g[l].reshape(1, D_MODEL), wgu, wd, fg)
        x = x2d.reshape(batch, seq, d)
    return x
```

```python
import functools

import numpy as np
import jax
import jax.numpy as jnp
from jax import lax
from jax.experimental import pallas as pl
from jax.experimental.pallas import tpu as pltpu

D_MODEL = 1024
SEQ = 2048
DEPTH = 2
GRID_W = 64
ROWS = SEQ // GRID_W
HG_DIM = 512
HG_HEADS = 4
HG_HEAD_DIM = 128
NA_DIM = 256
NA_HEADS = 4
NA_HEAD_DIM = 64
NA_KH = 8
NA_KW = 16
CONV_DIM = 256
CONV_WIDTH = 31
D_IN = 5 * HG_DIM + 3 * NA_DIM + 2 * CONV_DIM
FFN_DIM = 2816
EPS = 1e-6
NEG_INF = -1e30

LANES = 128
SUBLANES = 8
VMEM_LIMIT_BYTES = 60 * 1024 * 1024

ROW_TILE = 256
HG_CHUNK = 64
HG_CHUNK_LOG2 = HG_CHUNK.bit_length() - 1
assert HG_CHUNK == 1 << HG_CHUNK_LOG2
HG_NCHUNK = SEQ // HG_CHUNK
HG_TILE = 256
HG_COLS = 5 * HG_HEAD_DIM
NA_COL0 = 5 * HG_DIM
CV_COL0 = NA_COL0 + 3 * NA_DIM
NA_KEYS = NA_KH * GRID_W
NA_ROWS_PER_STEP = 2
CONV_TILE = 64
CONV_PAD = 16
P_COLS = HG_COLS
assert P_COLS >= 2 * CONV_DIM and P_COLS >= NA_DIM
FFN_TILE = 512
FFN_CHUNK = 256
FFN_NCHUNK = FFN_DIM // FFN_CHUNK

F32 = jnp.float32
BF16 = jnp.bfloat16


def _sigmoid(x):
    return 1.0 / (1.0 + jnp.exp(-x))


def _dot(a, b):
    return jnp.dot(a, b, preferred_element_type=F32)


def _dot_nt(a, b):
    return lax.dot_general(a, b, (((1,), (1,)), ((), ())), preferred_element_type=F32)


def _dot_tn(a, b):
    return lax.dot_general(a, b, (((0,), (0,)), ((), ())), preferred_element_type=F32)


def _split3(x):
    hi = x.astype(BF16)
    r1 = x - hi.astype(F32)
    mid = r1.astype(BF16)
    lo = (r1 - mid.astype(F32)).astype(BF16)
    return hi, mid, lo


def _mixer_kernel(layer, x_ref, ng_ref, win_ref, hlb_ref, hgn_ref, bias_ref, cw_ref, cb_ref, lng_ref, lnb_ref,
                  wout_ref, o_ref, h_scr, y_scr, p_scr, oi_scr, qa_scr, d_scr, st_scr, dec_scr, upad_scr):
    n_row_tiles = SEQ // ROW_TILE

    def norm_body(i, c):
        r = pl.multiple_of(i * ROW_TILE, ROW_TILE)
        xt = x_ref[pl.ds(r, ROW_TILE), :]
        ms = jnp.mean(xt * xt, axis=-1, keepdims=True)
        h_scr[pl.ds(r, ROW_TILE), :] = (xt * lax.rsqrt(ms + EPS) * ng_ref[...]).astype(BF16)
        return c

    lax.fori_loop(0, n_row_tiles, norm_body, 0)

    def project(col0, ncols):
        def body(i, c):
            r = pl.multiple_of(i * ROW_TILE, ROW_TILE)
            p_scr[pl.ds(r, ROW_TILE), 0:ncols] = _dot(h_scr[pl.ds(r, ROW_TILE), :], win_ref[:, col0:col0 + ncols])
            return c
        lax.fori_loop(0, n_row_tiles, body, 0)

    hlb = [hlb_ref[d] for d in range(DEPTH)]
    hmax = functools.reduce(jnp.maximum, hlb)
    hexp = [jnp.exp(v - hmax) for v in hlb]
    hsum = functools.reduce(lambda a, b: a + b, hexp)
    hsm = [e / hsum for e in hexp]
    lbs = functools.reduce(lambda a, b: a + b, hsm[:layer + 1]) - hsm[0]

    L = HG_CHUNK
    TL = HG_TILE
    cpt = TL // L
    row_c = lax.broadcasted_iota(jnp.int32, (TL, TL), 0)
    col_c = lax.broadcasted_iota(jnp.int32, (TL, TL), 1)
    same_chunk = jnp.right_shift(row_c, HG_CHUNK_LOG2) == jnp.right_shift(col_c, HG_CHUNK_LOG2)
    lower = same_chunk & (row_c >= col_c)
    upper = same_chunk & (col_c >= row_c)
    lower_b = jnp.where(lower, 1.0, 0.0).astype(BF16)
    upper_b = jnp.where(upper, 1.0, 0.0).astype(BF16)

    def chunk_rows(a, row):
        return [a[c * L + row:c * L + row + 1, :] for c in range(cpt)]

    def spread(rows):
        return jnp.concatenate([jnp.broadcast_to(rw, (L, LANES)) for rw in rows], axis=0)

    for hd in range(HG_HEADS):
        project(hd * HG_COLS, HG_COLS)
        sl = slice(hd * HG_HEAD_DIM, (hd + 1) * HG_HEAD_DIM)
        directions = (
            (HG_HEAD_DIM, lbs[0:1, sl], lower, lower_b, L // 2 - 1, L - 1),
            (2 * HG_HEAD_DIM, lbs[1:2, sl], upper, upper_b, L // 2, 0),
        )

        def hg_local_body(i, c):
            r = pl.multiple_of(i * TL, TL)
            q_raw = p_scr[pl.ds(r, TL), 0:HG_HEAD_DIM]
            q = q_raw * _sigmoid(q_raw)
            vb = p_scr[pl.ds(r, TL), 3 * HG_HEAD_DIM:4 * HG_HEAD_DIM].astype(BF16)
            gates = []
            for zcol, lb, mask, cum_op, ref_row, last_row in directions:
                z = p_scr[pl.ds(r, TL), zcol:zcol + HG_HEAD_DIM]
                t = jnp.exp(-jnp.abs(z))
                big = 1.0 / (1.0 + t)
                small = t * big
                nonneg = z >= 0.0
                f = lb + (1.0 - lb) * jnp.where(nonneg, big, small)
                k = (1.0 - lb) * jnp.where(nonneg, small, big)
                gates.append((k, _dot(cum_op, jnp.concatenate(_split3(jnp.log(f)), axis=1))))
            scaled = []
            for d, ((zcol, lb, mask, cum_op, ref_row, last_row), (k, c3)) in enumerate(zip(directions, gates)):
                cum = c3[:, 0:LANES] + c3[:, LANES:2 * LANES] + c3[:, 2 * LANES:3 * LANES]
                ref_rows = chunk_rows(cum, ref_row)
                last_rows = chunk_rows(cum, last_row)
                rel = cum - spread(ref_rows)
                q_in = q * jnp.exp(rel)
                k_in = k * jnp.exp(-rel)
                qa_scr[d, pl.ds(r, TL), :] = (q_in * spread([jnp.exp(rw) for rw in ref_rows])).astype(BF16)
                k_d = (k_in * spread([jnp.exp(lr - rr) for lr, rr in zip(last_rows, ref_rows)])).astype(BF16)
                for cc in range(cpt):
                    dec_scr[d, i * cpt + cc] = jnp.broadcast_to(jnp.exp(last_rows[cc]), (SUBLANES, LANES))
                scaled.append((_dot_nt(q_in.astype(BF16), k_in.astype(BF16)), k_d))
            o_intra = None
            for d, ((zcol, lb, mask, cum_op, ref_row, last_row), (qk, k_d)) in enumerate(zip(directions, scaled)):
                o_dir = _dot(jnp.where(mask, qk, 0.0).astype(BF16), vb)
                o_intra = o_dir if o_intra is None else o_intra + o_dir
                for cc in range(cpt):
                    d_scr[d, i * cpt + cc] = _dot_tn(vb[cc * L:(cc + 1) * L], k_d[cc * L:(cc + 1) * L])
            oi_scr[pl.ds(r, TL), :] = o_intra
            return c

        lax.fori_loop(0, SEQ // TL, hg_local_body, 0)

        def hg_scan_body(n, carry):
            s_f, s_b = carry
            m = HG_NCHUNK - 1 - n
            st_scr[0, n] = s_f.astype(BF16)
            st_scr[1, m] = s_b.astype(BF16)
            s_f = s_f * dec_scr[0, n][0:1, :] + d_scr[0, n]
            s_b = s_b * dec_scr[1, m][0:1, :] + d_scr[1, m]
            return s_f, s_b

        s_zero = jnp.zeros((HG_HEAD_DIM, HG_HEAD_DIM), F32)
        lax.fori_loop(0, HG_NCHUNK, hg_scan_body, (s_zero, s_zero))

        def hg_out_body(i, c):
            r = pl.multiple_of(i * TL, TL)
            inter = []
            for cc in range(cpt):
                n = i * cpt + cc
                rc = pl.multiple_of(r + cc * L, L)
                qa = jnp.concatenate([qa_scr[0, pl.ds(rc, L), :], qa_scr[1, pl.ds(rc, L), :]], axis=1)
                st = jnp.concatenate([st_scr[0, n], st_scr[1, n]], axis=1)
                inter.append(_dot_nt(qa, st))
            o = oi_scr[pl.ds(r, TL), :] + jnp.concatenate(inter, axis=0)
            o = o * lax.rsqrt(jnp.mean(o * o, axis=-1, keepdims=True) + EPS) * hgn_ref[...]
            gate = p_scr[pl.ds(r, TL), 4 * HG_HEAD_DIM:5 * HG_HEAD_DIM]
            y_scr[pl.ds(r, TL), sl] = (o * (gate * _sigmoid(gate))).astype(BF16)
            return c

        lax.fori_loop(0, SEQ // TL, hg_out_body, 0, unroll=2)

    project(CV_COL0, 2 * CONV_DIM)
    upad_scr[0:CONV_PAD, :] = jnp.zeros((CONV_PAD, CONV_DIM), F32)
    upad_scr[CONV_PAD + SEQ:CONV_PAD + SEQ + CONV_PAD, :] = jnp.zeros((CONV_PAD, CONV_DIM), F32)

    def glu_body(i, c):
        r = pl.multiple_of(i * ROW_TILE, ROW_TILE)
        a = p_scr[pl.ds(r, ROW_TILE), 0:CONV_DIM]
        gate = p_scr[pl.ds(r, ROW_TILE), CONV_DIM:2 * CONV_DIM]
        upad_scr[pl.ds(CONV_PAD + r, ROW_TILE), :] = a * _sigmoid(gate)
        return c

    lax.fori_loop(0, n_row_tiles, glu_body, 0)

    def conv_body(i, c):
        r = pl.multiple_of(i * CONV_TILE, CONV_TILE)
        off = CONV_PAD - CONV_WIDTH // 2
        groups = -(-CONV_WIDTH // SUBLANES)
        win_rows = CONV_TILE + 2 * CONV_PAD
        halves = []
        for c0 in range(0, CONV_DIM, LANES):
            win = upad_scr[pl.ds(r, win_rows), c0:c0 + LANES]
            acc = jnp.zeros((CONV_TILE, LANES), F32) + cb_ref[:, c0:c0 + LANES]
            for s in range(SUBLANES):
                k = off + s
                shifted = win if k % win_rows == 0 else pltpu.roll(win, win_rows - k, axis=0)
                for a in range(groups):
                    j = SUBLANES * a + s
                    if j < CONV_WIDTH:
                        acc = acc + shifted[SUBLANES * a:SUBLANES * a + CONV_TILE, :] * cw_ref[j:j + 1, c0:c0 + LANES]
            halves.append(acc)
        acc = jnp.concatenate(halves, axis=1)
        mu = jnp.mean(acc, axis=-1, keepdims=True)
        d = acc - mu
        var = jnp.mean(d * d, axis=-1, keepdims=True)
        u = d * lax.rsqrt(var + EPS) * lng_ref[...] + lnb_ref[...]
        y_scr[pl.ds(r, CONV_TILE), HG_DIM + NA_DIM:D_MODEL] = (u * _sigmoid(u)).astype(BF16)
        return c

    lax.fori_loop(0, SEQ // CONV_TILE, conv_body, 0)

    scale = NA_HEAD_DIM ** -0.5

    def na_proj_body(i, c):
        r = pl.multiple_of(i * ROW_TILE, ROW_TILE)
        res = _dot(h_scr[pl.ds(r, ROW_TILE), :], win_ref[:, NA_COL0:NA_COL0 + 3 * NA_DIM])
        p_scr[pl.ds(r, ROW_TILE), 0:NA_DIM] = res[:, 0:NA_DIM] * scale
        h_scr[pl.ds(r, ROW_TILE), 0:2 * NA_DIM] = res[:, NA_DIM:3 * NA_DIM].astype(BF16)
        return c

    lax.fori_loop(0, n_row_tiles, na_proj_body, 0)
    lane = lax.broadcasted_iota(jnp.int32, (1, LANES), 1)
    first_half = lane < NA_HEAD_DIM

    def na_body(i, c):
        units = []
        for dr in range(NA_ROWS_PER_STEP):
            r = i * NA_ROWS_PER_STEP + dr
            r0 = jnp.clip(r - NA_KH // 2, 0, ROWS - NA_KH)
            for hp in range(NA_HEADS // 2):
                units.append((pl.multiple_of(r * GRID_W, GRID_W), pl.multiple_of(r0 * GRID_W, GRID_W), r - r0, hp))
        scores = []
        for qrow, krow, var, hp in units:
            c0 = hp * LANES
            qt = p_scr[pl.ds(qrow, GRID_W), c0:c0 + LANES]
            qm = jnp.concatenate([jnp.where(first_half, qt, 0.0), jnp.where(first_half, 0.0, qt)], axis=0).astype(BF16)
            scores.append(_dot_nt(qm, h_scr[pl.ds(krow, NA_KEYS), c0:c0 + LANES]))
        probs = []
        for (qrow, krow, var, hp), qk in zip(units, scores):
            bt = bias_ref[hp, var]
            s = jnp.where(bt > 0.5 * NEG_INF, qk + bt, NEG_INF)
            e = jnp.exp(s - jnp.max(s, axis=-1, keepdims=True))
            probs.append((e.astype(BF16), jnp.sum(e, axis=-1, keepdims=True)))
        for (qrow, krow, var, hp), (e, den) in zip(units, probs):
            c0 = hp * LANES
            o = _dot(e, h_scr[pl.ds(krow, NA_KEYS), NA_DIM + c0:NA_DIM + c0 + LANES]) / den
            y_scr[pl.ds(qrow, GRID_W), HG_DIM + c0:HG_DIM + c0 + LANES] = jnp.where(
                first_half, o[0:GRID_W], o[GRID_W:2 * GRID_W]).astype(BF16)
        return c

    lax.fori_loop(0, ROWS // NA_ROWS_PER_STEP, na_body, 0)

    def out_body(i, c):
        r = pl.multiple_of(i * ROW_TILE, ROW_TILE)
        o_ref[pl.ds(r, ROW_TILE), :] = x_ref[pl.ds(r, ROW_TILE), :] + _dot(y_scr[pl.ds(r, ROW_TILE), :], wout_ref[...])
        return c

    lax.fori_loop(0, n_row_tiles, out_body, 0)


def _ffn_kernel(final, x_ref, ng_ref, wgu_ref, wd_ref, fg_ref, o_ref, h_scr, acc_scr):
    x = x_ref[...]
    ms = jnp.mean(x * x, axis=-1, keepdims=True)
    h_scr[...] = (x * lax.rsqrt(ms + EPS) * ng_ref[...]).astype(BF16)
    for c in range(FFN_NCHUNK):
        gu = _dot(h_scr[...], wgu_ref[:, 2 * FFN_CHUNK * c:2 * FFN_CHUNK * (c + 1)])
        gt = gu[:, 0:FFN_CHUNK]
        up = gu[:, FFN_CHUNK:2 * FFN_CHUNK]
        a = (gt * _sigmoid(gt) * up).astype(BF16)
        d = _dot(a, wd_ref[FFN_CHUNK * c:FFN_CHUNK * (c + 1), :])
        if c == 0:
            acc_scr[...] = d
        else:
            acc_scr[...] += d
    y = x_ref[...] + acc_scr[...]
    if final:
        y = y * lax.rsqrt(jnp.mean(y * y, axis=-1, keepdims=True) + EPS) * fg_ref[...]
    o_ref[...] = y


def _resident(shape):
    nd = len(shape)
    return pl.BlockSpec(shape, lambda *_: (0,) * nd, pipeline_mode=pl.Buffered(1))


def _mixer_call(layer, x, ng, win, hlb, hgn, bias, cw, cb, lng, lnb, wout):
    batch = x.shape[0]
    seq_block = pl.BlockSpec((None, SEQ, D_MODEL), lambda b: (b, 0, 0), pipeline_mode=pl.Buffered(1))
    return pl.pallas_call(
        functools.partial(_mixer_kernel, layer),
        grid=(batch,),
        in_specs=[seq_block] + [_resident(a.shape) for a in (ng, win, hlb, hgn, bias, cw, cb, lng, lnb, wout)],
        out_specs=pl.BlockSpec((None, SEQ, D_MODEL), lambda b: (b, 0, 0), pipeline_mode=pl.Buffered(1)),
        out_shape=jax.ShapeDtypeStruct(x.shape, x.dtype),
        scratch_shapes=[
            pltpu.VMEM((SEQ, D_MODEL), BF16),
            pltpu.VMEM((SEQ, D_MODEL), BF16),
            pltpu.VMEM((SEQ, P_COLS), F32),
            pltpu.VMEM((SEQ, HG_HEAD_DIM), F32),
            pltpu.VMEM((2, SEQ, HG_HEAD_DIM), BF16),
            pltpu.VMEM((2, HG_NCHUNK, HG_HEAD_DIM, HG_HEAD_DIM), F32),
            pltpu.VMEM((2, HG_NCHUNK, HG_HEAD_DIM, HG_HEAD_DIM), BF16),
            pltpu.VMEM((2, HG_NCHUNK, SUBLANES, LANES), F32),
            pltpu.VMEM((SEQ + 2 * CONV_PAD, CONV_DIM), F32),
        ],
        compiler_params=pltpu.CompilerParams(dimension_semantics=("arbitrary",), vmem_limit_bytes=VMEM_LIMIT_BYTES),
        name=f"mixer_l{layer}",
    )(x, ng, win, hlb, hgn, bias, cw, cb, lng, lnb, wout)


def _ffn_call(layer, final, x2d, ng, wgu, wd, fg):
    n_tok = x2d.shape[0]
    tile = pl.BlockSpec((FFN_TILE, D_MODEL), lambda i: (i, 0))
    return pl.pallas_call(
        functools.partial(_ffn_kernel, final),
        grid=(n_tok // FFN_TILE,),
        in_specs=[tile] + [_resident(a.shape) for a in (ng, wgu, wd, fg)],
        out_specs=tile,
        out_shape=jax.ShapeDtypeStruct(x2d.shape, x2d.dtype),
        scratch_shapes=[pltpu.VMEM((FFN_TILE, D_MODEL), BF16), pltpu.VMEM((FFN_TILE, D_MODEL), F32)],
        compiler_params=pltpu.CompilerParams(dimension_semantics=("arbitrary",), vmem_limit_bytes=VMEM_LIMIT_BYTES),
        name=f"ffn_l{layer}",
    )(x2d, ng, wgu, wd, fg)


def _regroup_w_in(w):
    hg = w[:, :NA_COL0].reshape(D_MODEL, 5, HG_HEADS, HG_HEAD_DIM).transpose(0, 2, 1, 3).reshape(D_MODEL, NA_COL0)
    return jnp.concatenate([hg, w[:, NA_COL0:]], axis=1)


def _regroup_w_gate_up(w):
    return w.reshape(D_MODEL, 2, FFN_NCHUNK, FFN_CHUNK).transpose(0, 2, 1, 3).reshape(D_MODEL, 2 * FFN_DIM)


def _na_visible():
    q = np.arange(GRID_W)[:, None]
    kc = np.arange(GRID_W)[None, :]
    q_start = np.clip(q - NA_KW // 2, 0, GRID_W - NA_KW)
    return np.tile((kc >= q_start) & (kc < q_start + NA_KW), (1, NA_KH))


def _na_bias_table(rpb):
    n_rel = 2 * NA_KW - 1
    lead = GRID_W - NA_KW
    v = jnp.pad(rpb, ((0, 0), (0, 0), (lead, 2 * GRID_W - lead - n_rel)))
    m = jnp.tile(v, (1, 1, GRID_W))[:, :, :GRID_W * (2 * GRID_W - 1)]
    m = m.reshape(NA_HEADS, 2 * NA_KH - 1, GRID_W, 2 * GRID_W - 1)
    toep = m[:, :, :, GRID_W - 1:2 * GRID_W - 1]
    per_variant = [toep[:, NA_KH - 1 - var:2 * NA_KH - 1 - var].transpose(0, 2, 1, 3).reshape(NA_HEADS, GRID_W, NA_KEYS)
                   for var in range(NA_KH)]
    table = jnp.where(_na_visible()[None, None], jnp.stack(per_variant, axis=1), NEG_INF)
    table = table.reshape(NA_HEADS // 2, 2, NA_KH, GRID_W, NA_KEYS).transpose(0, 2, 1, 3, 4)
    return table.reshape(NA_HEADS // 2, NA_KH, 2 * GRID_W, NA_KEYS)


def kernel(x, mix_norm_g, w_in, hg_lower_bounds, hg_norm_g, na_rpb, conv_w, conv_b, conv_ln_g, conv_ln_b, w_out,
           ffn_norm_g, w_gate_up, w_down, final_norm_g):
    batch, seq, d = x.shape
    assert (seq, d) == (SEQ, D_MODEL) and w_in.shape == (DEPTH, D_MODEL, D_IN)
    hlb = hg_lower_bounds.astype(F32)
    fg = final_norm_g.reshape(1, D_MODEL)
    for l in range(DEPTH):
        win = _regroup_w_in(w_in[l].astype(BF16))
        wout = w_out[l].astype(BF16)
        wgu = _regroup_w_gate_up(w_gate_up[l].astype(BF16))
        wd = w_down[l].astype(BF16)
        bias = _na_bias_table(na_rpb[l].astype(F32))
        x = _mixer_call(l, x, mix_norm_g[l].reshape(1, D_MODEL), win, hlb, hg_norm_g[l].reshape(1, HG_HEAD_DIM), bias,
                        conv_w[l], conv_b[l].reshape(1, CONV_DIM), conv_ln_g[l].reshape(1, CONV_DIM),
                        conv_ln_b[l].reshape(1, CONV_DIM), wout)
        x2d = _ffn_call(l, l == DEPTH - 1, x.reshape(batch * seq, d), ffn_norm_g[l].reshape(1, D_MODEL), wgu, wd, fg)
        x = x2d.reshape(batch, seq, d)
    return x
```

```python
import functools

import numpy as np
import jax
import jax.numpy as jnp
from jax import lax
from jax.experimental import pallas as pl
from jax.experimental.pallas import tpu as pltpu

D_MODEL = 1024
SEQ = 2048
DEPTH = 2
GRID_W = 64
ROWS = SEQ // GRID_W
HG_DIM = 512
HG_HEADS = 4
HG_HEAD_DIM = 128
NA_DIM = 256
NA_HEADS = 4
NA_HEAD_DIM = 64
NA_KH = 8
NA_KW = 16
CONV_DIM = 256
CONV_WIDTH = 31
D_IN = 5 * HG_DIM + 3 * NA_DIM + 2 * CONV_DIM
FFN_DIM = 2816
EPS = 1e-6
NEG_INF = -1e30

LANES = 128
SUBLANES = 8
VMEM_LIMIT_BYTES = 60 * 1024 * 1024

ROW_TILE = 256
HG_CHUNK = 64
HG_CHUNK_LOG2 = HG_CHUNK.bit_length() - 1
assert HG_CHUNK == 1 << HG_CHUNK_LOG2
HG_NCHUNK = SEQ // HG_CHUNK
HG_TILE = 256
HG_COLS = 5 * HG_HEAD_DIM
NA_COL0 = 5 * HG_DIM
CV_COL0 = NA_COL0 + 3 * NA_DIM
NA_KEYS = NA_KH * GRID_W
NA_NREL_H = 2 * NA_KH - 1
NA_NREL_W = 2 * NA_KW - 1
RPB_LEAD = GRID_W - NA_KW
NA_ROWS_PER_STEP = 2
CONV_TILE = 64
CONV_PAD = 16
P_COLS = HG_COLS
assert P_COLS >= 2 * CONV_DIM and P_COLS >= NA_DIM
FFN_TILE = 512
FFN_CHUNK = 256
FFN_NCHUNK = FFN_DIM // FFN_CHUNK

F32 = jnp.float32
BF16 = jnp.bfloat16


def _sigmoid(x):
    return 1.0 / (1.0 + jnp.exp(-x))


def _dot(a, b):
    return jnp.dot(a, b, preferred_element_type=F32)


def _dot_nt(a, b):
    return lax.dot_general(a, b, (((1,), (1,)), ((), ())), preferred_element_type=F32)


def _dot_tn(a, b):
    return lax.dot_general(a, b, (((0,), (0,)), ((), ())), preferred_element_type=F32)


def _split3(x):
    hi = x.astype(BF16)
    r1 = x - hi.astype(F32)
    mid = r1.astype(BF16)
    lo = (r1 - mid.astype(F32)).astype(BF16)
    return hi, mid, lo


def _build_na_bias(rpbp_ref, bias_ref):
    q_i = lax.broadcasted_iota(jnp.int32, (GRID_W, LANES), 0)
    l_i = lax.broadcasted_iota(jnp.int32, (GRID_W, LANES), 1)
    kc = jnp.bitwise_and(l_i, GRID_W - 1)
    q_start = jnp.clip(q_i - NA_KW // 2, 0, GRID_W - NA_KW)
    visible = (kc >= q_start) & (kc < q_start + NA_KW)
    left = l_i < GRID_W
    q_bits = [jnp.bitwise_and(q_i, 1 << b) != 0 for b in range(GRID_W.bit_length() - 1)]

    def skew(x, base):
        y = pltpu.roll(x, base, axis=1)
        for b, bit in enumerate(q_bits):
            y = jnp.where(bit, pltpu.roll(y, 1 << b, axis=1), y)
        return y

    def head_body(h, c):
        even, odd = [], []
        for ro in range(NA_NREL_H):
            row = jnp.broadcast_to(rpbp_ref[pl.ds(h * NA_NREL_H + ro, 1), :], (GRID_W, LANES))
            even.append(skew(row, LANES - (GRID_W - 1)))
            odd.append(skew(row, 1))
        hp = h // 2
        q0 = pl.multiple_of((h % 2) * GRID_W, GRID_W)
        for var in range(NA_KH):
            for kp in range(NA_KH // 2):
                ro = 2 * kp - var + NA_KH - 1
                tile = jnp.where(visible, jnp.where(left, even[ro], odd[ro + 1]), NEG_INF)
                bias_ref[hp, var, pl.ds(q0, GRID_W), kp * LANES:(kp + 1) * LANES] = tile
        return c

    lax.fori_loop(0, NA_HEADS, head_body, 0)


def _mixer_kernel(layer, x_ref, ng_ref, win_ref, hlb_ref, hgn_ref, rpbp_ref, cw_ref, cb_ref, lng_ref, lnb_ref,
                  wout_ref, o_ref, h_scr, y_scr, p_scr, oi_scr, qa_scr, d_scr, st_scr, dec_scr, upad_scr, bias_ref,
                  wh_scr):
    n_row_tiles = SEQ // ROW_TILE

    @pl.when(pl.program_id(0) == 0)
    def _():
        _build_na_bias(rpbp_ref, bias_ref)

    def norm_body(i, c):
        r = pl.multiple_of(i * ROW_TILE, ROW_TILE)
        xt = x_ref[pl.ds(r, ROW_TILE), :]
        ms = jnp.mean(xt * xt, axis=-1, keepdims=True)
        h_scr[pl.ds(r, ROW_TILE), :] = (xt * lax.rsqrt(ms + EPS) * ng_ref[...]).astype(BF16)
        return c

    lax.fori_loop(0, n_row_tiles, norm_body, 0)

    def project(w_ref, col0, ncols):
        def body(i, c):
            r = pl.multiple_of(i * ROW_TILE, ROW_TILE)
            p_scr[pl.ds(r, ROW_TILE), 0:ncols] = _dot(h_scr[pl.ds(r, ROW_TILE), :], w_ref[:, col0:col0 + ncols])
            return c
        lax.fori_loop(0, n_row_tiles, body, 0)

    hlb = [hlb_ref[d] for d in range(DEPTH)]
    hmax = functools.reduce(jnp.maximum, hlb)
    hexp = [jnp.exp(v - hmax) for v in hlb]
    hsum = functools.reduce(lambda a, b: a + b, hexp)
    hsm = [e / hsum for e in hexp]
    lbs = functools.reduce(lambda a, b: a + b, hsm[:layer + 1]) - hsm[0]

    L = HG_CHUNK
    TL = HG_TILE
    cpt = TL // L
    row_c = lax.broadcasted_iota(jnp.int32, (TL, TL), 0)
    col_c = lax.broadcasted_iota(jnp.int32, (TL, TL), 1)
    same_chunk = jnp.right_shift(row_c, HG_CHUNK_LOG2) == jnp.right_shift(col_c, HG_CHUNK_LOG2)
    lower = same_chunk & (row_c >= col_c)
    upper = same_chunk & (col_c >= row_c)
    lower_b = jnp.where(lower, 1.0, 0.0).astype(BF16)
    upper_b = jnp.where(upper, 1.0, 0.0).astype(BF16)

    def chunk_rows(a, row):
        return [a[c * L + row:c * L + row + 1, :] for c in range(cpt)]

    def spread(rows):
        return jnp.concatenate([jnp.broadcast_to(rw, (L, LANES)) for rw in rows], axis=0)

    for hd in range(HG_HEADS):
        sl = slice(hd * HG_HEAD_DIM, (hd + 1) * HG_HEAD_DIM)
        for grp in range(HG_COLS // HG_HEAD_DIM):
            wh_scr[:, grp * HG_HEAD_DIM:(grp + 1) * HG_HEAD_DIM] = win_ref[:, grp * HG_DIM + sl.start:grp * HG_DIM + sl.stop]
        project(wh_scr, 0, HG_COLS)
        directions = (
            (HG_HEAD_DIM, lbs[0:1, sl], lower, lower_b, L // 2 - 1, L - 1),
            (2 * HG_HEAD_DIM, lbs[1:2, sl], upper, upper_b, L // 2, 0),
        )

        def hg_local_body(i, c):
            r = pl.multiple_of(i * TL, TL)
            q_raw = p_scr[pl.ds(r, TL), 0:HG_HEAD_DIM]
            q = q_raw * _sigmoid(q_raw)
            vb = p_scr[pl.ds(r, TL), 3 * HG_HEAD_DIM:4 * HG_HEAD_DIM].astype(BF16)
            gates = []
            for zcol, lb, mask, cum_op, ref_row, last_row in directions:
                z = p_scr[pl.ds(r, TL), zcol:zcol + HG_HEAD_DIM]
                t = jnp.exp(-jnp.abs(z))
                big = 1.0 / (1.0 + t)
                small = t * big
                nonneg = z >= 0.0
                f = lb + (1.0 - lb) * jnp.where(nonneg, big, small)
                k = (1.0 - lb) * jnp.where(nonneg, small, big)
                gates.append((k, _dot(cum_op, jnp.concatenate(_split3(jnp.log(f)), axis=1))))
            scaled = []
            for d, ((zcol, lb, mask, cum_op, ref_row, last_row), (k, c3)) in enumerate(zip(directions, gates)):
                cum = c3[:, 0:LANES] + c3[:, LANES:2 * LANES] + c3[:, 2 * LANES:3 * LANES]
                ref_rows = chunk_rows(cum, ref_row)
                last_rows = chunk_rows(cum, last_row)
                rel = cum - spread(ref_rows)
                q_in = q * jnp.exp(rel)
                k_in = k * jnp.exp(-rel)
                qa_scr[d, pl.ds(r, TL), :] = (q_in * spread([jnp.exp(rw) for rw in ref_rows])).astype(BF16)
                k_d = (k_in * spread([jnp.exp(lr - rr) for lr, rr in zip(last_rows, ref_rows)])).astype(BF16)
                for cc in range(cpt):
                    dec_scr[d, i * cpt + cc] = jnp.broadcast_to(jnp.exp(last_rows[cc]), (SUBLANES, LANES))
                scaled.append((_dot_nt(q_in.astype(BF16), k_in.astype(BF16)), k_d))
            o_intra = None
            for d, ((zcol, lb, mask, cum_op, ref_row, last_row), (qk, k_d)) in enumerate(zip(directions, scaled)):
                o_dir = _dot(jnp.where(mask, qk, 0.0).astype(BF16), vb)
                o_intra = o_dir if o_intra is None else o_intra + o_dir
                for cc in range(cpt):
                    d_scr[d, i * cpt + cc] = _dot_tn(vb[cc * L:(cc + 1) * L], k_d[cc * L:(cc + 1) * L])
            oi_scr[pl.ds(r, TL), :] = o_intra
            return c

        lax.fori_loop(0, SEQ // TL, hg_local_body, 0)

        def hg_scan_body(n, carry):
            s_f, s_b = carry
            m = HG_NCHUNK - 1 - n
            st_scr[0, n] = s_f.astype(BF16)
            st_scr[1, m] = s_b.astype(BF16)
            s_f = s_f * dec_scr[0, n][0:1, :] + d_scr[0, n]
            s_b = s_b * dec_scr[1, m][0:1, :] + d_scr[1, m]
            return s_f, s_b

        s_zero = jnp.zeros((HG_HEAD_DIM, HG_HEAD_DIM), F32)
        lax.fori_loop(0, HG_NCHUNK, hg_scan_body, (s_zero, s_zero))

        def hg_out_body(i, c):
            r = pl.multiple_of(i * TL, TL)
            inter = []
            for cc in range(cpt):
                n = i * cpt + cc
                rc = pl.multiple_of(r + cc * L, L)
                qa = jnp.concatenate([qa_scr[0, pl.ds(rc, L), :], qa_scr[1, pl.ds(rc, L), :]], axis=1)
                st = jnp.concatenate([st_scr[0, n], st_scr[1, n]], axis=1)
                inter.append(_dot_nt(qa, st))
            o = oi_scr[pl.ds(r, TL), :] + jnp.concatenate(inter, axis=0)
            o = o * lax.rsqrt(jnp.mean(o * o, axis=-1, keepdims=True) + EPS) * hgn_ref[...]
            gate = p_scr[pl.ds(r, TL), 4 * HG_HEAD_DIM:5 * HG_HEAD_DIM]
            y_scr[pl.ds(r, TL), sl] = (o * (gate * _sigmoid(gate))).astype(BF16)
            return c

        lax.fori_loop(0, SEQ // TL, hg_out_body, 0, unroll=2)

    project(win_ref, CV_COL0, 2 * CONV_DIM)
    upad_scr[0:CONV_PAD, :] = jnp.zeros((CONV_PAD, CONV_DIM), F32)
    upad_scr[CONV_PAD + SEQ:CONV_PAD + SEQ + CONV_PAD, :] = jnp.zeros((CONV_PAD, CONV_DIM), F32)

    def glu_body(i, c):
        r = pl.multiple_of(i * ROW_TILE, ROW_TILE)
        a = p_scr[pl.ds(r, ROW_TILE), 0:CONV_DIM]
        gate = p_scr[pl.ds(r, ROW_TILE), CONV_DIM:2 * CONV_DIM]
        upad_scr[pl.ds(CONV_PAD + r, ROW_TILE), :] = a * _sigmoid(gate)
        return c

    lax.fori_loop(0, n_row_tiles, glu_body, 0)

    def conv_body(i, c):
        r = pl.multiple_of(i * CONV_TILE, CONV_TILE)
        off = CONV_PAD - CONV_WIDTH // 2
        groups = -(-CONV_WIDTH // SUBLANES)
        win_rows = CONV_TILE + 2 * CONV_PAD
        halves = []
        for c0 in range(0, CONV_DIM, LANES):
            win = upad_scr[pl.ds(r, win_rows), c0:c0 + LANES]
            acc = jnp.zeros((CONV_TILE, LANES), F32) + cb_ref[:, c0:c0 + LANES]
            for s in range(SUBLANES):
                k = off + s
                shifted = win if k % win_rows == 0 else pltpu.roll(win, win_rows - k, axis=0)
                for a in range(groups):
                    j = SUBLANES * a + s
                    if j < CONV_WIDTH:
                        acc = acc + shifted[SUBLANES * a:SUBLANES * a + CONV_TILE, :] * cw_ref[j:j + 1, c0:c0 + LANES]
            halves.append(acc)
        acc = jnp.concatenate(halves, axis=1)
        mu = jnp.mean(acc, axis=-1, keepdims=True)
        d = acc - mu
        var = jnp.mean(d * d, axis=-1, keepdims=True)
        u = d * lax.rsqrt(var + EPS) * lng_ref[...] + lnb_ref[...]
        y_scr[pl.ds(r, CONV_TILE), HG_DIM + NA_DIM:D_MODEL] = (u * _sigmoid(u)).astype(BF16)
        return c

    lax.fori_loop(0, SEQ // CONV_TILE, conv_body, 0)

    scale = NA_HEAD_DIM ** -0.5

    def na_proj_body(i, c):
        r = pl.multiple_of(i * ROW_TILE, ROW_TILE)
        res = _dot(h_scr[pl.ds(r, ROW_TILE), :], win_ref[:, NA_COL0:NA_COL0 + 3 * NA_DIM])
        p_scr[pl.ds(r, ROW_TILE), 0:NA_DIM] = res[:, 0:NA_DIM] * scale
        h_scr[pl.ds(r, ROW_TILE), 0:2 * NA_DIM] = res[:, NA_DIM:3 * NA_DIM].astype(BF16)
        return c

    lax.fori_loop(0, n_row_tiles, na_proj_body, 0)
    lane = lax.broadcasted_iota(jnp.int32, (1, LANES), 1)
    first_half = lane < NA_HEAD_DIM

    def na_body(i, c):
        units = []
        for dr in range(NA_ROWS_PER_STEP):
            r = i * NA_ROWS_PER_STEP + dr
            r0 = jnp.clip(r - NA_KH // 2, 0, ROWS - NA_KH)
            for hp in range(NA_HEADS // 2):
                units.append((pl.multiple_of(r * GRID_W, GRID_W), pl.multiple_of(r0 * GRID_W, GRID_W), r - r0, hp))
        scores = []
        for qrow, krow, var, hp in units:
            c0 = hp * LANES
            qt = p_scr[pl.ds(qrow, GRID_W), c0:c0 + LANES]
            qm = jnp.concatenate([jnp.where(first_half, qt, 0.0), jnp.where(first_half, 0.0, qt)], axis=0).astype(BF16)
            scores.append(_dot_nt(qm, h_scr[pl.ds(krow, NA_KEYS), c0:c0 + LANES]))
        probs = []
        for (qrow, krow, var, hp), qk in zip(units, scores):
            bt = bias_ref[hp, var]
            s = jnp.where(bt > 0.5 * NEG_INF, qk + bt, NEG_INF)
            e = jnp.exp(s - jnp.max(s, axis=-1, keepdims=True))
            probs.append((e.astype(BF16), jnp.sum(e, axis=-1, keepdims=True)))
        for (qrow, krow, var, hp), (e, den) in zip(units, probs):
            c0 = hp * LANES
            o = _dot(e, h_scr[pl.ds(krow, NA_KEYS), NA_DIM + c0:NA_DIM + c0 + LANES]) / den
            y_scr[pl.ds(qrow, GRID_W), HG_DIM + c0:HG_DIM + c0 + LANES] = jnp.where(
                first_half, o[0:GRID_W], o[GRID_W:2 * GRID_W]).astype(BF16)
        return c

    lax.fori_loop(0, ROWS // NA_ROWS_PER_STEP, na_body, 0)

    def out_body(i, c):
        r = pl.multiple_of(i * ROW_TILE, ROW_TILE)
        o_ref[pl.ds(r, ROW_TILE), :] = x_ref[pl.ds(r, ROW_TILE), :] + _dot(y_scr[pl.ds(r, ROW_TILE), :], wout_ref[...])
        return c

    lax.fori_loop(0, n_row_tiles, out_body, 0)


def _ffn_kernel(final, x_ref, ng_ref, wgu_ref, wd_ref, fg_ref, o_ref, h_scr, acc_scr):
    x = x_ref[...]
    ms = jnp.mean(x * x, axis=-1, keepdims=True)
    h_scr[...] = (x * lax.rsqrt(ms + EPS) * ng_ref[...]).astype(BF16)
    for c in range(FFN_NCHUNK):
        gt = _dot(h_scr[...], wgu_ref[:, FFN_CHUNK * c:FFN_CHUNK * (c + 1)])
        up = _dot(h_scr[...], wgu_ref[:, FFN_DIM + FFN_CHUNK * c:FFN_DIM + FFN_CHUNK * (c + 1)])
        a = (gt * _sigmoid(gt) * up).astype(BF16)
        d = _dot(a, wd_ref[FFN_CHUNK * c:FFN_CHUNK * (c + 1), :])
        if c == 0:
            acc_scr[...] = d
        else:
            acc_scr[...] += d
    y = x_ref[...] + acc_scr[...]
    if final:
        y = y * lax.rsqrt(jnp.mean(y * y, axis=-1, keepdims=True) + EPS) * fg_ref[...]
    o_ref[...] = y


def _resident(shape):
    nd = len(shape)
    return pl.BlockSpec(shape, lambda *_: (0,) * nd, pipeline_mode=pl.Buffered(1))


def _layer_slice(layer, shape):
    nd = len(shape)
    return pl.BlockSpec((None,) + tuple(shape[1:]), lambda *_: (layer,) + (0,) * (nd - 1), pipeline_mode=pl.Buffered(1))


def _mixer_call(layer, x, ng, win, hlb, hgn, rpbp, cw, cb, lng, lnb, wout):
    batch = x.shape[0]
    seq_block = pl.BlockSpec((None, SEQ, D_MODEL), lambda b: (b, 0, 0), pipeline_mode=pl.Buffered(1))
    stacked = (ng, win, None, hgn, rpbp, cw, cb, lng, lnb, wout)
    return pl.pallas_call(
        functools.partial(_mixer_kernel, layer),
        grid=(batch,),
        in_specs=[seq_block] + [_resident(hlb.shape) if a is None else _layer_slice(layer, a.shape) for a in stacked],
        out_specs=pl.BlockSpec((None, SEQ, D_MODEL), lambda b: (b, 0, 0), pipeline_mode=pl.Buffered(1)),
        out_shape=jax.ShapeDtypeStruct(x.shape, x.dtype),
        scratch_shapes=[
            pltpu.VMEM((SEQ, D_MODEL), BF16),
            pltpu.VMEM((SEQ, D_MODEL), BF16),
            pltpu.VMEM((SEQ, P_COLS), F32),
            pltpu.VMEM((SEQ, HG_HEAD_DIM), F32),
            pltpu.VMEM((2, SEQ, HG_HEAD_DIM), BF16),
            pltpu.VMEM((2, HG_NCHUNK, HG_HEAD_DIM, HG_HEAD_DIM), F32),
            pltpu.VMEM((2, HG_NCHUNK, HG_HEAD_DIM, HG_HEAD_DIM), BF16),
            pltpu.VMEM((2, HG_NCHUNK, SUBLANES, LANES), F32),
            pltpu.VMEM((SEQ + 2 * CONV_PAD, CONV_DIM), F32),
            pltpu.VMEM((NA_HEADS // 2, NA_KH, 2 * GRID_W, NA_KEYS), F32),
            pltpu.VMEM((D_MODEL, HG_COLS), BF16),
        ],
        compiler_params=pltpu.CompilerParams(dimension_semantics=("arbitrary",), vmem_limit_bytes=VMEM_LIMIT_BYTES),
        name=f"mixer_l{layer}",
    )(x, ng, win, hlb, hgn, rpbp, cw, cb, lng, lnb, wout)


def _ffn_call(layer, final, x2d, ng, wgu, wd, fg):
    n_tok = x2d.shape[0]
    tile = pl.BlockSpec((FFN_TILE, D_MODEL), lambda i: (i, 0))
    return pl.pallas_call(
        functools.partial(_ffn_kernel, final),
        grid=(n_tok // FFN_TILE,),
        in_specs=[tile] + [_layer_slice(layer, a.shape) for a in (ng, wgu, wd)] + [_resident(fg.shape)],
        out_specs=tile,
        out_shape=jax.ShapeDtypeStruct(x2d.shape, x2d.dtype),
        scratch_shapes=[pltpu.VMEM((FFN_TILE, D_MODEL), BF16), pltpu.VMEM((FFN_TILE, D_MODEL), F32)],
        compiler_params=pltpu.CompilerParams(dimension_semantics=("arbitrary",), vmem_limit_bytes=VMEM_LIMIT_BYTES),
        name=f"ffn_l{layer}",
    )(x2d, ng, wgu, wd, fg)


def _pad_rpb(rpb):
    v = jnp.pad(rpb, ((0, 0), (0, 0), (0, 0), (RPB_LEAD, LANES - RPB_LEAD - NA_NREL_W)))
    return v.reshape(DEPTH, NA_HEADS * NA_NREL_H, LANES)


def kernel(x, mix_norm_g, w_in, hg_lower_bounds, hg_norm_g, na_rpb, conv_w, conv_b, conv_ln_g, conv_ln_b, w_out,
           ffn_norm_g, w_gate_up, w_down, final_norm_g):
    batch, seq, d = x.shape
    assert (seq, d) == (SEQ, D_MODEL) and w_in.shape == (DEPTH, D_MODEL, D_IN)
    row = lambda p: p.astype(F32).reshape(DEPTH, 1, p.shape[-1])
    win, wout, wgu, wd = (w.astype(BF16) for w in (w_in, w_out, w_gate_up, w_down))
    hlb = hg_lower_bounds.astype(F32)
    rpbp = _pad_rpb(na_rpb.astype(F32))
    mix_g, hgn, cb, lng, lnb, ffn_g = (row(p) for p in (mix_norm_g, hg_norm_g, conv_b, conv_ln_g, conv_ln_b, ffn_norm_g))
    fg = final_norm_g.astype(F32).reshape(1, D_MODEL)
    for l in range(DEPTH):
        x = _mixer_call(l, x, mix_g, win, hlb, hgn, rpbp, conv_w.astype(F32), cb, lng, lnb, wout)
        x2d = _ffn_call(l, l == DEPTH - 1, x.reshape(batch * seq, d), ffn_g, wgu, wd, fg)
        x = x2d.reshape(batch, seq, d)
    return x
```

```python
import functools

import numpy as np
import jax
import jax.numpy as jnp
from jax import lax
from jax.experimental import pallas as pl
from jax.experimental.pallas import tpu as pltpu

D_MODEL = 1024
SEQ = 2048
DEPTH = 2
GRID_W = 64
ROWS = SEQ // GRID_W
HG_DIM = 512
HG_HEADS = 4
HG_HEAD_DIM = 128
NA_DIM = 256
NA_HEADS = 4
NA_HEAD_DIM = 64
NA_KH = 8
NA_KW = 16
CONV_DIM = 256
CONV_WIDTH = 31
D_IN = 5 * HG_DIM + 3 * NA_DIM + 2 * CONV_DIM
FFN_DIM = 2816
EPS = 1e-6
NEG_INF = -1e30

LANES = 128
SUBLANES = 8
VMEM_LIMIT_BYTES = 60 * 1024 * 1024

ROW_TILE = 1024
HG_CHUNK = 64
HG_CHUNK_LOG2 = HG_CHUNK.bit_length() - 1
assert HG_CHUNK == 1 << HG_CHUNK_LOG2
HG_NCHUNK = SEQ // HG_CHUNK
HG_TILE = 256
HG_OUT_TILE = 512
HG_COLS = 5 * HG_HEAD_DIM
NA_COL0 = 5 * HG_DIM
CV_COL0 = NA_COL0 + 3 * NA_DIM
NA_KEYS = NA_KH * GRID_W
NA_NREL_H = 2 * NA_KH - 1
NA_NREL_W = 2 * NA_KW - 1
RPB_LEAD = GRID_W - NA_KW
RPB_ROWS = 16
assert RPB_ROWS >= NA_NREL_H and RPB_ROWS % SUBLANES == 0
NA_ROWS_PER_STEP = 2
CONV_TILE = 64
CONV_PAD = 16
P_COLS = HG_COLS
assert P_COLS >= 2 * CONV_DIM and P_COLS >= NA_DIM
FFN_TILE = 512
FFN_CHUNK = 256
FFN_NCHUNK = FFN_DIM // FFN_CHUNK

F32 = jnp.float32
BF16 = jnp.bfloat16


def _sigmoid(x):
    return 1.0 / (1.0 + jnp.exp(-x))


def _dot(a, b):
    return jnp.dot(a, b, preferred_element_type=F32)


def _dot_nt(a, b):
    return lax.dot_general(a, b, (((1,), (1,)), ((), ())), preferred_element_type=F32)


def _dot_tn(a, b):
    return lax.dot_general(a, b, (((0,), (0,)), ((), ())), preferred_element_type=F32)


def _split3(x):
    hi = x.astype(BF16)
    r1 = x - hi.astype(F32)
    mid = r1.astype(BF16)
    lo = (r1 - mid.astype(F32)).astype(BF16)
    return hi, mid, lo


def _build_na_bias(rpbp_ref, bias_ref):
    q_i = lax.broadcasted_iota(jnp.int32, (GRID_W, LANES), 0)
    l_i = lax.broadcasted_iota(jnp.int32, (GRID_W, LANES), 1)
    kc = jnp.bitwise_and(l_i, GRID_W - 1)
    q_start = jnp.clip(q_i - NA_KW // 2, 0, GRID_W - NA_KW)
    visible = (kc >= q_start) & (kc < q_start + NA_KW)
    left = l_i < GRID_W
    q_all = jnp.bitwise_and(lax.broadcasted_iota(jnp.int32, (RPB_ROWS * GRID_W, LANES), 0), GRID_W - 1)

    def head_body(h, c):
        rows = rpbp_ref[pl.ds(pl.multiple_of(h * RPB_ROWS, RPB_ROWS), RPB_ROWS), :]
        x = jnp.concatenate([jnp.broadcast_to(rows[ro:ro + 1, :], (GRID_W, LANES)) for ro in range(RPB_ROWS)], axis=0)
        even = pltpu.roll(x, LANES - (GRID_W - 1), axis=1)
        for b in range(GRID_W.bit_length() - 1):
            even = jnp.where(jnp.bitwise_and(q_all, 1 << b) != 0, pltpu.roll(even, 1 << b, axis=1), even)
        odd = pltpu.roll(even, GRID_W, axis=1)
        hp = h // 2
        q0 = pl.multiple_of((h % 2) * GRID_W, GRID_W)
        for var in range(NA_KH):
            for kp in range(NA_KH // 2):
                ro = 2 * kp - var + NA_KH - 1
                tile = jnp.where(left, even[ro * GRID_W:(ro + 1) * GRID_W], odd[(ro + 1) * GRID_W:(ro + 2) * GRID_W])
                bias_ref[hp, var, pl.ds(q0, GRID_W), kp * LANES:(kp + 1) * LANES] = jnp.where(visible, tile, NEG_INF)
        return c

    lax.fori_loop(0, NA_HEADS, head_body, 0)


def _mixer_kernel(layer, x_ref, ng_ref, win_ref, hlb_ref, hgn_ref, rpbp_ref, cw_ref, cb_ref, lng_ref, lnb_ref,
                  wout_ref, o_ref, h_scr, y_scr, p_scr, oi_scr, qa_scr, d_scr, st_scr, dec_scr, upad_scr, bias_ref,
                  wh_scr, qk_scr):
    n_row_tiles = SEQ // ROW_TILE

    @pl.when(pl.program_id(0) == 0)
    def _():
        _build_na_bias(rpbp_ref, bias_ref)

    def norm_body(i, c):
        r = pl.multiple_of(i * ROW_TILE, ROW_TILE)
        xt = x_ref[pl.ds(r, ROW_TILE), :]
        ms = jnp.mean(xt * xt, axis=-1, keepdims=True)
        h_scr[pl.ds(r, ROW_TILE), :] = (xt * lax.rsqrt(ms + EPS) * ng_ref[...]).astype(BF16)
        return c

    lax.fori_loop(0, n_row_tiles, norm_body, 0)

    def project(w_ref, col0, ncols):
        def body(i, c):
            r = pl.multiple_of(i * ROW_TILE, ROW_TILE)
            p_scr[pl.ds(r, ROW_TILE), 0:ncols] = _dot(h_scr[pl.ds(r, ROW_TILE), :], w_ref[:, col0:col0 + ncols])
            return c
        lax.fori_loop(0, n_row_tiles, body, 0)

    hlb = [hlb_ref[d] for d in range(DEPTH)]
    hmax = functools.reduce(jnp.maximum, hlb)
    hexp = [jnp.exp(v - hmax) for v in hlb]
    hsum = functools.reduce(lambda a, b: a + b, hexp)
    hsm = [e / hsum for e in hexp]
    lbs = functools.reduce(lambda a, b: a + b, hsm[:layer + 1]) - hsm[0]

    L = HG_CHUNK
    TL = HG_TILE
    cpt = TL // L
    row_c = lax.broadcasted_iota(jnp.int32, (TL, TL), 0)
    col_c = lax.broadcasted_iota(jnp.int32, (TL, TL), 1)
    same_chunk = jnp.right_shift(row_c, HG_CHUNK_LOG2) == jnp.right_shift(col_c, HG_CHUNK_LOG2)
    lower = same_chunk & (row_c >= col_c)
    upper = same_chunk & (col_c >= row_c)
    lower_b = jnp.where(lower, 1.0, 0.0).astype(BF16)
    upper_b = jnp.where(upper, 1.0, 0.0).astype(BF16)

    def chunk_rows(a, row):
        return [a[c * L + row:c * L + row + 1, :] for c in range(cpt)]

    def spread(rows):
        return jnp.concatenate([jnp.broadcast_to(rw, (L, LANES)) for rw in rows], axis=0)

    for hd in range(HG_HEADS):
        sl = slice(hd * HG_HEAD_DIM, (hd + 1) * HG_HEAD_DIM)
        for grp in range(HG_COLS // HG_HEAD_DIM):
            wh_scr[:, grp * HG_HEAD_DIM:(grp + 1) * HG_HEAD_DIM] = win_ref[:, grp * HG_DIM + sl.start:grp * HG_DIM + sl.stop]
        project(wh_scr, 0, HG_COLS)
        directions = (
            (HG_HEAD_DIM, lbs[0:1, sl], lower, lower_b, L // 2 - 1, L - 1),
            (2 * HG_HEAD_DIM, lbs[1:2, sl], upper, upper_b, L // 2, 0),
        )

        def hg_local_body(i, c):
            r = pl.multiple_of(i * TL, TL)
            q_raw = p_scr[pl.ds(r, TL), 0:HG_HEAD_DIM]
            q = q_raw * _sigmoid(q_raw)
            vb = p_scr[pl.ds(r, TL), 3 * HG_HEAD_DIM:4 * HG_HEAD_DIM].astype(BF16)
            gates = []
            for zcol, lb, mask, cum_op, ref_row, last_row in directions:
                z = p_scr[pl.ds(r, TL), zcol:zcol + HG_HEAD_DIM]
                t = jnp.exp(-jnp.abs(z))
                big = 1.0 / (1.0 + t)
                small = t * big
                nonneg = z >= 0.0
                f = lb + (1.0 - lb) * jnp.where(nonneg, big, small)
                k = (1.0 - lb) * jnp.where(nonneg, small, big)
                gates.append((k, _dot(cum_op, jnp.concatenate(_split3(jnp.log(f)), axis=1))))
            scaled = []
            for d, ((zcol, lb, mask, cum_op, ref_row, last_row), (k, c3)) in enumerate(zip(directions, gates)):
                cum = c3[:, 0:LANES] + c3[:, LANES:2 * LANES] + c3[:, 2 * LANES:3 * LANES]
                ref_rows = chunk_rows(cum, ref_row)
                last_rows = chunk_rows(cum, last_row)
                rel = cum - spread(ref_rows)
                q_in = q * jnp.exp(rel)
                k_in = k * jnp.exp(-rel)
                qa_scr[d, pl.ds(r, TL), :] = (q_in * spread([jnp.exp(rw) for rw in ref_rows])).astype(BF16)
                k_d = (k_in * spread([jnp.exp(lr - rr) for lr, rr in zip(last_rows, ref_rows)])).astype(BF16)
                for cc in range(cpt):
                    dec_scr[d, i * cpt + cc] = jnp.broadcast_to(jnp.exp(last_rows[cc]), (SUBLANES, LANES))
                scaled.append((_dot_nt(q_in.astype(BF16), k_in.astype(BF16)), k_d))
            o_intra = None
            for d, ((zcol, lb, mask, cum_op, ref_row, last_row), (qk, k_d)) in enumerate(zip(directions, scaled)):
                o_dir = _dot(jnp.where(mask, qk, 0.0).astype(BF16), vb)
                o_intra = o_dir if o_intra is None else o_intra + o_dir
                for cc in range(cpt):
                    d_scr[d, i * cpt + cc] = _dot_tn(vb[cc * L:(cc + 1) * L], k_d[cc * L:(cc + 1) * L])
            oi_scr[pl.ds(r, TL), :] = o_intra
            return c

        lax.fori_loop(0, SEQ // TL, hg_local_body, 0)

        def hg_scan_body(n, carry):
            s_f, s_b = carry
            m = HG_NCHUNK - 1 - n
            st_scr[0, n] = s_f.astype(BF16)
            st_scr[1, m] = s_b.astype(BF16)
            s_f = s_f * dec_scr[0, n][0:1, :] + d_scr[0, n]
            s_b = s_b * dec_scr[1, m][0:1, :] + d_scr[1, m]
            return s_f, s_b

        s_zero = jnp.zeros((HG_HEAD_DIM, HG_HEAD_DIM), F32)
        lax.fori_loop(0, HG_NCHUNK, hg_scan_body, (s_zero, s_zero))

        def hg_out_body(i, c):
            r = pl.multiple_of(i * HG_OUT_TILE, HG_OUT_TILE)
            inter = []
            for cc in range(HG_OUT_TILE // L):
                n = i * (HG_OUT_TILE // L) + cc
                rc = pl.multiple_of(r + cc * L, L)
                qa = jnp.concatenate([qa_scr[0, pl.ds(rc, L), :], qa_scr[1, pl.ds(rc, L), :]], axis=1)
                st = jnp.concatenate([st_scr[0, n], st_scr[1, n]], axis=1)
                inter.append(_dot_nt(qa, st))
            o = oi_scr[pl.ds(r, HG_OUT_TILE), :] + jnp.concatenate(inter, axis=0)
            o = o * lax.rsqrt(jnp.mean(o * o, axis=-1, keepdims=True) + EPS) * hgn_ref[...]
            gate = p_scr[pl.ds(r, HG_OUT_TILE), 4 * HG_HEAD_DIM:5 * HG_HEAD_DIM]
            y_scr[pl.ds(r, HG_OUT_TILE), sl] = (o * (gate * _sigmoid(gate))).astype(BF16)
            return c

        lax.fori_loop(0, SEQ // HG_OUT_TILE, hg_out_body, 0, unroll=2)

    project(win_ref, CV_COL0, 2 * CONV_DIM)
    upad_scr[0:CONV_PAD, :] = jnp.zeros((CONV_PAD, CONV_DIM), F32)
    upad_scr[CONV_PAD + SEQ:CONV_PAD + SEQ + CONV_PAD, :] = jnp.zeros((CONV_PAD, CONV_DIM), F32)

    def glu_body(i, c):
        r = pl.multiple_of(i * ROW_TILE, ROW_TILE)
        a = p_scr[pl.ds(r, ROW_TILE), 0:CONV_DIM]
        gate = p_scr[pl.ds(r, ROW_TILE), CONV_DIM:2 * CONV_DIM]
        upad_scr[pl.ds(CONV_PAD + r, ROW_TILE), :] = a * _sigmoid(gate)
        return c

    lax.fori_loop(0, n_row_tiles, glu_body, 0)

    def conv_body(i, c):
        r = pl.multiple_of(i * CONV_TILE, CONV_TILE)
        off = CONV_PAD - CONV_WIDTH // 2
        groups = -(-CONV_WIDTH // SUBLANES)
        win_rows = CONV_TILE + 2 * CONV_PAD
        for c0 in range(0, CONV_DIM, LANES):
            win = upad_scr[pl.ds(r, win_rows), c0:c0 + LANES]
            acc = jnp.zeros((CONV_TILE, LANES), F32) + cb_ref[:, c0:c0 + LANES]
            shifted = win
            for s in range(SUBLANES):
                step = off if s == 0 else 1
                if step:
                    shifted = pltpu.roll(shifted, win_rows - step, axis=0)
                for a in range(groups):
                    j = SUBLANES * a + s
                    if j < CONV_WIDTH:
                        acc = acc + shifted[SUBLANES * a:SUBLANES * a + CONV_TILE, :] * cw_ref[j:j + 1, c0:c0 + LANES]
            p_scr[pl.ds(r, CONV_TILE), c0:c0 + LANES] = acc
        return c

    lax.fori_loop(0, SEQ // CONV_TILE, conv_body, 0)

    def conv_norm_body(i, c):
        r = pl.multiple_of(i * ROW_TILE, ROW_TILE)
        acc = p_scr[pl.ds(r, ROW_TILE), 0:CONV_DIM]
        mu = jnp.mean(acc, axis=-1, keepdims=True)
        d = acc - mu
        var = jnp.mean(d * d, axis=-1, keepdims=True)
        u = d * lax.rsqrt(var + EPS) * lng_ref[...] + lnb_ref[...]
        y_scr[pl.ds(r, ROW_TILE), HG_DIM + NA_DIM:D_MODEL] = (u * _sigmoid(u)).astype(BF16)
        return c

    lax.fori_loop(0, n_row_tiles, conv_norm_body, 0)

    scale = NA_HEAD_DIM ** -0.5

    def na_proj_body(i, c):
        r = pl.multiple_of(i * ROW_TILE, ROW_TILE)
        res = _dot(h_scr[pl.ds(r, ROW_TILE), :], win_ref[:, NA_COL0:NA_COL0 + 3 * NA_DIM])
        p_scr[pl.ds(r, ROW_TILE), 0:NA_DIM] = res[:, 0:NA_DIM] * scale
        h_scr[pl.ds(r, ROW_TILE), 0:2 * NA_DIM] = res[:, NA_DIM:3 * NA_DIM].astype(BF16)
        return c

    lax.fori_loop(0, n_row_tiles, na_proj_body, 0)
    lane = lax.broadcasted_iota(jnp.int32, (1, LANES), 1)
    first_half = lane < NA_HEAD_DIM

    n_na_steps = ROWS // NA_ROWS_PER_STEP

    def na_units(step):
        units = []
        for dr in range(NA_ROWS_PER_STEP):
            r = step * NA_ROWS_PER_STEP + dr
            r0 = jnp.clip(r - NA_KH // 2, 0, ROWS - NA_KH)
            for hp in range(NA_HEADS // 2):
                units.append((pl.multiple_of(r * GRID_W, GRID_W), pl.multiple_of(r0 * GRID_W, GRID_W), r - r0, hp))
        return units

    def na_scores(units):
        for u, (qrow, krow, var, hp) in enumerate(units):
            c0 = hp * LANES
            qt = p_scr[pl.ds(qrow, GRID_W), c0:c0 + LANES]
            qm = jnp.concatenate([jnp.where(first_half, qt, 0.0), jnp.where(first_half, 0.0, qt)], axis=0).astype(BF16)
            qk_scr[u] = _dot_nt(qm, h_scr[pl.ds(krow, NA_KEYS), c0:c0 + LANES])

    na_scores(na_units(0))

    def na_body(i, c):
        units = na_units(i)
        probs = []
        for u, (qrow, krow, var, hp) in enumerate(units):
            bt = bias_ref[hp, var]
            s = jnp.where(bt > 0.5 * NEG_INF, qk_scr[u] + bt, NEG_INF)
            e = jnp.exp(s - jnp.max(s, axis=-1, keepdims=True))
            probs.append((e.astype(BF16), jnp.sum(e, axis=-1, keepdims=True)))
        na_scores(na_units(jnp.minimum(i + 1, n_na_steps - 1)))
        for (qrow, krow, var, hp), (e, den) in zip(units, probs):
            c0 = hp * LANES
            o = _dot(e, h_scr[pl.ds(krow, NA_KEYS), NA_DIM + c0:NA_DIM + c0 + LANES]) / den
            y_scr[pl.ds(qrow, GRID_W), HG_DIM + c0:HG_DIM + c0 + LANES] = jnp.where(
                first_half, o[0:GRID_W], o[GRID_W:2 * GRID_W]).astype(BF16)
        return c

    lax.fori_loop(0, n_na_steps, na_body, 0)

    def out_body(i, c):
        r = pl.multiple_of(i * ROW_TILE, ROW_TILE)
        o_ref[pl.ds(r, ROW_TILE), :] = x_ref[pl.ds(r, ROW_TILE), :] + _dot(y_scr[pl.ds(r, ROW_TILE), :], wout_ref[...])
        return c

    lax.fori_loop(0, n_row_tiles, out_body, 0)


def _ffn_kernel(final, x_ref, ng_ref, wgu_ref, wd_ref, fg_ref, o_ref, h_scr, acc_scr):
    x = x_ref[...]
    ms = jnp.mean(x * x, axis=-1, keepdims=True)
    h_scr[...] = (x * lax.rsqrt(ms + EPS) * ng_ref[...]).astype(BF16)
    for c in range(FFN_NCHUNK):
        gt = _dot(h_scr[...], wgu_ref[:, FFN_CHUNK * c:FFN_CHUNK * (c + 1)])
        up = _dot(h_scr[...], wgu_ref[:, FFN_DIM + FFN_CHUNK * c:FFN_DIM + FFN_CHUNK * (c + 1)])
        a = (gt * _sigmoid(gt) * up).astype(BF16)
        d = _dot(a, wd_ref[FFN_CHUNK * c:FFN_CHUNK * (c + 1), :])
        if c == 0:
            acc_scr[...] = d
        else:
            acc_scr[...] += d
    y = x_ref[...] + acc_scr[...]
    if final:
        y = y * lax.rsqrt(jnp.mean(y * y, axis=-1, keepdims=True) + EPS) * fg_ref[...]
    o_ref[...] = y


def _resident(shape):
    nd = len(shape)
    return pl.BlockSpec(shape, lambda *_: (0,) * nd, pipeline_mode=pl.Buffered(1))


def _layer_slice(layer, shape):
    nd = len(shape)
    return pl.BlockSpec((None,) + tuple(shape[1:]), lambda *_: (layer,) + (0,) * (nd - 1), pipeline_mode=pl.Buffered(1))


def _mixer_call(layer, x, ng, win, hlb, hgn, rpbp, cw, cb, lng, lnb, wout):
    batch = x.shape[0]
    seq_block = pl.BlockSpec((None, SEQ, D_MODEL), lambda b: (b, 0, 0), pipeline_mode=pl.Buffered(1))
    stacked = (ng, win, None, hgn, rpbp, cw, cb, lng, lnb, wout)
    return pl.pallas_call(
        functools.partial(_mixer_kernel, layer),
        grid=(batch,),
        in_specs=[seq_block] + [_resident(hlb.shape) if a is None else _layer_slice(layer, a.shape) for a in stacked],
        out_specs=pl.BlockSpec((None, SEQ, D_MODEL), lambda b: (b, 0, 0), pipeline_mode=pl.Buffered(1)),
        out_shape=jax.ShapeDtypeStruct(x.shape, x.dtype),
        scratch_shapes=[
            pltpu.VMEM((SEQ, D_MODEL), BF16),
            pltpu.VMEM((SEQ, D_MODEL), BF16),
            pltpu.VMEM((SEQ, P_COLS), F32),
            pltpu.VMEM((SEQ, HG_HEAD_DIM), F32),
            pltpu.VMEM((2, SEQ, HG_HEAD_DIM), BF16),
            pltpu.VMEM((2, HG_NCHUNK, HG_HEAD_DIM, HG_HEAD_DIM), F32),
            pltpu.VMEM((2, HG_NCHUNK, HG_HEAD_DIM, HG_HEAD_DIM), BF16),
            pltpu.VMEM((2, HG_NCHUNK, SUBLANES, LANES), F32),
            pltpu.VMEM((SEQ + 2 * CONV_PAD, CONV_DIM), F32),
            pltpu.VMEM((NA_HEADS // 2, NA_KH, 2 * GRID_W, NA_KEYS), F32),
            pltpu.VMEM((D_MODEL, HG_COLS), BF16),
            pltpu.VMEM((NA_ROWS_PER_STEP * NA_HEADS // 2, 2 * GRID_W, NA_KEYS), F32),
        ],
        compiler_params=pltpu.CompilerParams(dimension_semantics=("arbitrary",), vmem_limit_bytes=VMEM_LIMIT_BYTES),
        name=f"mixer_l{layer}",
    )(x, ng, win, hlb, hgn, rpbp, cw, cb, lng, lnb, wout)


def _ffn_call(layer, final, x2d, ng, wgu, wd, fg):
    n_tok = x2d.shape[0]
    tile = pl.BlockSpec((FFN_TILE, D_MODEL), lambda i: (i, 0))
    return pl.pallas_call(
        functools.partial(_ffn_kernel, final),
        grid=(n_tok // FFN_TILE,),
        in_specs=[tile] + [_layer_slice(layer, a.shape) for a in (ng, wgu, wd)] + [_resident(fg.shape)],
        out_specs=tile,
        out_shape=jax.ShapeDtypeStruct(x2d.shape, x2d.dtype),
        scratch_shapes=[pltpu.VMEM((FFN_TILE, D_MODEL), BF16), pltpu.VMEM((FFN_TILE, D_MODEL), F32)],
        compiler_params=pltpu.CompilerParams(dimension_semantics=("arbitrary",), vmem_limit_bytes=VMEM_LIMIT_BYTES),
        name=f"ffn_l{layer}",
    )(x2d, ng, wgu, wd, fg)


def _pad_rpb(rpb):
    v = jnp.pad(rpb, ((0, 0), (0, 0), (0, RPB_ROWS - NA_NREL_H), (RPB_LEAD, LANES - RPB_LEAD - NA_NREL_W)))
    return v.reshape(DEPTH, NA_HEADS * RPB_ROWS, LANES)


def kernel(x, mix_norm_g, w_in, hg_lower_bounds, hg_norm_g, na_rpb, conv_w, conv_b, conv_ln_g, conv_ln_b, w_out,
           ffn_norm_g, w_gate_up, w_down, final_norm_g):
    batch, seq, d = x.shape
    assert (seq, d) == (SEQ, D_MODEL) and w_in.shape == (DEPTH, D_MODEL, D_IN)
    row = lambda p: p.astype(F32).reshape(DEPTH, 1, p.shape[-1])
    win, wout, wgu, wd = (w.astype(BF16) for w in (w_in, w_out, w_gate_up, w_down))
    hlb = hg_lower_bounds.astype(F32)
    rpbp = _pad_rpb(na_rpb.astype(F32))
    mix_g, hgn, cb, lng, lnb, ffn_g = (row(p) for p in (mix_norm_g, hg_norm_g, conv_b, conv_ln_g, conv_ln_b, ffn_norm_g))
    fg = final_norm_g.astype(F32).reshape(1, D_MODEL)
    for l in range(DEPTH):
        x = _mixer_call(l, x, mix_g, win, hlb, hgn, rpbp, conv_w.astype(F32), cb, lng, lnb, wout)
        x2d = _ffn_call(l, l == DEPTH - 1, x.reshape(batch * seq, d), ffn_g, wgu, wd, fg)
        x = x2d.reshape(batch, seq, d)
    return x
```

```python
import functools

import numpy as np
import jax
import jax.numpy as jnp
from jax import lax
from jax.experimental import pallas as pl
from jax.experimental.pallas import tpu as pltpu

D_MODEL = 1024
SEQ = 2048
DEPTH = 2
GRID_W = 64
ROWS = SEQ // GRID_W
HG_DIM = 512
HG_HEADS = 4
HG_HEAD_DIM = 128
NA_DIM = 256
NA_HEADS = 4
NA_HEAD_DIM = 64
NA_KH = 8
NA_KW = 16
CONV_DIM = 256
CONV_WIDTH = 31
D_IN = 5 * HG_DIM + 3 * NA_DIM + 2 * CONV_DIM
FFN_DIM = 2816
EPS = 1e-6
NEG_INF = -1e30

LANES = 128
SUBLANES = 8
VMEM_LIMIT_BYTES = 60 * 1024 * 1024

ROW_TILE = 1024
HG_CHUNK = 64
HG_CHUNK_LOG2 = HG_CHUNK.bit_length() - 1
assert HG_CHUNK == 1 << HG_CHUNK_LOG2
HG_NCHUNK = SEQ // HG_CHUNK
HG_TILE = 256
HG_OUT_TILE = 512
HG_LOCAL_UNROLL = 1
X_TILE = 256
HG_COLS = 5 * HG_HEAD_DIM
NA_COL0 = 5 * HG_DIM
CV_COL0 = NA_COL0 + 3 * NA_DIM
NA_KEYS = NA_KH * GRID_W
NA_NREL_H = 2 * NA_KH - 1
NA_NREL_W = 2 * NA_KW - 1
RPB_LEAD = GRID_W - NA_KW
RPB_ROWS = 16
assert RPB_ROWS >= NA_NREL_H and RPB_ROWS % SUBLANES == 0
NA_ROWS_PER_STEP = 2
CONV_TILE = 64
CONV_PAD = 16
P_COLS = HG_COLS
assert P_COLS >= 2 * CONV_DIM and P_COLS >= NA_DIM
FFN_TILE = 512
FFN_CHUNK = 256
FFN_NCHUNK = FFN_DIM // FFN_CHUNK

F32 = jnp.float32
BF16 = jnp.bfloat16


def _sigmoid(x):
    return 1.0 / (1.0 + jnp.exp(-x))


def _dot(a, b):
    return jnp.dot(a, b, preferred_element_type=F32)


def _dot_nt(a, b):
    return lax.dot_general(a, b, (((1,), (1,)), ((), ())), preferred_element_type=F32)


def _dot_tn(a, b):
    return lax.dot_general(a, b, (((0,), (0,)), ((), ())), preferred_element_type=F32)


def _split3(x):
    hi = x.astype(BF16)
    r1 = x - hi.astype(F32)
    mid = r1.astype(BF16)
    lo = (r1 - mid.astype(F32)).astype(BF16)
    return hi, mid, lo


def _build_na_bias(rpbp_ref, bias_ref):
    q_i = lax.broadcasted_iota(jnp.int32, (GRID_W, LANES), 0)
    l_i = lax.broadcasted_iota(jnp.int32, (GRID_W, LANES), 1)
    kc = jnp.bitwise_and(l_i, GRID_W - 1)
    q_start = jnp.clip(q_i - NA_KW // 2, 0, GRID_W - NA_KW)
    visible = (kc >= q_start) & (kc < q_start + NA_KW)
    left = l_i < GRID_W
    q_all = jnp.bitwise_and(lax.broadcasted_iota(jnp.int32, (RPB_ROWS * GRID_W, LANES), 0), GRID_W - 1)

    def head_body(h, c):
        rows = rpbp_ref[pl.ds(pl.multiple_of(h * RPB_ROWS, RPB_ROWS), RPB_ROWS), :]
        x = jnp.concatenate([jnp.broadcast_to(rows[ro:ro + 1, :], (GRID_W, LANES)) for ro in range(RPB_ROWS)], axis=0)
        even = pltpu.roll(x, LANES - (GRID_W - 1), axis=1)
        for b in range(GRID_W.bit_length() - 1):
            even = jnp.where(jnp.bitwise_and(q_all, 1 << b) != 0, pltpu.roll(even, 1 << b, axis=1), even)
        odd = pltpu.roll(even, GRID_W, axis=1)
        hp = h // 2
        q0 = pl.multiple_of((h % 2) * GRID_W, GRID_W)
        for var in range(NA_KH):
            for kp in range(NA_KH // 2):
                ro = 2 * kp - var + NA_KH - 1
                tile = jnp.where(left, even[ro * GRID_W:(ro + 1) * GRID_W], odd[(ro + 1) * GRID_W:(ro + 2) * GRID_W])
                bias_ref[hp, var, pl.ds(q0, GRID_W), kp * LANES:(kp + 1) * LANES] = jnp.where(visible, tile, NEG_INF)
        return c

    lax.fori_loop(0, NA_HEADS, head_body, 0)


def _mixer_kernel(layer, x_ref, ng_ref, win_ref, hlb_ref, hgn_ref, rpbp_ref, cw_ref, cb_ref, lng_ref, lnb_ref,
                  y_ref, h_scr, p_scr, oi_scr, qa_scr, d_scr, st_scr, dec_scr, upad_scr, bias_ref,
                  wh_scr, qk_scr, stage_scr):
    t = pl.program_id(1)

    @pl.when((pl.program_id(0) == 0) & (t == 0))
    def _():
        _build_na_bias(rpbp_ref, bias_ref)

    xt = x_ref[...]
    ms = jnp.mean(xt * xt, axis=-1, keepdims=True)
    h_scr[pl.ds(pl.multiple_of(t * X_TILE, X_TILE), X_TILE), :] = (xt * lax.rsqrt(ms + EPS) * ng_ref[...]).astype(BF16)

    @pl.when(t == SEQ // X_TILE - 1)
    def _():
        _mix_sequence(layer, win_ref, hlb_ref, hgn_ref, cw_ref, cb_ref, lng_ref, lnb_ref, y_ref, h_scr, p_scr, oi_scr,
                      qa_scr, d_scr, st_scr, dec_scr, upad_scr, bias_ref, wh_scr, qk_scr, stage_scr)


def _mix_sequence(layer, win_ref, hlb_ref, hgn_ref, cw_ref, cb_ref, lng_ref, lnb_ref, y_scr, h_scr, p_scr, oi_scr,
                  qa_scr, d_scr, st_scr, dec_scr, upad_scr, bias_ref, wh_scr, qk_scr, stage_scr):
    n_row_tiles = SEQ // ROW_TILE

    def project(w_ref, col0, ncols):
        def body(i, c):
            r = pl.multiple_of(i * ROW_TILE, ROW_TILE)
            p_scr[pl.ds(r, ROW_TILE), 0:ncols] = _dot(h_scr[pl.ds(r, ROW_TILE), :], w_ref[:, col0:col0 + ncols])
            return c
        lax.fori_loop(0, n_row_tiles, body, 0)

    hlb = [hlb_ref[d] for d in range(DEPTH)]
    hmax = functools.reduce(jnp.maximum, hlb)
    hexp = [jnp.exp(v - hmax) for v in hlb]
    hsum = functools.reduce(lambda a, b: a + b, hexp)
    hsm = [e / hsum for e in hexp]
    lbs = functools.reduce(lambda a, b: a + b, hsm[:layer + 1]) - hsm[0]

    L = HG_CHUNK
    TL = HG_TILE
    cpt = TL // L
    row_c = lax.broadcasted_iota(jnp.int32, (TL, TL), 0)
    col_c = lax.broadcasted_iota(jnp.int32, (TL, TL), 1)
    same_chunk = jnp.right_shift(row_c, HG_CHUNK_LOG2) == jnp.right_shift(col_c, HG_CHUNK_LOG2)
    lower = same_chunk & (row_c >= col_c)
    upper = same_chunk & (col_c >= row_c)
    lower_b = jnp.where(lower, 1.0, 0.0).astype(BF16)
    upper_b = jnp.where(upper, 1.0, 0.0).astype(BF16)

    def chunk_rows(a, row):
        return [a[c * L + row:c * L + row + 1, :] for c in range(cpt)]

    def spread(rows):
        return jnp.concatenate([jnp.broadcast_to(rw, (L, LANES)) for rw in rows], axis=0)

    for hd in range(HG_HEADS):
        sl = slice(hd * HG_HEAD_DIM, (hd + 1) * HG_HEAD_DIM)
        for grp in range(HG_COLS // HG_HEAD_DIM):
            wh_scr[:, grp * HG_HEAD_DIM:(grp + 1) * HG_HEAD_DIM] = win_ref[:, grp * HG_DIM + sl.start:grp * HG_DIM + sl.stop]
        project(wh_scr, 0, HG_COLS)
        directions = (
            (HG_HEAD_DIM, lbs[0:1, sl], lower, lower_b, L // 2 - 1, L - 1),
            (2 * HG_HEAD_DIM, lbs[1:2, sl], upper, upper_b, L // 2, 0),
        )

        def tile_start(i):
            return i * TL if isinstance(i, int) else pl.multiple_of(i * TL, TL)

        def hg_scale(i):
            r = tile_start(i)
            q_raw = p_scr[pl.ds(r, TL), 0:HG_HEAD_DIM]
            q = q_raw * _sigmoid(q_raw)
            gates = []
            for zcol, lb, mask, cum_op, ref_row, last_row in directions:
                z = p_scr[pl.ds(r, TL), zcol:zcol + HG_HEAD_DIM]
                t = jnp.exp(-jnp.abs(z))
                big = 1.0 / (1.0 + t)
                small = t * big
                nonneg = z >= 0.0
                f = lb + (1.0 - lb) * jnp.where(nonneg, big, small)
                k = (1.0 - lb) * jnp.where(nonneg, small, big)
                gates.append((k, _dot(cum_op, jnp.concatenate(_split3(jnp.log(f)), axis=1))))
            for d, ((zcol, lb, mask, cum_op, ref_row, last_row), (k, c3)) in enumerate(zip(directions, gates)):
                cum = c3[:, 0:LANES] + c3[:, LANES:2 * LANES] + c3[:, 2 * LANES:3 * LANES]
                ref_rows = chunk_rows(cum, ref_row)
                last_rows = chunk_rows(cum, last_row)
                rel = cum - spread(ref_rows)
                q_in = q * jnp.exp(rel)
                k_in = k * jnp.exp(-rel)
                qa_scr[d, pl.ds(r, TL), :] = (q_in * spread([jnp.exp(rw) for rw in ref_rows])).astype(BF16)
                k_d = (k_in * spread([jnp.exp(lr - rr) for lr, rr in zip(last_rows, ref_rows)])).astype(BF16)
                for cc in range(cpt):
                    dec_scr[d, i * cpt + cc] = jnp.broadcast_to(jnp.exp(last_rows[cc]), (SUBLANES, LANES))
                stage_scr[d, 0] = q_in.astype(BF16)
                stage_scr[d, 1] = k_in.astype(BF16)
                stage_scr[d, 2] = k_d

        def hg_contract(i):
            r = tile_start(i)
            vb = p_scr[pl.ds(r, TL), 3 * HG_HEAD_DIM:4 * HG_HEAD_DIM].astype(BF16)
            qk = [_dot_nt(stage_scr[d, 0], stage_scr[d, 1]) for d in range(len(directions))]
            o_intra = None
            for d, (zcol, lb, mask, cum_op, ref_row, last_row) in enumerate(directions):
                o_dir = _dot(jnp.where(mask, qk[d], 0.0).astype(BF16), vb)
                o_intra = o_dir if o_intra is None else o_intra + o_dir
                k_d = stage_scr[d, 2]
                for cc in range(cpt):
                    d_scr[d, i * cpt + cc] = _dot_tn(vb[cc * L:(cc + 1) * L], k_d[cc * L:(cc + 1) * L])
            oi_scr[pl.ds(r, TL), :] = o_intra

        def hg_local_body(i, c):
            hg_contract(i)
            hg_scale(i + 1)
            return c

        hg_scale(0)
        lax.fori_loop(0, SEQ // TL - 1, hg_local_body, 0, unroll=HG_LOCAL_UNROLL)
        hg_contract(SEQ // TL - 1)

        def hg_scan_body(n, carry):
            s_f, s_b = carry
            m = HG_NCHUNK - 1 - n
            st_scr[0, n] = s_f.astype(BF16)
            st_scr[1, m] = s_b.astype(BF16)
            s_f = s_f * dec_scr[0, n][0:1, :] + d_scr[0, n]
            s_b = s_b * dec_scr[1, m][0:1, :] + d_scr[1, m]
            return s_f, s_b

        s_zero = jnp.zeros((HG_HEAD_DIM, HG_HEAD_DIM), F32)
        lax.fori_loop(0, HG_NCHUNK, hg_scan_body, (s_zero, s_zero))

        def hg_out_body(i, c):
            r = pl.multiple_of(i * HG_OUT_TILE, HG_OUT_TILE)
            inter = []
            for cc in range(HG_OUT_TILE // L):
                n = i * (HG_OUT_TILE // L) + cc
                rc = pl.multiple_of(r + cc * L, L)
                qa = jnp.concatenate([qa_scr[0, pl.ds(rc, L), :], qa_scr[1, pl.ds(rc, L), :]], axis=1)
                st = jnp.concatenate([st_scr[0, n], st_scr[1, n]], axis=1)
                inter.append(_dot_nt(qa, st))
            o = oi_scr[pl.ds(r, HG_OUT_TILE), :] + jnp.concatenate(inter, axis=0)
            o = o * lax.rsqrt(jnp.mean(o * o, axis=-1, keepdims=True) + EPS) * hgn_ref[...]
            gate = p_scr[pl.ds(r, HG_OUT_TILE), 4 * HG_HEAD_DIM:5 * HG_HEAD_DIM]
            y_scr[pl.ds(r, HG_OUT_TILE), sl] = (o * (gate * _sigmoid(gate))).astype(BF16)
            return c

        lax.fori_loop(0, SEQ // HG_OUT_TILE, hg_out_body, 0, unroll=2)

    project(win_ref, CV_COL0, 2 * CONV_DIM)
    upad_scr[0:CONV_PAD, :] = jnp.zeros((CONV_PAD, CONV_DIM), F32)
    upad_scr[CONV_PAD + SEQ:CONV_PAD + SEQ + CONV_PAD, :] = jnp.zeros((CONV_PAD, CONV_DIM), F32)

    def glu_body(i, c):
        r = pl.multiple_of(i * ROW_TILE, ROW_TILE)
        a = p_scr[pl.ds(r, ROW_TILE), 0:CONV_DIM]
        gate = p_scr[pl.ds(r, ROW_TILE), CONV_DIM:2 * CONV_DIM]
        upad_scr[pl.ds(CONV_PAD + r, ROW_TILE), :] = a * _sigmoid(gate)
        return c

    lax.fori_loop(0, n_row_tiles, glu_body, 0)

    def conv_body(i, c):
        r = pl.multiple_of(i * CONV_TILE, CONV_TILE)
        off = CONV_PAD - CONV_WIDTH // 2
        groups = -(-CONV_WIDTH // SUBLANES)
        win_rows = CONV_TILE + 2 * CONV_PAD
        for c0 in range(0, CONV_DIM, LANES):
            win = upad_scr[pl.ds(r, win_rows), c0:c0 + LANES]
            acc = jnp.zeros((CONV_TILE, LANES), F32) + cb_ref[:, c0:c0 + LANES]
            shifted = win
            for s in range(SUBLANES):
                step = off if s == 0 else 1
                if step:
                    shifted = pltpu.roll(shifted, win_rows - step, axis=0)
                for a in range(groups):
                    j = SUBLANES * a + s
                    if j < CONV_WIDTH:
                        acc = acc + shifted[SUBLANES * a:SUBLANES * a + CONV_TILE, :] * cw_ref[j:j + 1, c0:c0 + LANES]
            p_scr[pl.ds(r, CONV_TILE), c0:c0 + LANES] = acc
        return c

    lax.fori_loop(0, SEQ // CONV_TILE, conv_body, 0)

    def conv_norm_body(i, c):
        r = pl.multiple_of(i * ROW_TILE, ROW_TILE)
        acc = p_scr[pl.ds(r, ROW_TILE), 0:CONV_DIM]
        mu = jnp.mean(acc, axis=-1, keepdims=True)
        d = acc - mu
        var = jnp.mean(d * d, axis=-1, keepdims=True)
        u = d * lax.rsqrt(var + EPS) * lng_ref[...] + lnb_ref[...]
        y_scr[pl.ds(r, ROW_TILE), HG_DIM + NA_DIM:D_MODEL] = (u * _sigmoid(u)).astype(BF16)
        return c

    lax.fori_loop(0, n_row_tiles, conv_norm_body, 0)

    scale = NA_HEAD_DIM ** -0.5

    def na_proj_body(i, c):
        r = pl.multiple_of(i * ROW_TILE, ROW_TILE)
        res = _dot(h_scr[pl.ds(r, ROW_TILE), :], win_ref[:, NA_COL0:NA_COL0 + 3 * NA_DIM])
        p_scr[pl.ds(r, ROW_TILE), 0:NA_DIM] = res[:, 0:NA_DIM] * scale
        h_scr[pl.ds(r, ROW_TILE), 0:2 * NA_DIM] = res[:, NA_DIM:3 * NA_DIM].astype(BF16)
        return c

    lax.fori_loop(0, n_row_tiles, na_proj_body, 0)
    lane = lax.broadcasted_iota(jnp.int32, (1, LANES), 1)
    first_half = lane < NA_HEAD_DIM

    n_na_steps = ROWS // NA_ROWS_PER_STEP

    def na_units(step):
        units = []
        for dr in range(NA_ROWS_PER_STEP):
            r = step * NA_ROWS_PER_STEP + dr
            r0 = jnp.clip(r - NA_KH // 2, 0, ROWS - NA_KH)
            for hp in range(NA_HEADS // 2):
                units.append((pl.multiple_of(r * GRID_W, GRID_W), pl.multiple_of(r0 * GRID_W, GRID_W), r - r0, hp))
        return units

    def na_scores(units):
        for u, (qrow, krow, var, hp) in enumerate(units):
            c0 = hp * LANES
            qt = p_scr[pl.ds(qrow, GRID_W), c0:c0 + LANES]
            qm = jnp.concatenate([jnp.where(first_half, qt, 0.0), jnp.where(first_half, 0.0, qt)], axis=0).astype(BF16)
            qk_scr[u] = _dot_nt(qm, h_scr[pl.ds(krow, NA_KEYS), c0:c0 + LANES])

    na_scores(na_units(0))

    def na_body(i, c):
        units = na_units(i)
        probs = []
        for u, (qrow, krow, var, hp) in enumerate(units):
            bt = bias_ref[hp, var]
            s = jnp.where(bt > 0.5 * NEG_INF, qk_scr[u] + bt, NEG_INF)
            e = jnp.exp(s - jnp.max(s, axis=-1, keepdims=True))
            probs.append((e.astype(BF16), jnp.sum(e, axis=-1, keepdims=True)))
        na_scores(na_units(jnp.minimum(i + 1, n_na_steps - 1)))
        for (qrow, krow, var, hp), (e, den) in zip(units, probs):
            c0 = hp * LANES
            o = _dot(e, h_scr[pl.ds(krow, NA_KEYS), NA_DIM + c0:NA_DIM + c0 + LANES]) / den
            y_scr[pl.ds(qrow, GRID_W), HG_DIM + c0:HG_DIM + c0 + LANES] = jnp.where(
                first_half, o[0:GRID_W], o[GRID_W:2 * GRID_W]).astype(BF16)
        return c

    lax.fori_loop(0, n_na_steps, na_body, 0)


def _ffn_kernel(final, x_ref, y_ref, wout_ref, ng_ref, wgu_ref, wd_ref, fg_ref, o_ref, h_scr, acc_scr, x_scr):
    x = x_ref[...] + _dot(y_ref[...], wout_ref[...])
    x_scr[...] = x
    ms = jnp.mean(x * x, axis=-1, keepdims=True)
    h_scr[...] = (x * lax.rsqrt(ms + EPS) * ng_ref[...]).astype(BF16)
    for c in range(FFN_NCHUNK):
        gt = _dot(h_scr[...], wgu_ref[:, FFN_CHUNK * c:FFN_CHUNK * (c + 1)])
        up = _dot(h_scr[...], wgu_ref[:, FFN_DIM + FFN_CHUNK * c:FFN_DIM + FFN_CHUNK * (c + 1)])
        a = (gt * _sigmoid(gt) * up).astype(BF16)
        d = _dot(a, wd_ref[FFN_CHUNK * c:FFN_CHUNK * (c + 1), :])
        if c == 0:
            acc_scr[...] = d
        else:
            acc_scr[...] += d
    y = x_scr[...] + acc_scr[...]
    if final:
        y = y * lax.rsqrt(jnp.mean(y * y, axis=-1, keepdims=True) + EPS) * fg_ref[...]
    o_ref[...] = y


def _resident(shape):
    nd = len(shape)
    return pl.BlockSpec(shape, lambda *_: (0,) * nd, pipeline_mode=pl.Buffered(1))


def _layer_slice(layer, shape):
    nd = len(shape)
    return pl.BlockSpec((None,) + tuple(shape[1:]), lambda *_: (layer,) + (0,) * (nd - 1), pipeline_mode=pl.Buffered(1))


def _mixer_call(layer, x, ng, win, hlb, hgn, rpbp, cw, cb, lng, lnb):
    batch = x.shape[0]
    stacked = (ng, win, None, hgn, rpbp, cw, cb, lng, lnb)
    return pl.pallas_call(
        functools.partial(_mixer_kernel, layer),
        grid=(batch, SEQ // X_TILE),
        in_specs=[pl.BlockSpec((None, X_TILE, D_MODEL), lambda b, t: (b, t, 0))]
        + [_resident(hlb.shape) if a is None else _layer_slice(layer, a.shape) for a in stacked],
        out_specs=pl.BlockSpec((None, SEQ, D_MODEL), lambda b, t: (b, 0, 0)),
        out_shape=jax.ShapeDtypeStruct(x.shape, BF16),
        scratch_shapes=[
            pltpu.VMEM((SEQ, D_MODEL), BF16),
            pltpu.VMEM((SEQ, P_COLS), F32),
            pltpu.VMEM((SEQ, HG_HEAD_DIM), F32),
            pltpu.VMEM((2, SEQ, HG_HEAD_DIM), BF16),
            pltpu.VMEM((2, HG_NCHUNK, HG_HEAD_DIM, HG_HEAD_DIM), F32),
            pltpu.VMEM((2, HG_NCHUNK, HG_HEAD_DIM, HG_HEAD_DIM), BF16),
            pltpu.VMEM((2, HG_NCHUNK, SUBLANES, LANES), F32),
            pltpu.VMEM((SEQ + 2 * CONV_PAD, CONV_DIM), F32),
            pltpu.VMEM((NA_HEADS // 2, NA_KH, 2 * GRID_W, NA_KEYS), F32),
            pltpu.VMEM((D_MODEL, HG_COLS), BF16),
            pltpu.VMEM((NA_ROWS_PER_STEP * NA_HEADS // 2, 2 * GRID_W, NA_KEYS), F32),
            pltpu.VMEM((2, 3, HG_TILE, HG_HEAD_DIM), BF16),
        ],
        compiler_params=pltpu.CompilerParams(dimension_semantics=("arbitrary", "arbitrary"),
                                             vmem_limit_bytes=VMEM_LIMIT_BYTES),
        name=f"mixer_l{layer}",
    )(x, ng, win, hlb, hgn, rpbp, cw, cb, lng, lnb)


def _ffn_call(layer, final, x2d, y2d, wout, ng, wgu, wd, fg):
    n_tok = x2d.shape[0]
    tile = pl.BlockSpec((FFN_TILE, D_MODEL), lambda i: (i, 0))
    return pl.pallas_call(
        functools.partial(_ffn_kernel, final),
        grid=(n_tok // FFN_TILE,),
        in_specs=[tile, tile] + [_layer_slice(layer, a.shape) for a in (wout, ng, wgu, wd)] + [_resident(fg.shape)],
        out_specs=tile,
        out_shape=jax.ShapeDtypeStruct(x2d.shape, x2d.dtype),
        scratch_shapes=[pltpu.VMEM((FFN_TILE, D_MODEL), BF16), pltpu.VMEM((FFN_TILE, D_MODEL), F32),
                        pltpu.VMEM((FFN_TILE, D_MODEL), F32)],
        compiler_params=pltpu.CompilerParams(dimension_semantics=("arbitrary",), vmem_limit_bytes=VMEM_LIMIT_BYTES),
        name=f"ffn_l{layer}",
    )(x2d, y2d, wout, ng, wgu, wd, fg)


def _pad_rpb(rpb):
    v = jnp.pad(rpb, ((0, 0), (0, 0), (0, RPB_ROWS - NA_NREL_H), (RPB_LEAD, LANES - RPB_LEAD - NA_NREL_W)))
    return v.reshape(DEPTH, NA_HEADS * RPB_ROWS, LANES)


def kernel(x, mix_norm_g, w_in, hg_lower_bounds, hg_norm_g, na_rpb, conv_w, conv_b, conv_ln_g, conv_ln_b, w_out,
           ffn_norm_g, w_gate_up, w_down, final_norm_g):
    batch, seq, d = x.shape
    assert (seq, d) == (SEQ, D_MODEL) and w_in.shape == (DEPTH, D_MODEL, D_IN)
    row = lambda p: p.astype(F32).reshape(DEPTH, 1, p.shape[-1])
    win, wout, wgu, wd = (w.astype(BF16) for w in (w_in, w_out, w_gate_up, w_down))
    hlb = hg_lower_bounds.astype(F32)
    rpbp = _pad_rpb(na_rpb.astype(F32))
    mix_g, hgn, cb, lng, lnb, ffn_g = (row(p) for p in (mix_norm_g, hg_norm_g, conv_b, conv_ln_g, conv_ln_b, ffn_norm_g))
    fg = final_norm_g.astype(F32).reshape(1, D_MODEL)
    for l in range(DEPTH):
        y = _mixer_call(l, x, mix_g, win, hlb, hgn, rpbp, conv_w.astype(F32), cb, lng, lnb)
        x2d = _ffn_call(l, l == DEPTH - 1, x.reshape(batch * seq, d), y.reshape(batch * seq, d), wout, ffn_g, wgu, wd, fg)
        x = x2d.reshape(batch, seq, d)
    return x
```

```python
import functools

import numpy as np
import jax
import jax.numpy as jnp
from jax import lax
from jax.experimental import pallas as pl
from jax.experimental.pallas import tpu as pltpu

D_MODEL = 1024
SEQ = 2048
DEPTH = 2
GRID_W = 64
ROWS = SEQ // GRID_W
HG_DIM = 512
HG_HEADS = 4
HG_HEAD_DIM = 128
NA_DIM = 256
NA_HEADS = 4
NA_HEAD_DIM = 64
NA_KH = 8
NA_KW = 16
CONV_DIM = 256
CONV_WIDTH = 31
D_IN = 5 * HG_DIM + 3 * NA_DIM + 2 * CONV_DIM
FFN_DIM = 2816
EPS = 1e-6
NEG_INF = -1e30

LANES = 128
SUBLANES = 8
VMEM_LIMIT_BYTES = 60 * 1024 * 1024

ROW_TILE = 1024
HG_CHUNK = 64
HG_CHUNK_LOG2 = HG_CHUNK.bit_length() - 1
assert HG_CHUNK == 1 << HG_CHUNK_LOG2
HG_NCHUNK = SEQ // HG_CHUNK
HG_TILE = 256
HG_OUT_TILE = 512
HG_LOCAL_UNROLL = 1
X_TILE = 512
HG_COLS = 5 * HG_HEAD_DIM
NA_COL0 = 5 * HG_DIM
CV_COL0 = NA_COL0 + 3 * NA_DIM
NA_KEYS = NA_KH * GRID_W
NA_NREL_H = 2 * NA_KH - 1
NA_NREL_W = 2 * NA_KW - 1
RPB_LEAD = GRID_W - NA_KW
RPB_ROWS = 16
assert RPB_ROWS >= NA_NREL_H and RPB_ROWS % SUBLANES == 0
NA_ROWS_PER_STEP = 2
CONV_TILE = 64
CONV_OUTER_TILE = 256
CONV_PAD = 16
P_COLS = HG_COLS
assert P_COLS >= 2 * CONV_DIM and P_COLS >= NA_DIM
FFN_TILE = 1024
FFN_CHUNK = 256
FFN_NCHUNK = FFN_DIM // FFN_CHUNK

F32 = jnp.float32
BF16 = jnp.bfloat16


def _sigmoid(x):
    return 1.0 / (1.0 + jnp.exp(-x))


def _dot(a, b):
    return jnp.dot(a, b, preferred_element_type=F32)


def _dot_nt(a, b):
    return lax.dot_general(a, b, (((1,), (1,)), ((), ())), preferred_element_type=F32)


def _dot_tn(a, b):
    return lax.dot_general(a, b, (((0,), (0,)), ((), ())), preferred_element_type=F32)


def _split3(x):
    hi = x.astype(BF16)
    r1 = x - hi.astype(F32)
    mid = r1.astype(BF16)
    lo = (r1 - mid.astype(F32)).astype(BF16)
    return hi, mid, lo


def _build_na_bias(rpbp_ref, bias_ref):
    q_i = lax.broadcasted_iota(jnp.int32, (GRID_W, LANES), 0)
    l_i = lax.broadcasted_iota(jnp.int32, (GRID_W, LANES), 1)
    kc = jnp.bitwise_and(l_i, GRID_W - 1)
    q_start = jnp.clip(q_i - NA_KW // 2, 0, GRID_W - NA_KW)
    visible = (kc >= q_start) & (kc < q_start + NA_KW)
    left = l_i < GRID_W
    q_all = jnp.bitwise_and(lax.broadcasted_iota(jnp.int32, (RPB_ROWS * GRID_W, LANES), 0), GRID_W - 1)

    def head_body(h, c):
        rows = rpbp_ref[pl.ds(pl.multiple_of(h * RPB_ROWS, RPB_ROWS), RPB_ROWS), :]
        x = jnp.concatenate([jnp.broadcast_to(rows[ro:ro + 1, :], (GRID_W, LANES)) for ro in range(RPB_ROWS)], axis=0)
        even = pltpu.roll(x, LANES - (GRID_W - 1), axis=1)
        for b in range(GRID_W.bit_length() - 1):
            even = jnp.where(jnp.bitwise_and(q_all, 1 << b) != 0, pltpu.roll(even, 1 << b, axis=1), even)
        odd = pltpu.roll(even, GRID_W, axis=1)
        hp = h // 2
        q0 = pl.multiple_of((h % 2) * GRID_W, GRID_W)
        for var in range(NA_KH):
            for kp in range(NA_KH // 2):
                ro = 2 * kp - var + NA_KH - 1
                tile = jnp.where(left, even[ro * GRID_W:(ro + 1) * GRID_W], odd[(ro + 1) * GRID_W:(ro + 2) * GRID_W])
                bias_ref[hp, var, pl.ds(q0, GRID_W), kp * LANES:(kp + 1) * LANES] = jnp.where(visible, tile, NEG_INF)
        return c

    lax.fori_loop(0, NA_HEADS, head_body, 0)


def _mixer_kernel(layer, x_ref, ng_ref, win_ref, hlb_ref, hgn_ref, rpbp_ref, cw_ref, cb_ref, lng_ref, lnb_ref,
                  y_ref, h_scr, p_scr, oi_scr, qa_scr, d_scr, st_scr, dec_scr, upad_scr, bias_ref,
                  wh_scr, qk_scr, stage_scr):
    t = pl.program_id(1)

    @pl.when((pl.program_id(0) == 0) & (t == 0))
    def _():
        _build_na_bias(rpbp_ref, bias_ref)

    xt = x_ref[...]
    ms = jnp.mean(xt * xt, axis=-1, keepdims=True)
    h_scr[pl.ds(pl.multiple_of(t * X_TILE, X_TILE), X_TILE), :] = (xt * lax.rsqrt(ms + EPS) * ng_ref[...]).astype(BF16)

    @pl.when(t == SEQ // X_TILE - 1)
    def _():
        _mix_sequence(layer, win_ref, hlb_ref, hgn_ref, cw_ref, cb_ref, lng_ref, lnb_ref, y_ref, h_scr, p_scr, oi_scr,
                      qa_scr, d_scr, st_scr, dec_scr, upad_scr, bias_ref, wh_scr, qk_scr, stage_scr)


def _mix_sequence(layer, win_ref, hlb_ref, hgn_ref, cw_ref, cb_ref, lng_ref, lnb_ref, y_scr, h_scr, p_scr, oi_scr,
                  qa_scr, d_scr, st_scr, dec_scr, upad_scr, bias_ref, wh_scr, qk_scr, stage_scr):
    n_row_tiles = SEQ // ROW_TILE

    p_bufs = (p_scr.at[0], p_scr.at[1])
    wh_bufs = (wh_scr.at[0], wh_scr.at[1])
    p_conv = p_bufs[HG_HEADS % 2]
    p_na = p_bufs[1 - HG_HEADS % 2]

    def project(dst, w_ref, col0, ncols):
        def body(i, c):
            r = pl.multiple_of(i * ROW_TILE, ROW_TILE)
            dst[pl.ds(r, ROW_TILE), 0:ncols] = _dot(h_scr[pl.ds(r, ROW_TILE), :], w_ref[:, col0:col0 + ncols])
            return c
        lax.fori_loop(0, n_row_tiles, body, 0)

    def load_head_weights(hd):
        for grp in range(HG_COLS // HG_HEAD_DIM):
            col = grp * HG_DIM + hd * HG_HEAD_DIM
            wh_bufs[hd % 2][:, grp * HG_HEAD_DIM:(grp + 1) * HG_HEAD_DIM] = win_ref[:, col:col + HG_HEAD_DIM]

    hlb = [hlb_ref[d] for d in range(DEPTH)]
    hmax = functools.reduce(jnp.maximum, hlb)
    hexp = [jnp.exp(v - hmax) for v in hlb]
    hsum = functools.reduce(lambda a, b: a + b, hexp)
    hsm = [e / hsum for e in hexp]
    lbs = functools.reduce(lambda a, b: a + b, hsm[:layer + 1]) - hsm[0]

    L = HG_CHUNK
    TL = HG_TILE
    cpt = TL // L
    row_c = lax.broadcasted_iota(jnp.int32, (TL, TL), 0)
    col_c = lax.broadcasted_iota(jnp.int32, (TL, TL), 1)
    same_chunk = jnp.right_shift(row_c, HG_CHUNK_LOG2) == jnp.right_shift(col_c, HG_CHUNK_LOG2)
    lower = same_chunk & (row_c >= col_c)
    upper = same_chunk & (col_c >= row_c)
    lower_b = jnp.where(lower, 1.0, 0.0).astype(BF16)
    upper_b = jnp.where(upper, 1.0, 0.0).astype(BF16)

    def chunk_rows(a, row):
        return [a[c * L + row:c * L + row + 1, :] for c in range(cpt)]

    def spread(rows):
        return jnp.concatenate([jnp.broadcast_to(rw, (L, LANES)) for rw in rows], axis=0)

    load_head_weights(0)
    project(p_bufs[0], wh_bufs[0], 0, HG_COLS)
    for hd in range(HG_HEADS):
        sl = slice(hd * HG_HEAD_DIM, (hd + 1) * HG_HEAD_DIM)
        pc = p_bufs[hd % 2]
        if hd + 1 < HG_HEADS:
            load_head_weights(hd + 1)
            nxt_w, nxt_col0, nxt_cols = wh_bufs[(hd + 1) % 2], 0, HG_COLS
        else:
            nxt_w, nxt_col0, nxt_cols = win_ref, CV_COL0, 2 * CONV_DIM
        directions = (
            (HG_HEAD_DIM, lbs[0:1, sl], lower, lower_b, L // 2 - 1, L - 1),
            (2 * HG_HEAD_DIM, lbs[1:2, sl], upper, upper_b, L // 2, 0),
        )

        def tile_start(i):
            return i * TL if isinstance(i, int) else pl.multiple_of(i * TL, TL)

        def hg_scale(i):
            r = tile_start(i)
            q_raw = pc[pl.ds(r, TL), 0:HG_HEAD_DIM]
            q = q_raw * _sigmoid(q_raw)
            gates = []
            for zcol, lb, mask, cum_op, ref_row, last_row in directions:
                z = pc[pl.ds(r, TL), zcol:zcol + HG_HEAD_DIM]
                t = jnp.exp(-jnp.abs(z))
                big = 1.0 / (1.0 + t)
                small = t * big
                nonneg = z >= 0.0
                f = lb + (1.0 - lb) * jnp.where(nonneg, big, small)
                k = (1.0 - lb) * jnp.where(nonneg, small, big)
                gates.append((k, _dot(cum_op, jnp.concatenate(_split3(jnp.log(f)), axis=1))))
            for d, ((zcol, lb, mask, cum_op, ref_row, last_row), (k, c3)) in enumerate(zip(directions, gates)):
                cum = c3[:, 0:LANES] + c3[:, LANES:2 * LANES] + c3[:, 2 * LANES:3 * LANES]
                ref_rows = chunk_rows(cum, ref_row)
                last_rows = chunk_rows(cum, last_row)
                rel = cum - spread(ref_rows)
                q_in = q * jnp.exp(rel)
                k_in = k * jnp.exp(-rel)
                qa_scr[d, pl.ds(r, TL), :] = (q_in * spread([jnp.exp(rw) for rw in ref_rows])).astype(BF16)
                k_d = (k_in * spread([jnp.exp(lr - rr) for lr, rr in zip(last_rows, ref_rows)])).astype(BF16)
                for cc in range(cpt):
                    dec_scr[d, i * cpt + cc] = jnp.broadcast_to(jnp.exp(last_rows[cc]), (SUBLANES, LANES))
                stage_scr[d, 0] = q_in.astype(BF16)
                stage_scr[d, 1] = k_in.astype(BF16)
                stage_scr[d, 2] = k_d

        def hg_contract(i):
            r = tile_start(i)
            vb = pc[pl.ds(r, TL), 3 * HG_HEAD_DIM:4 * HG_HEAD_DIM].astype(BF16)
            qk = [_dot_nt(stage_scr[d, 0], stage_scr[d, 1]) for d in range(len(directions))]
            o_intra = None
            for d, (zcol, lb, mask, cum_op, ref_row, last_row) in enumerate(directions):
                o_dir = _dot(jnp.where(mask, qk[d], 0.0).astype(BF16), vb)
                o_intra = o_dir if o_intra is None else o_intra + o_dir
                k_d = stage_scr[d, 2]
                for cc in range(cpt):
                    d_scr[d, i * cpt + cc] = _dot_tn(vb[cc * L:(cc + 1) * L], k_d[cc * L:(cc + 1) * L])
            oi_scr[pl.ds(r, TL), :] = o_intra

        def project_next(i, pn=p_bufs[(hd + 1) % 2], w=nxt_w, col0=nxt_col0, ncols=nxt_cols):
            r = tile_start(i)
            pn[pl.ds(r, TL), 0:ncols] = _dot(h_scr[pl.ds(r, TL), :], w[:, col0:col0 + ncols])

        def hg_local_body(i, c):
            hg_contract(i)
            hg_scale(i + 1)
            project_next(i + 1)
            return c

        hg_scale(0)
        project_next(0)
        lax.fori_loop(0, SEQ // TL - 1, hg_local_body, 0, unroll=HG_LOCAL_UNROLL)
        hg_contract(SEQ // TL - 1)

        def hg_scan_body(n, carry):
            s_f, s_b = carry
            m = HG_NCHUNK - 1 - n
            st_scr[0, n] = s_f.astype(BF16)
            st_scr[1, m] = s_b.astype(BF16)
            s_f = s_f * dec_scr[0, n][0:1, :] + d_scr[0, n]
            s_b = s_b * dec_scr[1, m][0:1, :] + d_scr[1, m]
            return s_f, s_b

        s_zero = jnp.zeros((HG_HEAD_DIM, HG_HEAD_DIM), F32)
        lax.fori_loop(0, HG_NCHUNK, hg_scan_body, (s_zero, s_zero))

        def hg_out_body(i, c):
            r = pl.multiple_of(i * HG_OUT_TILE, HG_OUT_TILE)
            inter = []
            for cc in range(HG_OUT_TILE // L):
                n = i * (HG_OUT_TILE // L) + cc
                rc = pl.multiple_of(r + cc * L, L)
                qa = jnp.concatenate([qa_scr[0, pl.ds(rc, L), :], qa_scr[1, pl.ds(rc, L), :]], axis=1)
                st = jnp.concatenate([st_scr[0, n], st_scr[1, n]], axis=1)
                inter.append(_dot_nt(qa, st))
            o = oi_scr[pl.ds(r, HG_OUT_TILE), :] + jnp.concatenate(inter, axis=0)
            o = o * lax.rsqrt(jnp.mean(o * o, axis=-1, keepdims=True) + EPS) * hgn_ref[...]
            gate = pc[pl.ds(r, HG_OUT_TILE), 4 * HG_HEAD_DIM:5 * HG_HEAD_DIM]
            y_scr[pl.ds(r, HG_OUT_TILE), sl] = (o * (gate * _sigmoid(gate))).astype(BF16)
            return c

        lax.fori_loop(0, SEQ // HG_OUT_TILE, hg_out_body, 0, unroll=2)

    upad_scr[0:CONV_PAD, :] = jnp.zeros((CONV_PAD, CONV_DIM), F32)
    upad_scr[CONV_PAD + SEQ:CONV_PAD + SEQ + CONV_PAD, :] = jnp.zeros((CONV_PAD, CONV_DIM), F32)

    def glu_body(i, c):
        r = pl.multiple_of(i * ROW_TILE, ROW_TILE)
        a = p_conv[pl.ds(r, ROW_TILE), 0:CONV_DIM]
        gate = p_conv[pl.ds(r, ROW_TILE), CONV_DIM:2 * CONV_DIM]
        upad_scr[pl.ds(CONV_PAD + r, ROW_TILE), :] = a * _sigmoid(gate)
        return c

    lax.fori_loop(0, n_row_tiles, glu_body, 0)

    scale = NA_HEAD_DIM ** -0.5

    def na_project_tile(r, rows):
        res = _dot(h_scr[pl.ds(r, rows), :], win_ref[:, NA_COL0:NA_COL0 + 3 * NA_DIM])
        p_na[pl.ds(r, rows), 0:NA_DIM] = res[:, 0:NA_DIM] * scale
        h_scr[pl.ds(r, rows), 0:2 * NA_DIM] = res[:, NA_DIM:3 * NA_DIM].astype(BF16)

    def conv_tile(r):
        off = CONV_PAD - CONV_WIDTH // 2
        groups = -(-CONV_WIDTH // SUBLANES)
        win_rows = CONV_TILE + 2 * CONV_PAD
        for c0 in range(0, CONV_DIM, LANES):
            win = upad_scr[pl.ds(r, win_rows), c0:c0 + LANES]
            acc = jnp.zeros((CONV_TILE, LANES), F32) + cb_ref[:, c0:c0 + LANES]
            shifted = win
            for s in range(SUBLANES):
                step = off if s == 0 else 1
                if step:
                    shifted = pltpu.roll(shifted, win_rows - step, axis=0)
                for a in range(groups):
                    j = SUBLANES * a + s
                    if j < CONV_WIDTH:
                        acc = acc + shifted[SUBLANES * a:SUBLANES * a + CONV_TILE, :] * cw_ref[j:j + 1, c0:c0 + LANES]
            p_conv[pl.ds(r, CONV_TILE), c0:c0 + LANES] = acc

    def conv_body(i, c):
        r = pl.multiple_of(i * CONV_OUTER_TILE, CONV_OUTER_TILE)
        na_project_tile(r, CONV_OUTER_TILE)
        for sub in range(CONV_OUTER_TILE // CONV_TILE):
            conv_tile(pl.multiple_of(r + sub * CONV_TILE, CONV_TILE))
        return c

    lax.fori_loop(0, SEQ // CONV_OUTER_TILE, conv_body, 0)

    def conv_norm_body(i, c):
        r = pl.multiple_of(i * ROW_TILE, ROW_TILE)
        acc = p_conv[pl.ds(r, ROW_TILE), 0:CONV_DIM]
        mu = jnp.mean(acc, axis=-1, keepdims=True)
        d = acc - mu
        var = jnp.mean(d * d, axis=-1, keepdims=True)
        u = d * lax.rsqrt(var + EPS) * lng_ref[...] + lnb_ref[...]
        y_scr[pl.ds(r, ROW_TILE), HG_DIM + NA_DIM:D_MODEL] = (u * _sigmoid(u)).astype(BF16)
        return c

    lax.fori_loop(0, n_row_tiles, conv_norm_body, 0)

    lane = lax.broadcasted_iota(jnp.int32, (1, LANES), 1)
    first_half = lane < NA_HEAD_DIM

    n_na_steps = ROWS // NA_ROWS_PER_STEP

    def na_units(step):
        units = []
        for dr in range(NA_ROWS_PER_STEP):
            r = step * NA_ROWS_PER_STEP + dr
            if isinstance(r, int):
                r0 = min(max(r - NA_KH // 2, 0), ROWS - NA_KH)
                qrow, krow = r * GRID_W, r0 * GRID_W
            else:
                r0 = jnp.clip(r - NA_KH // 2, 0, ROWS - NA_KH)
                qrow, krow = pl.multiple_of(r * GRID_W, GRID_W), pl.multiple_of(r0 * GRID_W, GRID_W)
            for hp in range(NA_HEADS // 2):
                units.append((qrow, krow, r - r0, hp))
        return units

    def na_scores(units):
        for u, (qrow, krow, var, hp) in enumerate(units):
            c0 = hp * LANES
            qt = p_na[pl.ds(qrow, GRID_W), c0:c0 + LANES]
            qm = jnp.concatenate([jnp.where(first_half, qt, 0.0), jnp.where(first_half, 0.0, qt)], axis=0).astype(BF16)
            qk_scr[u] = _dot_nt(qm, h_scr[pl.ds(krow, NA_KEYS), c0:c0 + LANES])

    na_scores(na_units(0))

    def na_body(i, c):
        units = na_units(i)
        probs = []
        for u, (qrow, krow, var, hp) in enumerate(units):
            bt = bias_ref[hp, var]
            s = jnp.where(bt > 0.5 * NEG_INF, qk_scr[u] + bt, NEG_INF)
            e = jnp.exp(s - jnp.max(s, axis=-1, keepdims=True))
            probs.append((e.astype(BF16), jnp.sum(e, axis=-1, keepdims=True)))
        na_scores(na_units(jnp.minimum(i + 1, n_na_steps - 1)))
        for (qrow, krow, var, hp), (e, den) in zip(units, probs):
            c0 = hp * LANES
            o = _dot(e, h_scr[pl.ds(krow, NA_KEYS), NA_DIM + c0:NA_DIM + c0 + LANES]) / den
            y_scr[pl.ds(qrow, GRID_W), HG_DIM + c0:HG_DIM + c0 + LANES] = jnp.where(
                first_half, o[0:GRID_W], o[GRID_W:2 * GRID_W]).astype(BF16)
        return c

    lax.fori_loop(0, n_na_steps, na_body, 0)


def _ffn_kernel(final, x_ref, y_ref, wout_ref, ng_ref, wgu_ref, wd_ref, fg_ref, o_ref, h_scr, acc_scr, x_scr):
    x = x_ref[...] + _dot(y_ref[...], wout_ref[...])
    x_scr[...] = x
    ms = jnp.mean(x * x, axis=-1, keepdims=True)
    h_scr[...] = (x * lax.rsqrt(ms + EPS) * ng_ref[...]).astype(BF16)
    for c in range(FFN_NCHUNK):
        gt = _dot(h_scr[...], wgu_ref[:, FFN_CHUNK * c:FFN_CHUNK * (c + 1)])
        up = _dot(h_scr[...], wgu_ref[:, FFN_DIM + FFN_CHUNK * c:FFN_DIM + FFN_CHUNK * (c + 1)])
        a = (gt * _sigmoid(gt) * up).astype(BF16)
        d = _dot(a, wd_ref[FFN_CHUNK * c:FFN_CHUNK * (c + 1), :])
        if c == 0:
            acc_scr[...] = d
        else:
            acc_scr[...] += d
    y = x_scr[...] + acc_scr[...]
    if final:
        y = y * lax.rsqrt(jnp.mean(y * y, axis=-1, keepdims=True) + EPS) * fg_ref[...]
    o_ref[...] = y


def _resident(shape):
    nd = len(shape)
    return pl.BlockSpec(shape, lambda *_: (0,) * nd, pipeline_mode=pl.Buffered(1))


def _layer_slice(layer, shape):
    nd = len(shape)
    return pl.BlockSpec((None,) + tuple(shape[1:]), lambda *_: (layer,) + (0,) * (nd - 1), pipeline_mode=pl.Buffered(1))


def _mixer_call(layer, x, ng, win, hlb, hgn, rpbp, cw, cb, lng, lnb):
    batch = x.shape[0]
    stacked = (ng, win, None, hgn, rpbp, cw, cb, lng, lnb)
    return pl.pallas_call(
        functools.partial(_mixer_kernel, layer),
        grid=(batch, SEQ // X_TILE),
        in_specs=[pl.BlockSpec((None, X_TILE, D_MODEL), lambda b, t: (b, t, 0))]
        + [_resident(hlb.shape) if a is None else _layer_slice(layer, a.shape) for a in stacked],
        out_specs=pl.BlockSpec((None, SEQ, D_MODEL), lambda b, t: (b, 0, 0)),
        out_shape=jax.ShapeDtypeStruct(x.shape, BF16),
        scratch_shapes=[
            pltpu.VMEM((SEQ, D_MODEL), BF16),
            pltpu.VMEM((2, SEQ, P_COLS), F32),
            pltpu.VMEM((SEQ, HG_HEAD_DIM), F32),
            pltpu.VMEM((2, SEQ, HG_HEAD_DIM), BF16),
            pltpu.VMEM((2, HG_NCHUNK, HG_HEAD_DIM, HG_HEAD_DIM), F32),
            pltpu.VMEM((2, HG_NCHUNK, HG_HEAD_DIM, HG_HEAD_DIM), BF16),
            pltpu.VMEM((2, HG_NCHUNK, SUBLANES, LANES), F32),
            pltpu.VMEM((SEQ + 2 * CONV_PAD, CONV_DIM), F32),
            pltpu.VMEM((NA_HEADS // 2, NA_KH, 2 * GRID_W, NA_KEYS), F32),
            pltpu.VMEM((2, D_MODEL, HG_COLS), BF16),
            pltpu.VMEM((NA_ROWS_PER_STEP * NA_HEADS // 2, 2 * GRID_W, NA_KEYS), F32),
            pltpu.VMEM((2, 3, HG_TILE, HG_HEAD_DIM), BF16),
        ],
        compiler_params=pltpu.CompilerParams(dimension_semantics=("arbitrary", "arbitrary"),
                                             vmem_limit_bytes=VMEM_LIMIT_BYTES),
        name=f"mixer_l{layer}",
    )(x, ng, win, hlb, hgn, rpbp, cw, cb, lng, lnb)


def _ffn_call(layer, final, x2d, y2d, wout, ng, wgu, wd, fg):
    n_tok = x2d.shape[0]
    tile = pl.BlockSpec((FFN_TILE, D_MODEL), lambda i: (i, 0))
    return pl.pallas_call(
        functools.partial(_ffn_kernel, final),
        grid=(n_tok // FFN_TILE,),
        in_specs=[tile, tile] + [_layer_slice(layer, a.shape) for a in (wout, ng, wgu, wd)] + [_resident(fg.shape)],
        out_specs=tile,
        out_shape=jax.ShapeDtypeStruct(x2d.shape, x2d.dtype),
        scratch_shapes=[pltpu.VMEM((FFN_TILE, D_MODEL), BF16), pltpu.VMEM((FFN_TILE, D_MODEL), F32),
                        pltpu.VMEM((FFN_TILE, D_MODEL), F32)],
        compiler_params=pltpu.CompilerParams(dimension_semantics=("arbitrary",), vmem_limit_bytes=VMEM_LIMIT_BYTES),
        name=f"ffn_l{layer}",
    )(x2d, y2d, wout, ng, wgu, wd, fg)


def _pad_rpb(rpb):
    v = jnp.pad(rpb, ((0, 0), (0, 0), (0, RPB_ROWS - NA_NREL_H), (RPB_LEAD, LANES - RPB_LEAD - NA_NREL_W)))
    return v.reshape(DEPTH, NA_HEADS * RPB_ROWS, LANES)


def kernel(x, mix_norm_g, w_in, hg_lower_bounds, hg_norm_g, na_rpb, conv_w, conv_b, conv_ln_g, conv_ln_b, w_out,
           ffn_norm_g, w_gate_up, w_down, final_norm_g):
    batch, seq, d = x.shape
    assert (seq, d) == (SEQ, D_MODEL) and w_in.shape == (DEPTH, D_MODEL, D_IN)
    row = lambda p: p.astype(F32).reshape(DEPTH, 1, p.shape[-1])
    win, wout, wgu, wd = (w.astype(BF16) for w in (w_in, w_out, w_gate_up, w_down))
    hlb = hg_lower_bounds.astype(F32)
    rpbp = _pad_rpb(na_rpb.astype(F32))
    mix_g, hgn, cb, lng, lnb, ffn_g = (row(p) for p in (mix_norm_g, hg_norm_g, conv_b, conv_ln_g, conv_ln_b, ffn_norm_g))
    fg = final_norm_g.astype(F32).reshape(1, D_MODEL)
    for l in range(DEPTH):
        y = _mixer_call(l, x, mix_g, win, hlb, hgn, rpbp, conv_w.astype(F32), cb, lng, lnb)
        x2d = _ffn_call(l, l == DEPTH - 1, x.reshape(batch * seq, d), y.reshape(batch * seq, d), wout, ffn_g, wgu, wd, fg)
        x = x2d.reshape(batch, seq, d)
    return x
```

```python
import functools

import numpy as np
import jax
import jax.numpy as jnp
from jax import lax
from jax.experimental import pallas as pl
from jax.experimental.pallas import tpu as pltpu

D_MODEL = 1024
SEQ = 2048
DEPTH = 2
GRID_W = 64
ROWS = SEQ // GRID_W
HG_DIM = 512
HG_HEADS = 4
HG_HEAD_DIM = 128
NA_DIM = 256
NA_HEADS = 4
NA_HEAD_DIM = 64
NA_KH = 8
NA_KW = 16
CONV_DIM = 256
CONV_WIDTH = 31
D_IN = 5 * HG_DIM + 3 * NA_DIM + 2 * CONV_DIM
FFN_DIM = 2816
EPS = 1e-6
NEG_INF = -1e30

LANES = 128
SUBLANES = 8
VMEM_LIMIT_BYTES = 60 * 1024 * 1024

ROW_TILE = 1024
HG_CHUNK = 64
HG_CHUNK_LOG2 = HG_CHUNK.bit_length() - 1
assert HG_CHUNK == 1 << HG_CHUNK_LOG2
HG_NCHUNK = SEQ // HG_CHUNK
HG_TILE = 256
HG_OUT_TILE = 512
HG_LOCAL_UNROLL = 1
X_TILE = 512
HG_COLS = 5 * HG_HEAD_DIM
NA_COL0 = 5 * HG_DIM
CV_COL0 = NA_COL0 + 3 * NA_DIM
NA_KEYS = NA_KH * GRID_W
NA_NREL_H = 2 * NA_KH - 1
NA_NREL_W = 2 * NA_KW - 1
RPB_LEAD = GRID_W - NA_KW
RPB_ROWS = 16
assert RPB_ROWS >= NA_NREL_H and RPB_ROWS % SUBLANES == 0
NA_ROWS_PER_STEP = 2
CONV_TILE = 64
CONV_OUTER_TILE = 256
CONV_PAD = 16
P_COLS = HG_COLS
assert P_COLS >= 2 * CONV_DIM and P_COLS >= NA_DIM
FFN_TILE = 512
FFN_CHUNK = 256
FFN_NCHUNK = FFN_DIM // FFN_CHUNK

F32 = jnp.float32
BF16 = jnp.bfloat16


def _sigmoid(x):
    return 1.0 / (1.0 + jnp.exp(-x))


def _dot(a, b):
    return jnp.dot(a, b, preferred_element_type=F32)


def _dot_nt(a, b):
    return lax.dot_general(a, b, (((1,), (1,)), ((), ())), preferred_element_type=F32)


def _dot_tn(a, b):
    return lax.dot_general(a, b, (((0,), (0,)), ((), ())), preferred_element_type=F32)


def _split3(x):
    hi = x.astype(BF16)
    r1 = x - hi.astype(F32)
    mid = r1.astype(BF16)
    lo = (r1 - mid.astype(F32)).astype(BF16)
    return hi, mid, lo


def _build_na_bias(rpbp_ref, bias_ref):
    q_i = lax.broadcasted_iota(jnp.int32, (GRID_W, LANES), 0)
    l_i = lax.broadcasted_iota(jnp.int32, (GRID_W, LANES), 1)
    kc = jnp.bitwise_and(l_i, GRID_W - 1)
    q_start = jnp.clip(q_i - NA_KW // 2, 0, GRID_W - NA_KW)
    visible = (kc >= q_start) & (kc < q_start + NA_KW)
    left = l_i < GRID_W
    q_all = jnp.bitwise_and(lax.broadcasted_iota(jnp.int32, (RPB_ROWS * GRID_W, LANES), 0), GRID_W - 1)

    def head_body(h, c):
        rows = rpbp_ref[pl.ds(pl.multiple_of(h * RPB_ROWS, RPB_ROWS), RPB_ROWS), :]
        x = jnp.concatenate([jnp.broadcast_to(rows[ro:ro + 1, :], (GRID_W, LANES)) for ro in range(RPB_ROWS)], axis=0)
        even = pltpu.roll(x, LANES - (GRID_W - 1), axis=1)
        for b in range(GRID_W.bit_length() - 1):
            even = jnp.where(jnp.bitwise_and(q_all, 1 << b) != 0, pltpu.roll(even, 1 << b, axis=1), even)
        odd = pltpu.roll(even, GRID_W, axis=1)
        hp = h // 2
        q0 = pl.multiple_of((h % 2) * GRID_W, GRID_W)
        for var in range(NA_KH):
            for kp in range(NA_KH // 2):
                ro = 2 * kp - var + NA_KH - 1
                tile = jnp.where(left, even[ro * GRID_W:(ro + 1) * GRID_W], odd[(ro + 1) * GRID_W:(ro + 2) * GRID_W])
                bias_ref[hp, var, pl.ds(q0, GRID_W), kp * LANES:(kp + 1) * LANES] = jnp.where(visible, tile, NEG_INF)
        return c

    lax.fori_loop(0, NA_HEADS, head_body, 0)


def _mixer_kernel(layer, x_ref, ng_ref, win_ref, hlb_ref, hgn_ref, rpbp_ref, cw_ref, cb_ref, lng_ref, lnb_ref,
                  y_ref, h_scr, p_scr, oi_scr, qa_scr, d_scr, st_scr, dec_scr, upad_scr, bias_ref,
                  wh_scr, qk_scr, stage_scr):
    t = pl.program_id(1)

    @pl.when((pl.program_id(0) == 0) & (t == 0))
    def _():
        _build_na_bias(rpbp_ref, bias_ref)

    xt = x_ref[...]
    ms = jnp.mean(xt * xt, axis=-1, keepdims=True)
    h_scr[pl.ds(pl.multiple_of(t * X_TILE, X_TILE), X_TILE), :] = (xt * lax.rsqrt(ms + EPS) * ng_ref[...]).astype(BF16)

    @pl.when(t == SEQ // X_TILE - 1)
    def _():
        _mix_sequence(layer, win_ref, hlb_ref, hgn_ref, cw_ref, cb_ref, lng_ref, lnb_ref, y_ref, h_scr, p_scr, oi_scr,
                      qa_scr, d_scr, st_scr, dec_scr, upad_scr, bias_ref, wh_scr, qk_scr, stage_scr)


def _mix_sequence(layer, win_ref, hlb_ref, hgn_ref, cw_ref, cb_ref, lng_ref, lnb_ref, y_scr, h_scr, p_scr, oi_scr,
                  qa_scr, d_scr, st_scr, dec_scr, upad_scr, bias_ref, wh_scr, qk_scr, stage_scr):
    n_row_tiles = SEQ // ROW_TILE

    p_bufs = (p_scr.at[0], p_scr.at[1])
    wh_bufs = (wh_scr.at[0], wh_scr.at[1])
    p_conv = p_bufs[HG_HEADS % 2]
    p_na = p_bufs[1 - HG_HEADS % 2]

    def project(dst, w_ref, col0, ncols):
        def body(i, c):
            r = pl.multiple_of(i * ROW_TILE, ROW_TILE)
            dst[pl.ds(r, ROW_TILE), 0:ncols] = _dot(h_scr[pl.ds(r, ROW_TILE), :], w_ref[:, col0:col0 + ncols])
            return c
        lax.fori_loop(0, n_row_tiles, body, 0)

    def load_head_weights(hd):
        for grp in range(HG_COLS // HG_HEAD_DIM):
            col = grp * HG_DIM + hd * HG_HEAD_DIM
            wh_bufs[hd % 2][:, grp * HG_HEAD_DIM:(grp + 1) * HG_HEAD_DIM] = win_ref[:, col:col + HG_HEAD_DIM]

    hlb = [hlb_ref[d] for d in range(DEPTH)]
    hmax = functools.reduce(jnp.maximum, hlb)
    hexp = [jnp.exp(v - hmax) for v in hlb]
    hsum = functools.reduce(lambda a, b: a + b, hexp)
    hsm = [e / hsum for e in hexp]
    lbs = functools.reduce(lambda a, b: a + b, hsm[:layer + 1]) - hsm[0]

    L = HG_CHUNK
    TL = HG_TILE
    cpt = TL // L
    row_c = lax.broadcasted_iota(jnp.int32, (TL, TL), 0)
    col_c = lax.broadcasted_iota(jnp.int32, (TL, TL), 1)
    same_chunk = jnp.right_shift(row_c, HG_CHUNK_LOG2) == jnp.right_shift(col_c, HG_CHUNK_LOG2)
    lower = same_chunk & (row_c >= col_c)
    upper = same_chunk & (col_c >= row_c)
    lower_b = jnp.where(lower, 1.0, 0.0).astype(BF16)
    upper_b = jnp.where(upper, 1.0, 0.0).astype(BF16)

    def chunk_rows(a, row):
        return [a[c * L + row:c * L + row + 1, :] for c in range(cpt)]

    def spread(rows):
        return jnp.concatenate([jnp.broadcast_to(rw, (L, LANES)) for rw in rows], axis=0)

    load_head_weights(0)
    project(p_bufs[0], wh_bufs[0], 0, HG_COLS)
    for hd in range(HG_HEADS):
        sl = slice(hd * HG_HEAD_DIM, (hd + 1) * HG_HEAD_DIM)
        pc = p_bufs[hd % 2]
        if hd + 1 < HG_HEADS:
            load_head_weights(hd + 1)
            nxt_w, nxt_col0, nxt_cols = wh_bufs[(hd + 1) % 2], 0, HG_COLS
        else:
            nxt_w, nxt_col0, nxt_cols = win_ref, CV_COL0, 2 * CONV_DIM
        directions = (
            (HG_HEAD_DIM, lbs[0:1, sl], lower, lower_b, L // 2 - 1, L - 1),
            (2 * HG_HEAD_DIM, lbs[1:2, sl], upper, upper_b, L // 2, 0),
        )

        def tile_start(i):
            return i * TL if isinstance(i, int) else pl.multiple_of(i * TL, TL)

        def hg_scale(i):
            r = tile_start(i)
            q_raw = pc[pl.ds(r, TL), 0:HG_HEAD_DIM]
            q = q_raw * _sigmoid(q_raw)
            gates = []
            for zcol, lb, mask, cum_op, ref_row, last_row in directions:
                z = pc[pl.ds(r, TL), zcol:zcol + HG_HEAD_DIM]
                t = jnp.exp(-jnp.abs(z))
                big = 1.0 / (1.0 + t)
                small = t * big
                nonneg = z >= 0.0
                f = lb + (1.0 - lb) * jnp.where(nonneg, big, small)
                k = (1.0 - lb) * jnp.where(nonneg, small, big)
                gates.append((k, _dot(cum_op, jnp.concatenate(_split3(jnp.log(f)), axis=1))))
            for d, ((zcol, lb, mask, cum_op, ref_row, last_row), (k, c3)) in enumerate(zip(directions, gates)):
                cum = c3[:, 0:LANES] + c3[:, LANES:2 * LANES] + c3[:, 2 * LANES:3 * LANES]
                ref_rows = chunk_rows(cum, ref_row)
                last_rows = chunk_rows(cum, last_row)
                rel = cum - spread(ref_rows)
                q_in = q * jnp.exp(rel)
                k_in = k * jnp.exp(-rel)
                qa_scr[d, pl.ds(r, TL), :] = (q_in * spread([jnp.exp(rw) for rw in ref_rows])).astype(BF16)
                k_d = (k_in * spread([jnp.exp(lr - rr) for lr, rr in zip(last_rows, ref_rows)])).astype(BF16)
                for cc in range(cpt):
                    dec_scr[d, i * cpt + cc] = jnp.broadcast_to(jnp.exp(last_rows[cc]), (SUBLANES, LANES))
                stage_scr[d, 0] = q_in.astype(BF16)
                stage_scr[d, 1] = k_in.astype(BF16)
                stage_scr[d, 2] = k_d

        def hg_contract(i):
            r = tile_start(i)
            vb = pc[pl.ds(r, TL), 3 * HG_HEAD_DIM:4 * HG_HEAD_DIM].astype(BF16)
            qk = [_dot_nt(stage_scr[d, 0], stage_scr[d, 1]) for d in range(len(directions))]
            o_intra = None
            for d, (zcol, lb, mask, cum_op, ref_row, last_row) in enumerate(directions):
                o_dir = _dot(jnp.where(mask, qk[d], 0.0).astype(BF16), vb)
                o_intra = o_dir if o_intra is None else o_intra + o_dir
                k_d = stage_scr[d, 2]
                for cc in range(cpt):
                    d_scr[d, i * cpt + cc] = _dot_tn(vb[cc * L:(cc + 1) * L], k_d[cc * L:(cc + 1) * L])
            oi_scr[pl.ds(r, TL), :] = o_intra

        def project_next(i, pn=p_bufs[(hd + 1) % 2], w=nxt_w, col0=nxt_col0, ncols=nxt_cols):
            r = tile_start(i)
            pn[pl.ds(r, TL), 0:ncols] = _dot(h_scr[pl.ds(r, TL), :], w[:, col0:col0 + ncols])

        def hg_local_body(i, c):
            hg_contract(i)
            hg_scale(i + 1)
            project_next(i + 1)
            return c

        hg_scale(0)
        project_next(0)
        lax.fori_loop(0, SEQ // TL - 1, hg_local_body, 0, unroll=HG_LOCAL_UNROLL)
        hg_contract(SEQ // TL - 1)

        def hg_scan_body(n, carry):
            s_f, s_b = carry
            m = HG_NCHUNK - 1 - n
            st_scr[0, n] = s_f.astype(BF16)
            st_scr[1, m] = s_b.astype(BF16)
            s_f = s_f * dec_scr[0, n][0:1, :] + d_scr[0, n]
            s_b = s_b * dec_scr[1, m][0:1, :] + d_scr[1, m]
            return s_f, s_b

        s_zero = jnp.zeros((HG_HEAD_DIM, HG_HEAD_DIM), F32)
        lax.fori_loop(0, HG_NCHUNK, hg_scan_body, (s_zero, s_zero))

        def hg_out_body(i, c):
            r = pl.multiple_of(i * HG_OUT_TILE, HG_OUT_TILE)
            inter = []
            for cc in range(HG_OUT_TILE // L):
                n = i * (HG_OUT_TILE // L) + cc
                rc = pl.multiple_of(r + cc * L, L)
                qa = jnp.concatenate([qa_scr[0, pl.ds(rc, L), :], qa_scr[1, pl.ds(rc, L), :]], axis=1)
                st = jnp.concatenate([st_scr[0, n], st_scr[1, n]], axis=1)
                inter.append(_dot_nt(qa, st))
            o = oi_scr[pl.ds(r, HG_OUT_TILE), :] + jnp.concatenate(inter, axis=0)
            o = o * lax.rsqrt(jnp.mean(o * o, axis=-1, keepdims=True) + EPS) * hgn_ref[...]
            gate = pc[pl.ds(r, HG_OUT_TILE), 4 * HG_HEAD_DIM:5 * HG_HEAD_DIM]
            y_scr[pl.ds(r, HG_OUT_TILE), sl] = (o * (gate * _sigmoid(gate))).astype(BF16)
            return c

        lax.fori_loop(0, SEQ // HG_OUT_TILE, hg_out_body, 0, unroll=2)

    upad_scr[0:CONV_PAD, :] = jnp.zeros((CONV_PAD, CONV_DIM), F32)
    upad_scr[CONV_PAD + SEQ:CONV_PAD + SEQ + CONV_PAD, :] = jnp.zeros((CONV_PAD, CONV_DIM), F32)

    def glu_body(i, c):
        r = pl.multiple_of(i * ROW_TILE, ROW_TILE)
        a = p_conv[pl.ds(r, ROW_TILE), 0:CONV_DIM]
        gate = p_conv[pl.ds(r, ROW_TILE), CONV_DIM:2 * CONV_DIM]
        upad_scr[pl.ds(CONV_PAD + r, ROW_TILE), :] = a * _sigmoid(gate)
        return c

    lax.fori_loop(0, n_row_tiles, glu_body, 0)

    scale = NA_HEAD_DIM ** -0.5

    def na_project_tile(r, rows):
        res = _dot(h_scr[pl.ds(r, rows), :], win_ref[:, NA_COL0:NA_COL0 + 3 * NA_DIM])
        p_na[pl.ds(r, rows), 0:NA_DIM] = res[:, 0:NA_DIM] * scale
        h_scr[pl.ds(r, rows), 0:2 * NA_DIM] = res[:, NA_DIM:3 * NA_DIM].astype(BF16)

    def conv_tile(r):
        off = CONV_PAD - CONV_WIDTH // 2
        groups = -(-CONV_WIDTH // SUBLANES)
        win_rows = CONV_TILE + 2 * CONV_PAD
        for c0 in range(0, CONV_DIM, LANES):
            win = upad_scr[pl.ds(r, win_rows), c0:c0 + LANES]
            acc = jnp.zeros((CONV_TILE, LANES), F32) + cb_ref[:, c0:c0 + LANES]
            shifted = win
            for s in range(SUBLANES):
                step = off if s == 0 else 1
                if step:
                    shifted = pltpu.roll(shifted, win_rows - step, axis=0)
                for a in range(groups):
                    j = SUBLANES * a + s
                    if j < CONV_WIDTH:
                        acc = acc + shifted[SUBLANES * a:SUBLANES * a + CONV_TILE, :] * cw_ref[j:j + 1, c0:c0 + LANES]
            p_conv[pl.ds(r, CONV_TILE), c0:c0 + LANES] = acc

    def conv_body(i, c):
        r = pl.multiple_of(i * CONV_OUTER_TILE, CONV_OUTER_TILE)
        na_project_tile(r, CONV_OUTER_TILE)
        for sub in range(CONV_OUTER_TILE // CONV_TILE):
            conv_tile(pl.multiple_of(r + sub * CONV_TILE, CONV_TILE))
        return c

    lax.fori_loop(0, SEQ // CONV_OUTER_TILE, conv_body, 0)

    def conv_norm_body(i, c):
        r = pl.multiple_of(i * ROW_TILE, ROW_TILE)
        acc = p_conv[pl.ds(r, ROW_TILE), 0:CONV_DIM]
        mu = jnp.mean(acc, axis=-1, keepdims=True)
        d = acc - mu
        var = jnp.mean(d * d, axis=-1, keepdims=True)
        u = d * lax.rsqrt(var + EPS) * lng_ref[...] + lnb_ref[...]
        y_scr[pl.ds(r, ROW_TILE), HG_DIM + NA_DIM:D_MODEL] = (u * _sigmoid(u)).astype(BF16)
        return c

    lax.fori_loop(0, n_row_tiles, conv_norm_body, 0)

    lane = lax.broadcasted_iota(jnp.int32, (1, LANES), 1)
    first_half = lane < NA_HEAD_DIM

    n_na_steps = ROWS // NA_ROWS_PER_STEP

    def na_units(step):
        units = []
        for dr in range(NA_ROWS_PER_STEP):
            r = step * NA_ROWS_PER_STEP + dr
            if isinstance(r, int):
                r0 = min(max(r - NA_KH // 2, 0), ROWS - NA_KH)
                qrow, krow = r * GRID_W, r0 * GRID_W
            else:
                r0 = jnp.clip(r - NA_KH // 2, 0, ROWS - NA_KH)
                qrow, krow = pl.multiple_of(r * GRID_W, GRID_W), pl.multiple_of(r0 * GRID_W, GRID_W)
            for hp in range(NA_HEADS // 2):
                units.append((qrow, krow, r - r0, hp))
        return units

    def na_scores(units):
        for u, (qrow, krow, var, hp) in enumerate(units):
            c0 = hp * LANES
            qt = p_na[pl.ds(qrow, GRID_W), c0:c0 + LANES]
            qm = jnp.concatenate([jnp.where(first_half, qt, 0.0), jnp.where(first_half, 0.0, qt)], axis=0).astype(BF16)
            qk_scr[u] = _dot_nt(qm, h_scr[pl.ds(krow, NA_KEYS), c0:c0 + LANES])

    na_scores(na_units(0))

    def na_body(i, c):
        units = na_units(i)
        probs = []
        for u, (qrow, krow, var, hp) in enumerate(units):
            bt = bias_ref[hp, var]
            s = jnp.where(bt > 0.5 * NEG_INF, qk_scr[u] + bt, NEG_INF)
            e = jnp.exp(s - jnp.max(s, axis=-1, keepdims=True))
            probs.append((e.astype(BF16), jnp.sum(e, axis=-1, keepdims=True)))
        na_scores(na_units(jnp.minimum(i + 1, n_na_steps - 1)))
        for (qrow, krow, var, hp), (e, den) in zip(units, probs):
            c0 = hp * LANES
            o = _dot(e, h_scr[pl.ds(krow, NA_KEYS), NA_DIM + c0:NA_DIM + c0 + LANES]) / den
            y_scr[pl.ds(qrow, GRID_W), HG_DIM + c0:HG_DIM + c0 + LANES] = jnp.where(
                first_half, o[0:GRID_W], o[GRID_W:2 * GRID_W]).astype(BF16)
        return c

    lax.fori_loop(0, n_na_steps, na_body, 0)


def _ffn_kernel(final, x_ref, y_ref, wout_ref, ng_ref, wgu_ref, wd_ref, fg_ref, o_ref, h_scr, acc_scr, x_scr):
    x = x_ref[...] + _dot(y_ref[...], wout_ref[...].astype(BF16))
    x_scr[...] = x
    ms = jnp.mean(x * x, axis=-1, keepdims=True)
    h_scr[...] = (x * lax.rsqrt(ms + EPS) * ng_ref[...]).astype(BF16)
    for c in range(FFN_NCHUNK):
        gt = _dot(h_scr[...], wgu_ref[:, FFN_CHUNK * c:FFN_CHUNK * (c + 1)].astype(BF16))
        up = _dot(h_scr[...], wgu_ref[:, FFN_DIM + FFN_CHUNK * c:FFN_DIM + FFN_CHUNK * (c + 1)].astype(BF16))
        a = (gt * _sigmoid(gt) * up).astype(BF16)
        d = _dot(a, wd_ref[FFN_CHUNK * c:FFN_CHUNK * (c + 1), :].astype(BF16))
        if c == 0:
            acc_scr[...] = d
        else:
            acc_scr[...] += d
    y = x_scr[...] + acc_scr[...]
    if final:
        y = y * lax.rsqrt(jnp.mean(y * y, axis=-1, keepdims=True) + EPS) * fg_ref[...]
    o_ref[...] = y


def _resident(shape):
    nd = len(shape)
    return pl.BlockSpec(shape, lambda *_: (0,) * nd, pipeline_mode=pl.Buffered(1))


def _layer_slice(layer, shape):
    nd = len(shape)
    return pl.BlockSpec((None,) + tuple(shape[1:]), lambda *_: (layer,) + (0,) * (nd - 1), pipeline_mode=pl.Buffered(1))


def _mixer_call(layer, x, ng, win, hlb, hgn, rpbp, cw, cb, lng, lnb):
    batch = x.shape[0]
    stacked = (ng, win, None, hgn, rpbp, cw, cb, lng, lnb)
    return pl.pallas_call(
        functools.partial(_mixer_kernel, layer),
        grid=(batch, SEQ // X_TILE),
        in_specs=[pl.BlockSpec((None, X_TILE, D_MODEL), lambda b, t: (b, t, 0))]
        + [_resident(hlb.shape) if a is None else _layer_slice(layer, a.shape) for a in stacked],
        out_specs=pl.BlockSpec((None, SEQ, D_MODEL), lambda b, t: (b, 0, 0)),
        out_shape=jax.ShapeDtypeStruct(x.shape, BF16),
        scratch_shapes=[
            pltpu.VMEM((SEQ, D_MODEL), BF16),
            pltpu.VMEM((2, SEQ, P_COLS), F32),
            pltpu.VMEM((SEQ, HG_HEAD_DIM), F32),
            pltpu.VMEM((2, SEQ, HG_HEAD_DIM), BF16),
            pltpu.VMEM((2, HG_NCHUNK, HG_HEAD_DIM, HG_HEAD_DIM), F32),
            pltpu.VMEM((2, HG_NCHUNK, HG_HEAD_DIM, HG_HEAD_DIM), BF16),
            pltpu.VMEM((2, HG_NCHUNK, SUBLANES, LANES), F32),
            pltpu.VMEM((SEQ + 2 * CONV_PAD, CONV_DIM), F32),
            pltpu.VMEM((NA_HEADS // 2, NA_KH, 2 * GRID_W, NA_KEYS), F32),
            pltpu.VMEM((2, D_MODEL, HG_COLS), BF16),
            pltpu.VMEM((NA_ROWS_PER_STEP * NA_HEADS // 2, 2 * GRID_W, NA_KEYS), F32),
            pltpu.VMEM((2, 3, HG_TILE, HG_HEAD_DIM), BF16),
        ],
        compiler_params=pltpu.CompilerParams(dimension_semantics=("arbitrary", "arbitrary"),
                                             vmem_limit_bytes=VMEM_LIMIT_BYTES),
        name=f"mixer_l{layer}",
    )(x, ng, win, hlb, hgn, rpbp, cw, cb, lng, lnb)


def _ffn_call(layer, final, x2d, y2d, wout, ng, wgu, wd, fg):
    n_tok = x2d.shape[0]
    tile = pl.BlockSpec((FFN_TILE, D_MODEL), lambda i: (i, 0))
    return pl.pallas_call(
        functools.partial(_ffn_kernel, final),
        grid=(n_tok // FFN_TILE,),
        in_specs=[tile, tile] + [_layer_slice(layer, a.shape) for a in (wout, ng, wgu, wd)] + [_resident(fg.shape)],
        out_specs=tile,
        out_shape=jax.ShapeDtypeStruct(x2d.shape, x2d.dtype),
        scratch_shapes=[pltpu.VMEM((FFN_TILE, D_MODEL), BF16), pltpu.VMEM((FFN_TILE, D_MODEL), F32),
                        pltpu.VMEM((FFN_TILE, D_MODEL), F32)],
        compiler_params=pltpu.CompilerParams(dimension_semantics=("arbitrary",), vmem_limit_bytes=VMEM_LIMIT_BYTES),
        name=f"ffn_l{layer}",
    )(x2d, y2d, wout, ng, wgu, wd, fg)


def _pad_rpb(rpb):
    v = jnp.pad(rpb, ((0, 0), (0, 0), (0, RPB_ROWS - NA_NREL_H), (RPB_LEAD, LANES - RPB_LEAD - NA_NREL_W)))
    return v.reshape(DEPTH, NA_HEADS * RPB_ROWS, LANES)


def kernel(x, mix_norm_g, w_in, hg_lower_bounds, hg_norm_g, na_rpb, conv_w, conv_b, conv_ln_g, conv_ln_b, w_out,
           ffn_norm_g, w_gate_up, w_down, final_norm_g):
    batch, seq, d = x.shape
    assert (seq, d) == (SEQ, D_MODEL) and w_in.shape == (DEPTH, D_MODEL, D_IN)
    row = lambda p: p.astype(F32).reshape(DEPTH, 1, p.shape[-1])
    win = w_in.astype(BF16)
    wout, wgu, wd = (w.astype(F32) for w in (w_out, w_gate_up, w_down))
    hlb = hg_lower_bounds.astype(F32)
    rpbp = _pad_rpb(na_rpb.astype(F32))
    mix_g, hgn, cb, lng, lnb, ffn_g = (row(p) for p in (mix_norm_g, hg_norm_g, conv_b, conv_ln_g, conv_ln_b, ffn_norm_g))
    fg = final_norm_g.astype(F32).reshape(1, D_MODEL)
    for l in range(DEPTH):
        y = _mixer_call(l, x, mix_g, win, hlb, hgn, rpbp, conv_w.astype(F32), cb, lng, lnb)
        x2d = _ffn_call(l, l == DEPTH - 1, x.reshape(batch * seq, d), y.reshape(batch * seq, d), wout, ffn_g, wgu, wd, fg)
        x = x2d.reshape(batch, seq, d)
    return x
```

```python
import functools

import numpy as np
import jax
import jax.numpy as jnp
from jax import lax
from jax.experimental import pallas as pl
from jax.experimental.pallas import tpu as pltpu

D_MODEL = 1024
SEQ = 2048
DEPTH = 2
GRID_W = 64
ROWS = SEQ // GRID_W
HG_DIM = 512
HG_HEADS = 4
HG_HEAD_DIM = 128
NA_DIM = 256
NA_HEADS = 4
NA_HEAD_DIM = 64
NA_KH = 8
NA_KW = 16
CONV_DIM = 256
CONV_WIDTH = 31
D_IN = 5 * HG_DIM + 3 * NA_DIM + 2 * CONV_DIM
FFN_DIM = 2816
EPS = 1e-6
NEG_INF = -1e30

LANES = 128
SUBLANES = 8
VMEM_LIMIT_BYTES = 60 * 1024 * 1024

ROW_TILE = 1024
HG_CHUNK = 64
HG_CHUNK_LOG2 = HG_CHUNK.bit_length() - 1
assert HG_CHUNK == 1 << HG_CHUNK_LOG2
HG_NCHUNK = SEQ // HG_CHUNK
HG_TILE = 256
HG_OUT_TILE = 512
HG_LOCAL_UNROLL = 1
X_TILE = 512
HG_COLS = 5 * HG_HEAD_DIM
NA_COL0 = 5 * HG_DIM
CV_COL0 = NA_COL0 + 3 * NA_DIM
NA_KEYS = NA_KH * GRID_W
NA_NREL_H = 2 * NA_KH - 1
NA_NREL_W = 2 * NA_KW - 1
RPB_LEAD = GRID_W - NA_KW
RPB_ROWS = 16
assert RPB_ROWS >= NA_NREL_H and RPB_ROWS % SUBLANES == 0
NA_ROWS_PER_STEP = 2
CONV_TILE = 64
CONV_OUTER_TILE = 256
CONV_PAD = 16
P_COLS = HG_COLS
assert P_COLS >= 2 * CONV_DIM and P_COLS >= NA_DIM
FFN_TILE = 512
FFN_CHUNK = 256
FFN_NCHUNK = FFN_DIM // FFN_CHUNK

F32 = jnp.float32
BF16 = jnp.bfloat16


def _sigmoid(x):
    return 1.0 / (1.0 + jnp.exp(-x))


def _dot(a, b):
    return jnp.dot(a, b, preferred_element_type=F32)


def _dot_nt(a, b):
    return lax.dot_general(a, b, (((1,), (1,)), ((), ())), preferred_element_type=F32)


def _dot_tn(a, b):
    return lax.dot_general(a, b, (((0,), (0,)), ((), ())), preferred_element_type=F32)


def _split3(x):
    hi = x.astype(BF16)
    r1 = x - hi.astype(F32)
    mid = r1.astype(BF16)
    lo = (r1 - mid.astype(F32)).astype(BF16)
    return hi, mid, lo


def _build_na_bias(rpbp_ref, bias_ref):
    q_i = lax.broadcasted_iota(jnp.int32, (GRID_W, LANES), 0)
    l_i = lax.broadcasted_iota(jnp.int32, (GRID_W, LANES), 1)
    kc = jnp.bitwise_and(l_i, GRID_W - 1)
    q_start = jnp.clip(q_i - NA_KW // 2, 0, GRID_W - NA_KW)
    visible = (kc >= q_start) & (kc < q_start + NA_KW)
    left = l_i < GRID_W
    q_all = jnp.bitwise_and(lax.broadcasted_iota(jnp.int32, (RPB_ROWS * GRID_W, LANES), 0), GRID_W - 1)

    def head_body(h, c):
        rows = rpbp_ref[pl.ds(pl.multiple_of(h * RPB_ROWS, RPB_ROWS), RPB_ROWS), :]
        x = jnp.concatenate([jnp.broadcast_to(rows[ro:ro + 1, :], (GRID_W, LANES)) for ro in range(RPB_ROWS)], axis=0)
        even = pltpu.roll(x, LANES - (GRID_W - 1), axis=1)
        for b in range(GRID_W.bit_length() - 1):
            even = jnp.where(jnp.bitwise_and(q_all, 1 << b) != 0, pltpu.roll(even, 1 << b, axis=1), even)
        odd = pltpu.roll(even, GRID_W, axis=1)
        hp = h // 2
        q0 = pl.multiple_of((h % 2) * GRID_W, GRID_W)
        for var in range(NA_KH):
            for kp in range(NA_KH // 2):
                ro = 2 * kp - var + NA_KH - 1
                tile = jnp.where(left, even[ro * GRID_W:(ro + 1) * GRID_W], odd[(ro + 1) * GRID_W:(ro + 2) * GRID_W])
                bias_ref[hp, var, pl.ds(q0, GRID_W), kp * LANES:(kp + 1) * LANES] = jnp.where(visible, tile, NEG_INF)
        return c

    lax.fori_loop(0, NA_HEADS, head_body, 0)


def _mixer_kernel(layer, x_ref, ng_ref, win_ref, hlb_ref, hgn_ref, rpbp_ref, cw_ref, cb_ref, lng_ref, lnb_ref,
                  y_ref, h_scr, p_scr, oi_scr, qa_scr, d_scr, st_scr, dec_scr, upad_scr, bias_ref,
                  wh_scr, qk_scr, stage_scr):
    t = pl.program_id(1)

    @pl.when((pl.program_id(0) == 0) & (t == 0))
    def _():
        _build_na_bias(rpbp_ref, bias_ref)

    xt = x_ref[...]
    ms = jnp.mean(xt * xt, axis=-1, keepdims=True)
    h_scr[pl.ds(pl.multiple_of(t * X_TILE, X_TILE), X_TILE), :] = (xt * lax.rsqrt(ms + EPS) * ng_ref[...]).astype(BF16)

    @pl.when(t == SEQ // X_TILE - 1)
    def _():
        _mix_sequence(layer, win_ref, hlb_ref, hgn_ref, cw_ref, cb_ref, lng_ref, lnb_ref, y_ref, h_scr, p_scr, oi_scr,
                      qa_scr, d_scr, st_scr, dec_scr, upad_scr, bias_ref, wh_scr, qk_scr, stage_scr)


def _mix_sequence(layer, win_ref, hlb_ref, hgn_ref, cw_ref, cb_ref, lng_ref, lnb_ref, y_scr, h_scr, p_scr, oi_scr,
                  qa_scr, d_scr, st_scr, dec_scr, upad_scr, bias_ref, wh_scr, qk_scr, stage_scr):
    n_row_tiles = SEQ // ROW_TILE

    p_bufs = (p_scr.at[0], p_scr.at[1])
    wh_bufs = (wh_scr.at[0], wh_scr.at[1])
    p_conv = p_bufs[HG_HEADS % 2]
    p_na = p_bufs[1 - HG_HEADS % 2]

    def project(dst, w_ref, col0, ncols):
        def body(i, c):
            r = pl.multiple_of(i * ROW_TILE, ROW_TILE)
            dst[pl.ds(r, ROW_TILE), 0:ncols] = _dot(h_scr[pl.ds(r, ROW_TILE), :], w_ref[:, col0:col0 + ncols])
            return c
        lax.fori_loop(0, n_row_tiles, body, 0)

    def load_head_weights(hd):
        for grp in range(HG_COLS // HG_HEAD_DIM):
            col = grp * HG_DIM + hd * HG_HEAD_DIM
            wh_bufs[hd % 2][:, grp * HG_HEAD_DIM:(grp + 1) * HG_HEAD_DIM] = win_ref[:, col:col + HG_HEAD_DIM]

    hlb = [hlb_ref[d] for d in range(DEPTH)]
    hmax = functools.reduce(jnp.maximum, hlb)
    hexp = [jnp.exp(v - hmax) for v in hlb]
    hsum = functools.reduce(lambda a, b: a + b, hexp)
    hsm = [e / hsum for e in hexp]
    lbs = functools.reduce(lambda a, b: a + b, hsm[:layer + 1]) - hsm[0]

    L = HG_CHUNK
    TL = HG_TILE
    cpt = TL // L
    row_c = lax.broadcasted_iota(jnp.int32, (TL, TL), 0)
    col_c = lax.broadcasted_iota(jnp.int32, (TL, TL), 1)
    same_chunk = jnp.right_shift(row_c, HG_CHUNK_LOG2) == jnp.right_shift(col_c, HG_CHUNK_LOG2)
    lower = same_chunk & (row_c >= col_c)
    upper = same_chunk & (col_c >= row_c)
    lower_b = jnp.where(lower, 1.0, 0.0).astype(BF16)
    upper_b = jnp.where(upper, 1.0, 0.0).astype(BF16)

    def chunk_rows(a, row):
        return [a[c * L + row:c * L + row + 1, :] for c in range(cpt)]

    def spread(rows):
        return jnp.concatenate([jnp.broadcast_to(rw, (L, LANES)) for rw in rows], axis=0)

    load_head_weights(0)
    project(p_bufs[0], wh_bufs[0], 0, HG_COLS)
    for hd in range(HG_HEADS):
        sl = slice(hd * HG_HEAD_DIM, (hd + 1) * HG_HEAD_DIM)
        pc = p_bufs[hd % 2]
        if hd + 1 < HG_HEADS:
            load_head_weights(hd + 1)
            nxt_w, nxt_col0, nxt_cols = wh_bufs[(hd + 1) % 2], 0, HG_COLS
        else:
            nxt_w, nxt_col0, nxt_cols = win_ref, CV_COL0, 2 * CONV_DIM
        directions = (
            (HG_HEAD_DIM, lbs[0:1, sl], lower, lower_b, L // 2 - 1, L - 1),
            (2 * HG_HEAD_DIM, lbs[1:2, sl], upper, upper_b, L // 2, 0),
        )

        def tile_start(i):
            return i * TL if isinstance(i, int) else pl.multiple_of(i * TL, TL)

        def hg_scale(i):
            r = tile_start(i)
            q_raw = pc[pl.ds(r, TL), 0:HG_HEAD_DIM]
            q = q_raw * _sigmoid(q_raw)
            gates = []
            for zcol, lb, mask, cum_op, ref_row, last_row in directions:
                z = pc[pl.ds(r, TL), zcol:zcol + HG_HEAD_DIM]
                t = jnp.exp(-jnp.abs(z))
                big = 1.0 / (1.0 + t)
                small = t * big
                nonneg = z >= 0.0
                f = lb + (1.0 - lb) * jnp.where(nonneg, big, small)
                k = (1.0 - lb) * jnp.where(nonneg, small, big)
                gates.append((k, _dot(cum_op, jnp.concatenate(_split3(jnp.log(f)), axis=1))))
            for d, ((zcol, lb, mask, cum_op, ref_row, last_row), (k, c3)) in enumerate(zip(directions, gates)):
                cum = c3[:, 0:LANES] + c3[:, LANES:2 * LANES] + c3[:, 2 * LANES:3 * LANES]
                ref_rows = chunk_rows(cum, ref_row)
                last_rows = chunk_rows(cum, last_row)
                rel = cum - spread(ref_rows)
                q_in = q * jnp.exp(rel)
                k_in = k * jnp.exp(-rel)
                qa_scr[d, pl.ds(r, TL), :] = (q_in * spread([jnp.exp(rw) for rw in ref_rows])).astype(BF16)
                k_d = (k_in * spread([jnp.exp(lr - rr) for lr, rr in zip(last_rows, ref_rows)])).astype(BF16)
                for cc in range(cpt):
                    dec_scr[d, i * cpt + cc] = jnp.broadcast_to(jnp.exp(last_rows[cc]), (SUBLANES, LANES))
                stage_scr[d, 0] = q_in.astype(BF16)
                stage_scr[d, 1] = k_in.astype(BF16)
                stage_scr[d, 2] = k_d

        def hg_contract(i):
            r = tile_start(i)
            vb = pc[pl.ds(r, TL), 3 * HG_HEAD_DIM:4 * HG_HEAD_DIM].astype(BF16)
            qk = [_dot_nt(stage_scr[d, 0], stage_scr[d, 1]) for d in range(len(directions))]
            masked = [jnp.where(mask, s, 0.0) for (zcol, lb, mask, cum_op, ref_row, last_row), s in zip(directions, qk)]
            oi_scr[pl.ds(r, TL), :] = _dot(functools.reduce(lambda a, b: a + b, masked).astype(BF16), vb)
            k_d = jnp.concatenate([stage_scr[d, 2] for d in range(len(directions))], axis=1)
            for cc in range(cpt):
                inc = _dot_tn(vb[cc * L:(cc + 1) * L], k_d[cc * L:(cc + 1) * L])
                for d in range(len(directions)):
                    d_scr[d, i * cpt + cc] = inc[:, d * HG_HEAD_DIM:(d + 1) * HG_HEAD_DIM]

        def project_next(i, pn=p_bufs[(hd + 1) % 2], w=nxt_w, col0=nxt_col0, ncols=nxt_cols):
            r = tile_start(i)
            pn[pl.ds(r, TL), 0:ncols] = _dot(h_scr[pl.ds(r, TL), :], w[:, col0:col0 + ncols])

        def hg_local_body(i, c):
            hg_contract(i)
            hg_scale(i + 1)
            project_next(i + 1)
            return c

        hg_scale(0)
        project_next(0)
        lax.fori_loop(0, SEQ // TL - 1, hg_local_body, 0, unroll=HG_LOCAL_UNROLL)
        hg_contract(SEQ // TL - 1)

        def hg_scan_body(n, carry):
            s_f, s_b = carry
            m = HG_NCHUNK - 1 - n
            st_scr[0, n] = s_f.astype(BF16)
            st_scr[1, m] = s_b.astype(BF16)
            s_f = s_f * dec_scr[0, n][0:1, :] + d_scr[0, n]
            s_b = s_b * dec_scr[1, m][0:1, :] + d_scr[1, m]
            return s_f, s_b

        s_zero = jnp.zeros((HG_HEAD_DIM, HG_HEAD_DIM), F32)
        lax.fori_loop(0, HG_NCHUNK, hg_scan_body, (s_zero, s_zero))

        def hg_out_body(i, c):
            r = pl.multiple_of(i * HG_OUT_TILE, HG_OUT_TILE)
            inter = []
            for cc in range(HG_OUT_TILE // L):
                n = i * (HG_OUT_TILE // L) + cc
                rc = pl.multiple_of(r + cc * L, L)
                qa = jnp.concatenate([qa_scr[0, pl.ds(rc, L), :], qa_scr[1, pl.ds(rc, L), :]], axis=1)
                st = jnp.concatenate([st_scr[0, n], st_scr[1, n]], axis=1)
                inter.append(_dot_nt(qa, st))
            o = oi_scr[pl.ds(r, HG_OUT_TILE), :] + jnp.concatenate(inter, axis=0)
            o = o * lax.rsqrt(jnp.mean(o * o, axis=-1, keepdims=True) + EPS) * hgn_ref[...]
            gate = pc[pl.ds(r, HG_OUT_TILE), 4 * HG_HEAD_DIM:5 * HG_HEAD_DIM]
            y_scr[pl.ds(r, HG_OUT_TILE), sl] = (o * (gate * _sigmoid(gate))).astype(BF16)
            return c

        lax.fori_loop(0, SEQ // HG_OUT_TILE, hg_out_body, 0, unroll=2)

    upad_scr[0:CONV_PAD, :] = jnp.zeros((CONV_PAD, CONV_DIM), F32)
    upad_scr[CONV_PAD + SEQ:CONV_PAD + SEQ + CONV_PAD, :] = jnp.zeros((CONV_PAD, CONV_DIM), F32)

    def glu_body(i, c):
        r = pl.multiple_of(i * ROW_TILE, ROW_TILE)
        a = p_conv[pl.ds(r, ROW_TILE), 0:CONV_DIM]
        gate = p_conv[pl.ds(r, ROW_TILE), CONV_DIM:2 * CONV_DIM]
        upad_scr[pl.ds(CONV_PAD + r, ROW_TILE), :] = a * _sigmoid(gate)
        return c

    lax.fori_loop(0, n_row_tiles, glu_body, 0)

    scale = NA_HEAD_DIM ** -0.5

    def na_project_tile(r, rows):
        res = _dot(h_scr[pl.ds(r, rows), :], win_ref[:, NA_COL0:NA_COL0 + 3 * NA_DIM])
        p_na[pl.ds(r, rows), 0:NA_DIM] = res[:, 0:NA_DIM] * scale
        h_scr[pl.ds(r, rows), 0:2 * NA_DIM] = res[:, NA_DIM:3 * NA_DIM].astype(BF16)

    def conv_tile(r):
        off = CONV_PAD - CONV_WIDTH // 2
        groups = -(-CONV_WIDTH // SUBLANES)
        win_rows = CONV_TILE + 2 * CONV_PAD
        for c0 in range(0, CONV_DIM, LANES):
            win = upad_scr[pl.ds(r, win_rows), c0:c0 + LANES]
            acc = jnp.zeros((CONV_TILE, LANES), F32) + cb_ref[:, c0:c0 + LANES]
            shifted = win
            for s in range(SUBLANES):
                step = off if s == 0 else 1
                if step:
                    shifted = pltpu.roll(shifted, win_rows - step, axis=0)
                for a in range(groups):
                    j = SUBLANES * a + s
                    if j < CONV_WIDTH:
                        acc = acc + shifted[SUBLANES * a:SUBLANES * a + CONV_TILE, :] * cw_ref[j:j + 1, c0:c0 + LANES]
            p_conv[pl.ds(r, CONV_TILE), c0:c0 + LANES] = acc

    def conv_body(i, c):
        r = pl.multiple_of(i * CONV_OUTER_TILE, CONV_OUTER_TILE)
        na_project_tile(r, CONV_OUTER_TILE)
        for sub in range(CONV_OUTER_TILE // CONV_TILE):
            conv_tile(pl.multiple_of(r + sub * CONV_TILE, CONV_TILE))
        return c

    lax.fori_loop(0, SEQ // CONV_OUTER_TILE, conv_body, 0)

    def conv_norm_body(i, c):
        r = pl.multiple_of(i * ROW_TILE, ROW_TILE)
        acc = p_conv[pl.ds(r, ROW_TILE), 0:CONV_DIM]
        mu = jnp.mean(acc, axis=-1, keepdims=True)
        d = acc - mu
        var = jnp.mean(d * d, axis=-1, keepdims=True)
        u = d * lax.rsqrt(var + EPS) * lng_ref[...] + lnb_ref[...]
        y_scr[pl.ds(r, ROW_TILE), HG_DIM + NA_DIM:D_MODEL] = (u * _sigmoid(u)).astype(BF16)
        return c

    lax.fori_loop(0, n_row_tiles, conv_norm_body, 0)

    lane = lax.broadcasted_iota(jnp.int32, (1, LANES), 1)
    first_half = lane < NA_HEAD_DIM

    n_na_steps = ROWS // NA_ROWS_PER_STEP

    def na_units(step):
        units = []
        for dr in range(NA_ROWS_PER_STEP):
            r = step * NA_ROWS_PER_STEP + dr
            if isinstance(r, int):
                r0 = min(max(r - NA_KH // 2, 0), ROWS - NA_KH)
                qrow, krow = r * GRID_W, r0 * GRID_W
            else:
                r0 = jnp.clip(r - NA_KH // 2, 0, ROWS - NA_KH)
                qrow, krow = pl.multiple_of(r * GRID_W, GRID_W), pl.multiple_of(r0 * GRID_W, GRID_W)
            for hp in range(NA_HEADS // 2):
                units.append((qrow, krow, r - r0, hp))
        return units

    def na_scores(units):
        for u, (qrow, krow, var, hp) in enumerate(units):
            c0 = hp * LANES
            qt = p_na[pl.ds(qrow, GRID_W), c0:c0 + LANES]
            qm = jnp.concatenate([jnp.where(first_half, qt, 0.0), jnp.where(first_half, 0.0, qt)], axis=0).astype(BF16)
            qk_scr[u] = _dot_nt(qm, h_scr[pl.ds(krow, NA_KEYS), c0:c0 + LANES])

    na_scores(na_units(0))

    def na_body(i, c):
        units = na_units(i)
        probs = []
        for u, (qrow, krow, var, hp) in enumerate(units):
            bt = bias_ref[hp, var]
            s = jnp.where(bt > 0.5 * NEG_INF, qk_scr[u] + bt, NEG_INF)
            probs.append(jnp.exp(s - jnp.max(s, axis=-1, keepdims=True)).astype(BF16))
        na_scores(na_units(jnp.minimum(i + 1, n_na_steps - 1)))
        ones = jnp.ones((NA_KEYS, LANES), BF16)
        for (qrow, krow, var, hp), e in zip(units, probs):
            c0 = hp * LANES
            v_ones = jnp.concatenate([h_scr[pl.ds(krow, NA_KEYS), NA_DIM + c0:NA_DIM + c0 + LANES], ones], axis=1)
            pv = _dot(e, v_ones)
            o = pv[:, 0:LANES] / pv[:, LANES:2 * LANES]
            y_scr[pl.ds(qrow, GRID_W), HG_DIM + c0:HG_DIM + c0 + LANES] = jnp.where(
                first_half, o[0:GRID_W], o[GRID_W:2 * GRID_W]).astype(BF16)
        return c

    lax.fori_loop(0, n_na_steps, na_body, 0)


def _ffn_kernel(final, x_ref, y_ref, wout_ref, ng_ref, wgu_ref, wd_ref, fg_ref, o_ref, h_scr, acc_scr, x_scr):
    x = x_ref[...] + _dot(y_ref[...], wout_ref[...].astype(BF16))
    x_scr[...] = x
    ms = jnp.mean(x * x, axis=-1, keepdims=True)
    h_scr[...] = (x * lax.rsqrt(ms + EPS) * ng_ref[...]).astype(BF16)
    for c in range(FFN_NCHUNK):
        gt = _dot(h_scr[...], wgu_ref[:, FFN_CHUNK * c:FFN_CHUNK * (c + 1)].astype(BF16))
        up = _dot(h_scr[...], wgu_ref[:, FFN_DIM + FFN_CHUNK * c:FFN_DIM + FFN_CHUNK * (c + 1)].astype(BF16))
        a = (gt * _sigmoid(gt) * up).astype(BF16)
        d = _dot(a, wd_ref[FFN_CHUNK * c:FFN_CHUNK * (c + 1), :].astype(BF16))
        if c == 0:
            acc_scr[...] = d
        else:
            acc_scr[...] += d
    y = x_scr[...] + acc_scr[...]
    if final:
        y = y * lax.rsqrt(jnp.mean(y * y, axis=-1, keepdims=True) + EPS) * fg_ref[...]
    o_ref[...] = y


def _resident(shape):
    nd = len(shape)
    return pl.BlockSpec(shape, lambda *_: (0,) * nd, pipeline_mode=pl.Buffered(1))


def _layer_slice(layer, shape):
    nd = len(shape)
    return pl.BlockSpec((None,) + tuple(shape[1:]), lambda *_: (layer,) + (0,) * (nd - 1), pipeline_mode=pl.Buffered(1))


def _mixer_call(layer, x, ng, win, hlb, hgn, rpbp, cw, cb, lng, lnb):
    batch = x.shape[0]
    stacked = (ng, win, None, hgn, rpbp, cw, cb, lng, lnb)
    return pl.pallas_call(
        functools.partial(_mixer_kernel, layer),
        grid=(batch, SEQ // X_TILE),
        in_specs=[pl.BlockSpec((None, X_TILE, D_MODEL), lambda b, t: (b, t, 0))]
        + [_resident(hlb.shape) if a is None else _layer_slice(layer, a.shape) for a in stacked],
        out_specs=pl.BlockSpec((None, SEQ, D_MODEL), lambda b, t: (b, 0, 0)),
        out_shape=jax.ShapeDtypeStruct(x.shape, BF16),
        scratch_shapes=[
            pltpu.VMEM((SEQ, D_MODEL), BF16),
            pltpu.VMEM((2, SEQ, P_COLS), F32),
            pltpu.VMEM((SEQ, HG_HEAD_DIM), F32),
            pltpu.VMEM((2, SEQ, HG_HEAD_DIM), BF16),
            pltpu.VMEM((2, HG_NCHUNK, HG_HEAD_DIM, HG_HEAD_DIM), F32),
            pltpu.VMEM((2, HG_NCHUNK, HG_HEAD_DIM, HG_HEAD_DIM), BF16),
            pltpu.VMEM((2, HG_NCHUNK, SUBLANES, LANES), F32),
            pltpu.VMEM((SEQ + 2 * CONV_PAD, CONV_DIM), F32),
            pltpu.VMEM((NA_HEADS // 2, NA_KH, 2 * GRID_W, NA_KEYS), F32),
            pltpu.VMEM((2, D_MODEL, HG_COLS), BF16),
            pltpu.VMEM((NA_ROWS_PER_STEP * NA_HEADS // 2, 2 * GRID_W, NA_KEYS), F32),
            pltpu.VMEM((2, 3, HG_TILE, HG_HEAD_DIM), BF16),
        ],
        compiler_params=pltpu.CompilerParams(dimension_semantics=("arbitrary", "arbitrary"),
                                             vmem_limit_bytes=VMEM_LIMIT_BYTES),
        name=f"mixer_l{layer}",
    )(x, ng, win, hlb, hgn, rpbp, cw, cb, lng, lnb)


def _ffn_call(layer, final, x2d, y2d, wout, ng, wgu, wd, fg):
    n_tok = x2d.shape[0]
    tile = pl.BlockSpec((FFN_TILE, D_MODEL), lambda i: (i, 0))
    return pl.pallas_call(
        functools.partial(_ffn_kernel, final),
        grid=(n_tok // FFN_TILE,),
        in_specs=[tile, tile] + [_layer_slice(layer, a.shape) for a in (wout, ng, wgu, wd)] + [_resident(fg.shape)],
        out_specs=tile,
        out_shape=jax.ShapeDtypeStruct(x2d.shape, x2d.dtype),
        scratch_shapes=[pltpu.VMEM((FFN_TILE, D_MODEL), BF16), pltpu.VMEM((FFN_TILE, D_MODEL), F32),
                        pltpu.VMEM((FFN_TILE, D_MODEL), F32)],
        compiler_params=pltpu.CompilerParams(dimension_semantics=("arbitrary",), vmem_limit_bytes=VMEM_LIMIT_BYTES),
        name=f"ffn_l{layer}",
    )(x2d, y2d, wout, ng, wgu, wd, fg)


def _pad_rpb(rpb):
    v = jnp.pad(rpb, ((0, 0), (0, 0), (0, RPB_ROWS - NA_NREL_H), (RPB_LEAD, LANES - RPB_LEAD - NA_NREL_W)))
    return v.reshape(DEPTH, NA_HEADS * RPB_ROWS, LANES)


def kernel(x, mix_norm_g, w_in, hg_lower_bounds, hg_norm_g, na_rpb, conv_w, conv_b, conv_ln_g, conv_ln_b, w_out,
           ffn_norm_g, w_gate_up, w_down, final_norm_g):
    batch, seq, d = x.shape
    assert (seq, d) == (SEQ, D_MODEL) and w_in.shape == (DEPTH, D_MODEL, D_IN)
    row = lambda p: p.astype(F32).reshape(DEPTH, 1, p.shape[-1])
    win = w_in.astype(BF16)
    wout, wgu, wd = (w.astype(F32) for w in (w_out, w_gate_up, w_down))
    hlb = hg_lower_bounds.astype(F32)
    rpbp = _pad_rpb(na_rpb.astype(F32))
    mix_g, hgn, cb, lng, lnb, ffn_g = (row(p) for p in (mix_norm_g, hg_norm_g, conv_b, conv_ln_g, conv_ln_b, ffn_norm_g))
    fg = final_norm_g.astype(F32).reshape(1, D_MODEL)
    for l in range(DEPTH):
        y = _mixer_call(l, x, mix_g, win, hlb, hgn, rpbp, conv_w.astype(F32), cb, lng, lnb)
        x2d = _ffn_call(l, l == DEPTH - 1, x.reshape(batch * seq, d), y.reshape(batch * seq, d), wout, ffn_g, wgu, wd, fg)
        x = x2d.reshape(batch, seq, d)
    return x
```

```python
import functools

import numpy as np
import jax
import jax.numpy as jnp
from jax import lax
from jax.experimental import pallas as pl
from jax.experimental.pallas import tpu as pltpu

D_MODEL = 1024
SEQ = 2048
DEPTH = 2
GRID_W = 64
ROWS = SEQ // GRID_W
HG_DIM = 512
HG_HEADS = 4
HG_HEAD_DIM = 128
NA_DIM = 256
NA_HEADS = 4
NA_HEAD_DIM = 64
NA_KH = 8
NA_KW = 16
CONV_DIM = 256
CONV_WIDTH = 31
D_IN = 5 * HG_DIM + 3 * NA_DIM + 2 * CONV_DIM
FFN_DIM = 2816
EPS = 1e-6
NEG_INF = -1e30

LANES = 128
SUBLANES = 8
VMEM_LIMIT_BYTES = 60 * 1024 * 1024

ROW_TILE = 1024
HG_CHUNK = 64
HG_CHUNK_LOG2 = HG_CHUNK.bit_length() - 1
assert HG_CHUNK == 1 << HG_CHUNK_LOG2
HG_NCHUNK = SEQ // HG_CHUNK
HG_TILE = 256
HG_OUT_TILE = 512
HG_LOCAL_UNROLL = 1
X_TILE = 512
HG_COLS = 5 * HG_HEAD_DIM
NA_COL0 = 5 * HG_DIM
CV_COL0 = NA_COL0 + 3 * NA_DIM
NA_KEYS = NA_KH * GRID_W
NA_NREL_H = 2 * NA_KH - 1
NA_NREL_W = 2 * NA_KW - 1
RPB_LEAD = GRID_W - NA_KW
RPB_ROWS = 16
assert RPB_ROWS >= NA_NREL_H and RPB_ROWS % SUBLANES == 0
NA_ROWS_PER_STEP = 2
CONV_TILE = 64
CONV_OUTER_TILE = 256
CONV_PAD = 16
P_COLS = HG_COLS
assert P_COLS >= 2 * CONV_DIM and P_COLS >= NA_DIM
FFN_TILE = 512
FFN_CHUNK = 256
FFN_NCHUNK = FFN_DIM // FFN_CHUNK

F32 = jnp.float32
BF16 = jnp.bfloat16


def _sigmoid(x):
    return 1.0 / (1.0 + jnp.exp(-x))


def _dot(a, b):
    return jnp.dot(a, b, preferred_element_type=F32)


def _dot_nt(a, b):
    return lax.dot_general(a, b, (((1,), (1,)), ((), ())), preferred_element_type=F32)


def _dot_tn(a, b):
    return lax.dot_general(a, b, (((0,), (0,)), ((), ())), preferred_element_type=F32)


def _split3(x):
    hi = x.astype(BF16)
    r1 = x - hi.astype(F32)
    mid = r1.astype(BF16)
    lo = (r1 - mid.astype(F32)).astype(BF16)
    return hi, mid, lo


def _build_na_bias(rpbp_ref, bias_ref):
    q_i = lax.broadcasted_iota(jnp.int32, (GRID_W, LANES), 0)
    l_i = lax.broadcasted_iota(jnp.int32, (GRID_W, LANES), 1)
    kc = jnp.bitwise_and(l_i, GRID_W - 1)
    q_start = jnp.clip(q_i - NA_KW // 2, 0, GRID_W - NA_KW)
    visible = (kc >= q_start) & (kc < q_start + NA_KW)
    left = l_i < GRID_W
    q_all = jnp.bitwise_and(lax.broadcasted_iota(jnp.int32, (RPB_ROWS * GRID_W, LANES), 0), GRID_W - 1)

    def head_body(h, c):
        rows = rpbp_ref[pl.ds(pl.multiple_of(h * RPB_ROWS, RPB_ROWS), RPB_ROWS), :]
        x = jnp.concatenate([jnp.broadcast_to(rows[ro:ro + 1, :], (GRID_W, LANES)) for ro in range(RPB_ROWS)], axis=0)
        even = pltpu.roll(x, LANES - (GRID_W - 1), axis=1)
        for b in range(GRID_W.bit_length() - 1):
            even = jnp.where(jnp.bitwise_and(q_all, 1 << b) != 0, pltpu.roll(even, 1 << b, axis=1), even)
        odd = pltpu.roll(even, GRID_W, axis=1)
        hp = h // 2
        q0 = pl.multiple_of((h % 2) * GRID_W, GRID_W)
        for var in range(NA_KH):
            for kp in range(NA_KH // 2):
                ro = 2 * kp - var + NA_KH - 1
                tile = jnp.where(left, even[ro * GRID_W:(ro + 1) * GRID_W], odd[(ro + 1) * GRID_W:(ro + 2) * GRID_W])
                bias_ref[hp, var, pl.ds(q0, GRID_W), kp * LANES:(kp + 1) * LANES] = jnp.where(visible, tile, NEG_INF)
        return c

    lax.fori_loop(0, NA_HEADS, head_body, 0)


def _load_head_weights(win_ref, wh_ref, hd):
    for grp in range(HG_COLS // HG_HEAD_DIM):
        col = grp * HG_DIM + hd * HG_HEAD_DIM
        wh_ref[:, grp * HG_HEAD_DIM:(grp + 1) * HG_HEAD_DIM] = win_ref[:, col:col + HG_HEAD_DIM]


def _mixer_kernel(layer, x_ref, ng_ref, win_ref, hlb_ref, hgn_ref, rpbp_ref, cw_ref, cb_ref, lng_ref, lnb_ref,
                  y_ref, h_scr, p_scr, oi_scr, qa_scr, d_scr, st_scr, dec_scr, upad_scr, bias_ref,
                  wh_scr, qk_scr, stage_scr):
    t = pl.program_id(1)

    @pl.when((pl.program_id(0) == 0) & (t == 0))
    def _():
        _build_na_bias(rpbp_ref, bias_ref)

    @pl.when(t == 0)
    def _():
        _load_head_weights(win_ref, wh_scr.at[0], 0)

    rows = pl.ds(pl.multiple_of(t * X_TILE, X_TILE), X_TILE)
    xt = x_ref[...]
    ms = jnp.mean(xt * xt, axis=-1, keepdims=True)
    ht = (xt * lax.rsqrt(ms + EPS) * ng_ref[...]).astype(BF16)
    h_scr[rows, :] = ht
    p_scr[0, rows, 0:HG_COLS] = _dot(ht, wh_scr[0])

    @pl.when(t == SEQ // X_TILE - 1)
    def _():
        _mix_sequence(layer, win_ref, hlb_ref, hgn_ref, cw_ref, cb_ref, lng_ref, lnb_ref, y_ref, h_scr, p_scr, oi_scr,
                      qa_scr, d_scr, st_scr, dec_scr, upad_scr, bias_ref, wh_scr, qk_scr, stage_scr)


def _mix_sequence(layer, win_ref, hlb_ref, hgn_ref, cw_ref, cb_ref, lng_ref, lnb_ref, y_scr, h_scr, p_scr, oi_scr,
                  qa_scr, d_scr, st_scr, dec_scr, upad_scr, bias_ref, wh_scr, qk_scr, stage_scr):
    n_row_tiles = SEQ // ROW_TILE

    p_bufs = (p_scr.at[0], p_scr.at[1])
    wh_bufs = (wh_scr.at[0], wh_scr.at[1])
    p_conv = p_bufs[HG_HEADS % 2]
    p_na = p_bufs[1 - HG_HEADS % 2]

    def load_head_weights(hd):
        _load_head_weights(win_ref, wh_bufs[hd % 2], hd)

    hlb = [hlb_ref[d] for d in range(DEPTH)]
    hmax = functools.reduce(jnp.maximum, hlb)
    hexp = [jnp.exp(v - hmax) for v in hlb]
    hsum = functools.reduce(lambda a, b: a + b, hexp)
    hsm = [e / hsum for e in hexp]
    lbs = functools.reduce(lambda a, b: a + b, hsm[:layer + 1]) - hsm[0]

    L = HG_CHUNK
    TL = HG_TILE
    cpt = TL // L
    row_c = lax.broadcasted_iota(jnp.int32, (TL, TL), 0)
    col_c = lax.broadcasted_iota(jnp.int32, (TL, TL), 1)
    same_chunk = jnp.right_shift(row_c, HG_CHUNK_LOG2) == jnp.right_shift(col_c, HG_CHUNK_LOG2)
    lower = same_chunk & (row_c >= col_c)
    upper = same_chunk & (col_c >= row_c)
    lower_b = jnp.where(lower, 1.0, 0.0).astype(BF16)
    upper_b = jnp.where(upper, 1.0, 0.0).astype(BF16)

    def chunk_rows(a, row):
        return [a[c * L + row:c * L + row + 1, :] for c in range(cpt)]

    def spread(rows):
        return jnp.concatenate([jnp.broadcast_to(rw, (L, LANES)) for rw in rows], axis=0)

    for hd in range(HG_HEADS):
        sl = slice(hd * HG_HEAD_DIM, (hd + 1) * HG_HEAD_DIM)
        pc = p_bufs[hd % 2]
        if hd + 1 < HG_HEADS:
            load_head_weights(hd + 1)
            nxt_w, nxt_col0, nxt_cols = wh_bufs[(hd + 1) % 2], 0, HG_COLS
        else:
            nxt_w, nxt_col0, nxt_cols = win_ref, CV_COL0, 2 * CONV_DIM
        directions = (
            (HG_HEAD_DIM, lbs[0:1, sl], lower, lower_b, L // 2 - 1, L - 1),
            (2 * HG_HEAD_DIM, lbs[1:2, sl], upper, upper_b, L // 2, 0),
        )

        def tile_start(i):
            return i * TL if isinstance(i, int) else pl.multiple_of(i * TL, TL)

        def hg_scale(i):
            r = tile_start(i)
            q_raw = pc[pl.ds(r, TL), 0:HG_HEAD_DIM]
            q = q_raw * _sigmoid(q_raw)
            gates = []
            for zcol, lb, mask, cum_op, ref_row, last_row in directions:
                z = pc[pl.ds(r, TL), zcol:zcol + HG_HEAD_DIM]
                t = jnp.exp(-jnp.abs(z))
                big = 1.0 / (1.0 + t)
                small = t * big
                nonneg = z >= 0.0
                f = lb + (1.0 - lb) * jnp.where(nonneg, big, small)
                k = (1.0 - lb) * jnp.where(nonneg, small, big)
                gates.append((k, _dot(cum_op, jnp.concatenate(_split3(jnp.log(f)), axis=1))))
            for d, ((zcol, lb, mask, cum_op, ref_row, last_row), (k, c3)) in enumerate(zip(directions, gates)):
                cum = c3[:, 0:LANES] + c3[:, LANES:2 * LANES] + c3[:, 2 * LANES:3 * LANES]
                ref_rows = chunk_rows(cum, ref_row)
                last_rows = chunk_rows(cum, last_row)
                rel = cum - spread(ref_rows)
                q_in = q * jnp.exp(rel)
                k_in = k * jnp.exp(-rel)
                qa_scr[d, pl.ds(r, TL), :] = (q_in * spread([jnp.exp(rw) for rw in ref_rows])).astype(BF16)
                k_d = (k_in * spread([jnp.exp(lr - rr) for lr, rr in zip(last_rows, ref_rows)])).astype(BF16)
                for cc in range(cpt):
                    dec_scr[d, i * cpt + cc] = jnp.broadcast_to(jnp.exp(last_rows[cc]), (SUBLANES, LANES))
                stage_scr[d, 0] = q_in.astype(BF16)
                stage_scr[d, 1] = k_in.astype(BF16)
                stage_scr[d, 2] = k_d

        def hg_contract(i):
            r = tile_start(i)
            vb = pc[pl.ds(r, TL), 3 * HG_HEAD_DIM:4 * HG_HEAD_DIM].astype(BF16)
            qk = [_dot_nt(stage_scr[d, 0], stage_scr[d, 1]) for d in range(len(directions))]
            masked = [jnp.where(mask, s, 0.0) for (zcol, lb, mask, cum_op, ref_row, last_row), s in zip(directions, qk)]
            oi_scr[pl.ds(r, TL), :] = _dot(functools.reduce(lambda a, b: a + b, masked).astype(BF16), vb)
            k_d = jnp.concatenate([stage_scr[d, 2] for d in range(len(directions))], axis=1)
            for cc in range(cpt):
                inc = _dot_tn(vb[cc * L:(cc + 1) * L], k_d[cc * L:(cc + 1) * L])
                for d in range(len(directions)):
                    d_scr[d, i * cpt + cc] = inc[:, d * HG_HEAD_DIM:(d + 1) * HG_HEAD_DIM]

        def project_next(i, pn=p_bufs[(hd + 1) % 2], w=nxt_w, col0=nxt_col0, ncols=nxt_cols):
            r = tile_start(i)
            pn[pl.ds(r, TL), 0:ncols] = _dot(h_scr[pl.ds(r, TL), :], w[:, col0:col0 + ncols])

        def hg_local_body(i, c):
            hg_contract(i)
            hg_scale(i + 1)
            project_next(i + 1)
            return c

        hg_scale(0)
        project_next(0)
        lax.fori_loop(0, SEQ // TL - 1, hg_local_body, 0, unroll=HG_LOCAL_UNROLL)
        hg_contract(SEQ // TL - 1)

        def hg_scan_body(n, carry):
            s_f, s_b = carry
            m = HG_NCHUNK - 1 - n
            st_scr[0, n] = s_f.astype(BF16)
            st_scr[1, m] = s_b.astype(BF16)
            s_f = s_f * dec_scr[0, n][0:1, :] + d_scr[0, n]
            s_b = s_b * dec_scr[1, m][0:1, :] + d_scr[1, m]
            return s_f, s_b

        s_zero = jnp.zeros((HG_HEAD_DIM, HG_HEAD_DIM), F32)
        lax.fori_loop(0, HG_NCHUNK, hg_scan_body, (s_zero, s_zero))

        def hg_out_body(i, c):
            r = pl.multiple_of(i * HG_OUT_TILE, HG_OUT_TILE)
            inter = []
            for cc in range(HG_OUT_TILE // L):
                n = i * (HG_OUT_TILE // L) + cc
                rc = pl.multiple_of(r + cc * L, L)
                qa = jnp.concatenate([qa_scr[0, pl.ds(rc, L), :], qa_scr[1, pl.ds(rc, L), :]], axis=1)
                st = jnp.concatenate([st_scr[0, n], st_scr[1, n]], axis=1)
                inter.append(_dot_nt(qa, st))
            o = oi_scr[pl.ds(r, HG_OUT_TILE), :] + jnp.concatenate(inter, axis=0)
            o = o * lax.rsqrt(jnp.mean(o * o, axis=-1, keepdims=True) + EPS) * hgn_ref[...]
            gate = pc[pl.ds(r, HG_OUT_TILE), 4 * HG_HEAD_DIM:5 * HG_HEAD_DIM]
            y_scr[pl.ds(r, HG_OUT_TILE), sl] = (o * (gate * _sigmoid(gate))).astype(BF16)
            return c

        lax.fori_loop(0, SEQ // HG_OUT_TILE, hg_out_body, 0, unroll=2)

    upad_scr[0:CONV_PAD, :] = jnp.zeros((CONV_PAD, CONV_DIM), F32)
    upad_scr[CONV_PAD + SEQ:CONV_PAD + SEQ + CONV_PAD, :] = jnp.zeros((CONV_PAD, CONV_DIM), F32)

    def glu_body(i, c):
        r = pl.multiple_of(i * ROW_TILE, ROW_TILE)
        a = p_conv[pl.ds(r, ROW_TILE), 0:CONV_DIM]
        gate = p_conv[pl.ds(r, ROW_TILE), CONV_DIM:2 * CONV_DIM]
        upad_scr[pl.ds(CONV_PAD + r, ROW_TILE), :] = a * _sigmoid(gate)
        return c

    lax.fori_loop(0, n_row_tiles, glu_body, 0)

    scale = NA_HEAD_DIM ** -0.5

    def na_project_tile(r, rows):
        res = _dot(h_scr[pl.ds(r, rows), :], win_ref[:, NA_COL0:NA_COL0 + 3 * NA_DIM])
        p_na[pl.ds(r, rows), 0:NA_DIM] = res[:, 0:NA_DIM] * scale
        h_scr[pl.ds(r, rows), 0:2 * NA_DIM] = res[:, NA_DIM:3 * NA_DIM].astype(BF16)

    def conv_tile(r):
        off = CONV_PAD - CONV_WIDTH // 2
        groups = -(-CONV_WIDTH // SUBLANES)
        win_rows = CONV_TILE + 2 * CONV_PAD
        for c0 in range(0, CONV_DIM, LANES):
            win = upad_scr[pl.ds(r, win_rows), c0:c0 + LANES]
            acc = jnp.zeros((CONV_TILE, LANES), F32) + cb_ref[:, c0:c0 + LANES]
            shifted = win
            for s in range(SUBLANES):
                step = off if s == 0 else 1
                if step:
                    shifted = pltpu.roll(shifted, win_rows - step, axis=0)
                for a in range(groups):
                    j = SUBLANES * a + s
                    if j < CONV_WIDTH:
                        acc = acc + shifted[SUBLANES * a:SUBLANES * a + CONV_TILE, :] * cw_ref[j:j + 1, c0:c0 + LANES]
            p_conv[pl.ds(r, CONV_TILE), c0:c0 + LANES] = acc

    def conv_body(i, c):
        r = pl.multiple_of(i * CONV_OUTER_TILE, CONV_OUTER_TILE)
        na_project_tile(r, CONV_OUTER_TILE)
        for sub in range(CONV_OUTER_TILE // CONV_TILE):
            conv_tile(pl.multiple_of(r + sub * CONV_TILE, CONV_TILE))
        return c

    lax.fori_loop(0, SEQ // CONV_OUTER_TILE, conv_body, 0)

    def conv_norm_body(i, c):
        r = pl.multiple_of(i * ROW_TILE, ROW_TILE)
        acc = p_conv[pl.ds(r, ROW_TILE), 0:CONV_DIM]
        mu = jnp.mean(acc, axis=-1, keepdims=True)
        d = acc - mu
        var = jnp.mean(d * d, axis=-1, keepdims=True)
        u = d * lax.rsqrt(var + EPS) * lng_ref[...] + lnb_ref[...]
        y_scr[pl.ds(r, ROW_TILE), HG_DIM + NA_DIM:D_MODEL] = (u * _sigmoid(u)).astype(BF16)
        return c

    lax.fori_loop(0, n_row_tiles, conv_norm_body, 0)

    lane = lax.broadcasted_iota(jnp.int32, (1, LANES), 1)
    first_half = lane < NA_HEAD_DIM

    n_na_steps = ROWS // NA_ROWS_PER_STEP

    def na_units(step):
        units = []
        for dr in range(NA_ROWS_PER_STEP):
            r = step * NA_ROWS_PER_STEP + dr
            if isinstance(r, int):
                r0 = min(max(r - NA_KH // 2, 0), ROWS - NA_KH)
                qrow, krow = r * GRID_W, r0 * GRID_W
            else:
                r0 = jnp.clip(r - NA_KH // 2, 0, ROWS - NA_KH)
                qrow, krow = pl.multiple_of(r * GRID_W, GRID_W), pl.multiple_of(r0 * GRID_W, GRID_W)
            for hp in range(NA_HEADS // 2):
                units.append((qrow, krow, r - r0, hp))
        return units

    def na_scores(units):
        for u, (qrow, krow, var, hp) in enumerate(units):
            c0 = hp * LANES
            qt = p_na[pl.ds(qrow, GRID_W), c0:c0 + LANES]
            qm = jnp.concatenate([jnp.where(first_half, qt, 0.0), jnp.where(first_half, 0.0, qt)], axis=0).astype(BF16)
            qk_scr[u] = _dot_nt(qm, h_scr[pl.ds(krow, NA_KEYS), c0:c0 + LANES])

    na_scores(na_units(0))

    def na_body(i, c):
        units = na_units(i)
        probs = []
        for u, (qrow, krow, var, hp) in enumerate(units):
            bt = bias_ref[hp, var]
            s = jnp.where(bt > 0.5 * NEG_INF, qk_scr[u] + bt, NEG_INF)
            probs.append(jnp.exp(s - jnp.max(s, axis=-1, keepdims=True)).astype(BF16))
        na_scores(na_units(jnp.minimum(i + 1, n_na_steps - 1)))
        ones = jnp.ones((NA_KEYS, LANES), BF16)
        for (qrow, krow, var, hp), e in zip(units, probs):
            c0 = hp * LANES
            v_ones = jnp.concatenate([h_scr[pl.ds(krow, NA_KEYS), NA_DIM + c0:NA_DIM + c0 + LANES], ones], axis=1)
            pv = _dot(e, v_ones)
            o = pv[:, 0:LANES] / pv[:, LANES:2 * LANES]
            y_scr[pl.ds(qrow, GRID_W), HG_DIM + c0:HG_DIM + c0 + LANES] = jnp.where(
                first_half, o[0:GRID_W], o[GRID_W:2 * GRID_W]).astype(BF16)
        return c

    lax.fori_loop(0, n_na_steps, na_body, 0)


def _ffn_kernel(final, x_ref, y_ref, wout_ref, ng_ref, wgu_ref, wd_ref, fg_ref, o_ref, h_scr, acc_scr, x_scr):
    x = x_ref[...] + _dot(y_ref[...], wout_ref[...].astype(BF16))
    x_scr[...] = x
    ms = jnp.mean(x * x, axis=-1, keepdims=True)
    h_scr[...] = (x * lax.rsqrt(ms + EPS) * ng_ref[...]).astype(BF16)
    for c in range(FFN_NCHUNK):
        gt = _dot(h_scr[...], wgu_ref[:, FFN_CHUNK * c:FFN_CHUNK * (c + 1)].astype(BF16))
        up = _dot(h_scr[...], wgu_ref[:, FFN_DIM + FFN_CHUNK * c:FFN_DIM + FFN_CHUNK * (c + 1)].astype(BF16))
        a = (gt * _sigmoid(gt) * up).astype(BF16)
        d = _dot(a, wd_ref[FFN_CHUNK * c:FFN_CHUNK * (c + 1), :].astype(BF16))
        if c == 0:
            acc_scr[...] = d
        else:
            acc_scr[...] += d
    y = x_scr[...] + acc_scr[...]
    if final:
        y = y * lax.rsqrt(jnp.mean(y * y, axis=-1, keepdims=True) + EPS) * fg_ref[...]
    o_ref[...] = y


def _resident(shape):
    nd = len(shape)
    return pl.BlockSpec(shape, lambda *_: (0,) * nd, pipeline_mode=pl.Buffered(1))


def _layer_slice(layer, shape):
    nd = len(shape)
    return pl.BlockSpec((None,) + tuple(shape[1:]), lambda *_: (layer,) + (0,) * (nd - 1), pipeline_mode=pl.Buffered(1))


def _mixer_call(layer, x, ng, win, hlb, hgn, rpbp, cw, cb, lng, lnb):
    batch = x.shape[0]
    stacked = (ng, win, None, hgn, rpbp, cw, cb, lng, lnb)
    return pl.pallas_call(
        functools.partial(_mixer_kernel, layer),
        grid=(batch, SEQ // X_TILE),
        in_specs=[pl.BlockSpec((None, X_TILE, D_MODEL), lambda b, t: (b, t, 0))]
        + [_resident(hlb.shape) if a is None else _layer_slice(layer, a.shape) for a in stacked],
        out_specs=pl.BlockSpec((None, SEQ, D_MODEL), lambda b, t: (b, 0, 0)),
        out_shape=jax.ShapeDtypeStruct(x.shape, BF16),
        scratch_shapes=[
            pltpu.VMEM((SEQ, D_MODEL), BF16),
            pltpu.VMEM((2, SEQ, P_COLS), F32),
            pltpu.VMEM((SEQ, HG_HEAD_DIM), F32),
            pltpu.VMEM((2, SEQ, HG_HEAD_DIM), BF16),
            pltpu.VMEM((2, HG_NCHUNK, HG_HEAD_DIM, HG_HEAD_DIM), F32),
            pltpu.VMEM((2, HG_NCHUNK, HG_HEAD_DIM, HG_HEAD_DIM), BF16),
            pltpu.VMEM((2, HG_NCHUNK, SUBLANES, LANES), F32),
            pltpu.VMEM((SEQ + 2 * CONV_PAD, CONV_DIM), F32),
            pltpu.VMEM((NA_HEADS // 2, NA_KH, 2 * GRID_W, NA_KEYS), F32),
            pltpu.VMEM((2, D_MODEL, HG_COLS), BF16),
            pltpu.VMEM((NA_ROWS_PER_STEP * NA_HEADS // 2, 2 * GRID_W, NA_KEYS), F32),
            pltpu.VMEM((2, 3, HG_TILE, HG_HEAD_DIM), BF16),
        ],
        compiler_params=pltpu.CompilerParams(dimension_semantics=("arbitrary", "arbitrary"),
                                             vmem_limit_bytes=VMEM_LIMIT_BYTES),
        name=f"mixer_l{layer}",
    )(x, ng, win, hlb, hgn, rpbp, cw, cb, lng, lnb)


def _ffn_call(layer, final, x2d, y2d, wout, ng, wgu, wd, fg):
    n_tok = x2d.shape[0]
    tile = pl.BlockSpec((FFN_TILE, D_MODEL), lambda i: (i, 0))
    return pl.pallas_call(
        functools.partial(_ffn_kernel, final),
        grid=(n_tok // FFN_TILE,),
        in_specs=[tile, tile] + [_layer_slice(layer, a.shape) for a in (wout, ng, wgu, wd)] + [_resident(fg.shape)],
        out_specs=tile,
        out_shape=jax.ShapeDtypeStruct(x2d.shape, x2d.dtype),
        scratch_shapes=[pltpu.VMEM((FFN_TILE, D_MODEL), BF16), pltpu.VMEM((FFN_TILE, D_MODEL), F32),
                        pltpu.VMEM((FFN_TILE, D_MODEL), F32)],
        compiler_params=pltpu.CompilerParams(dimension_semantics=("arbitrary",), vmem_limit_bytes=VMEM_LIMIT_BYTES),
        name=f"ffn_l{layer}",
    )(x2d, y2d, wout, ng, wgu, wd, fg)


def _pad_rpb(rpb):
    v = jnp.pad(rpb, ((0, 0), (0, 0), (0, RPB_ROWS - NA_NREL_H), (RPB_LEAD, LANES - RPB_LEAD - NA_NREL_W)))
    return v.reshape(DEPTH, NA_HEADS * RPB_ROWS, LANES)


def kernel(x, mix_norm_g, w_in, hg_lower_bounds, hg_norm_g, na_rpb, conv_w, conv_b, conv_ln_g, conv_ln_b, w_out,
           ffn_norm_g, w_gate_up, w_down, final_norm_g):
    batch, seq, d = x.shape
    assert (seq, d) == (SEQ, D_MODEL) and w_in.shape == (DEPTH, D_MODEL, D_IN)
    row = lambda p: p.astype(F32).reshape(DEPTH, 1, p.shape[-1])
    win = w_in.astype(BF16)
    wout, wgu, wd = (w.astype(F32) for w in (w_out, w_gate_up, w_down))
    hlb = hg_lower_bounds.astype(F32)
    rpbp = _pad_rpb(na_rpb.astype(F32))
    mix_g, hgn, cb, lng, lnb, ffn_g = (row(p) for p in (mix_norm_g, hg_norm_g, conv_b, conv_ln_g, conv_ln_b, ffn_norm_g))
    fg = final_norm_g.astype(F32).reshape(1, D_MODEL)
    for l in range(DEPTH):
        y = _mixer_call(l, x, mix_g, win, hlb, hgn, rpbp, conv_w.astype(F32), cb, lng, lnb)
        x2d = _ffn_call(l, l == DEPTH - 1, x.reshape(batch * seq, d), y.reshape(batch * seq, d), wout, ffn_g, wgu, wd, fg)
        x = x2d.reshape(batch, seq, d)
    return x
```

```python
import functools

import numpy as np
import jax
import jax.numpy as jnp
from jax import lax
from jax.experimental import pallas as pl
from jax.experimental.pallas import tpu as pltpu

D_MODEL = 1024
SEQ = 2048
DEPTH = 2
GRID_W = 64
ROWS = SEQ // GRID_W
HG_DIM = 512
HG_HEADS = 4
HG_HEAD_DIM = 128
NA_DIM = 256
NA_HEADS = 4
NA_HEAD_DIM = 64
NA_KH = 8
NA_KW = 16
CONV_DIM = 256
CONV_WIDTH = 31
D_IN = 5 * HG_DIM + 3 * NA_DIM + 2 * CONV_DIM
FFN_DIM = 2816
EPS = 1e-6
NEG_INF = -1e30

LANES = 128
SUBLANES = 8
VMEM_LIMIT_BYTES = 60 * 1024 * 1024

ROW_TILE = 1024
HG_CHUNK = 64
HG_CHUNK_LOG2 = HG_CHUNK.bit_length() - 1
assert HG_CHUNK == 1 << HG_CHUNK_LOG2
HG_NCHUNK = SEQ // HG_CHUNK
HG_TILE = 256
HG_OUT_TILE = 512
HG_LOCAL_UNROLL = 1
X_TILE = 512
HG_COLS = 5 * HG_HEAD_DIM
NA_COL0 = 5 * HG_DIM
CV_COL0 = NA_COL0 + 3 * NA_DIM
NA_KEYS = NA_KH * GRID_W
NA_NREL_H = 2 * NA_KH - 1
NA_NREL_W = 2 * NA_KW - 1
RPB_LEAD = GRID_W - NA_KW
RPB_ROWS = 16
assert RPB_ROWS >= NA_NREL_H and RPB_ROWS % SUBLANES == 0
NA_ROWS_PER_STEP = 2
CONV_TILE = 64
CONV_OUTER_TILE = 256
CONV_PAD = 16
P_COLS = HG_COLS
assert P_COLS >= 2 * CONV_DIM and P_COLS >= NA_DIM
FFN_TILE = 512
FFN_CHUNK = 256
FFN_NCHUNK = FFN_DIM // FFN_CHUNK

F32 = jnp.float32
BF16 = jnp.bfloat16


def _sigmoid(x):
    return 1.0 / (1.0 + jnp.exp(-x))


def _dot(a, b):
    return jnp.dot(a, b, preferred_element_type=F32)


def _dot_nt(a, b):
    return lax.dot_general(a, b, (((1,), (1,)), ((), ())), preferred_element_type=F32)


def _dot_tn(a, b):
    return lax.dot_general(a, b, (((0,), (0,)), ((), ())), preferred_element_type=F32)


def _split3(x):
    hi = x.astype(BF16)
    r1 = x - hi.astype(F32)
    mid = r1.astype(BF16)
    lo = (r1 - mid.astype(F32)).astype(BF16)
    return hi, mid, lo


def _build_na_bias(rpbp_ref, bias_ref):
    q_i = lax.broadcasted_iota(jnp.int32, (GRID_W, LANES), 0)
    l_i = lax.broadcasted_iota(jnp.int32, (GRID_W, LANES), 1)
    kc = jnp.bitwise_and(l_i, GRID_W - 1)
    q_start = jnp.clip(q_i - NA_KW // 2, 0, GRID_W - NA_KW)
    visible = (kc >= q_start) & (kc < q_start + NA_KW)
    left = l_i < GRID_W
    q_all = jnp.bitwise_and(lax.broadcasted_iota(jnp.int32, (RPB_ROWS * GRID_W, LANES), 0), GRID_W - 1)

    def head_body(h, c):
        rows = rpbp_ref[pl.ds(pl.multiple_of(h * RPB_ROWS, RPB_ROWS), RPB_ROWS), :]
        x = jnp.concatenate([jnp.broadcast_to(rows[ro:ro + 1, :], (GRID_W, LANES)) for ro in range(RPB_ROWS)], axis=0)
        even = pltpu.roll(x, LANES - (GRID_W - 1), axis=1)
        for b in range(GRID_W.bit_length() - 1):
            even = jnp.where(jnp.bitwise_and(q_all, 1 << b) != 0, pltpu.roll(even, 1 << b, axis=1), even)
        odd = pltpu.roll(even, GRID_W, axis=1)
        hp = h // 2
        q0 = pl.multiple_of((h % 2) * GRID_W, GRID_W)
        for var in range(NA_KH):
            for kp in range(NA_KH // 2):
                ro = 2 * kp - var + NA_KH - 1
                tile = jnp.where(left, even[ro * GRID_W:(ro + 1) * GRID_W], odd[(ro + 1) * GRID_W:(ro + 2) * GRID_W])
                bias_ref[hp, var, pl.ds(q0, GRID_W), kp * LANES:(kp + 1) * LANES] = jnp.where(visible, tile, NEG_INF)
        return c

    lax.fori_loop(0, NA_HEADS, head_body, 0)


def _load_head_weights(win_ref, wh_ref, hd):
    for grp in range(HG_COLS // HG_HEAD_DIM):
        col = grp * HG_DIM + hd * HG_HEAD_DIM
        wh_ref[:, grp * HG_HEAD_DIM:(grp + 1) * HG_HEAD_DIM] = win_ref[:, col:col + HG_HEAD_DIM]


def _mixer_kernel(layer, x_ref, ng_ref, win_ref, hlb_ref, hgn_ref, rpbp_ref, cw_ref, cb_ref, lng_ref, lnb_ref,
                  y_ref, h_scr, p_scr, oi_scr, qa_scr, d_scr, st_scr, dec_scr, upad_scr, bias_ref,
                  wh_scr, qk_scr, stage_scr):
    t = pl.program_id(1)

    @pl.when((pl.program_id(0) == 0) & (t == 0))
    def _():
        _build_na_bias(rpbp_ref, bias_ref)

    @pl.when(t == 0)
    def _():
        _load_head_weights(win_ref, wh_scr.at[0], 0)
        upad_scr[0:CONV_PAD, :] = jnp.zeros((CONV_PAD, CONV_DIM), F32)
        upad_scr[CONV_PAD + SEQ:CONV_PAD + SEQ + CONV_PAD, :] = jnp.zeros((CONV_PAD, CONV_DIM), F32)

    row0 = pl.multiple_of(t * X_TILE, X_TILE)
    xt = x_ref[...]
    ms = jnp.mean(xt * xt, axis=-1, keepdims=True)
    ht = (xt * lax.rsqrt(ms + EPS) * ng_ref[...]).astype(BF16)
    h_scr[pl.ds(row0, X_TILE), :] = ht
    p_scr[0, pl.ds(row0, X_TILE), 0:HG_COLS] = _dot(ht, wh_scr[0])
    cv = _dot(ht, win_ref[:, CV_COL0:CV_COL0 + 2 * CONV_DIM])
    upad_scr[pl.ds(pl.multiple_of(CONV_PAD + row0, SUBLANES), X_TILE), :] = (
        cv[:, 0:CONV_DIM] * _sigmoid(cv[:, CONV_DIM:2 * CONV_DIM]))

    @pl.when(t == SEQ // X_TILE - 1)
    def _():
        _mix_sequence(layer, win_ref, hlb_ref, hgn_ref, cw_ref, cb_ref, lng_ref, lnb_ref, y_ref, h_scr, p_scr, oi_scr,
                      qa_scr, d_scr, st_scr, dec_scr, upad_scr, bias_ref, wh_scr, qk_scr, stage_scr)


def _mix_sequence(layer, win_ref, hlb_ref, hgn_ref, cw_ref, cb_ref, lng_ref, lnb_ref, y_scr, h_scr, p_scr, oi_scr,
                  qa_scr, d_scr, st_scr, dec_scr, upad_scr, bias_ref, wh_scr, qk_scr, stage_scr):
    n_row_tiles = SEQ // ROW_TILE

    p_bufs = (p_scr.at[0], p_scr.at[1])
    wh_bufs = (wh_scr.at[0], wh_scr.at[1])
    p_na = p_bufs[HG_HEADS % 2]

    def load_head_weights(hd):
        _load_head_weights(win_ref, wh_bufs[hd % 2], hd)

    hlb = [hlb_ref[d] for d in range(DEPTH)]
    hmax = functools.reduce(jnp.maximum, hlb)
    hexp = [jnp.exp(v - hmax) for v in hlb]
    hsum = functools.reduce(lambda a, b: a + b, hexp)
    hsm = [e / hsum for e in hexp]
    lbs = functools.reduce(lambda a, b: a + b, hsm[:layer + 1]) - hsm[0]

    L = HG_CHUNK
    TL = HG_TILE
    cpt = TL // L
    row_c = lax.broadcasted_iota(jnp.int32, (TL, TL), 0)
    col_c = lax.broadcasted_iota(jnp.int32, (TL, TL), 1)
    same_chunk = jnp.right_shift(row_c, HG_CHUNK_LOG2) == jnp.right_shift(col_c, HG_CHUNK_LOG2)
    lower = same_chunk & (row_c >= col_c)
    upper = same_chunk & (col_c >= row_c)
    lower_b = jnp.where(lower, 1.0, 0.0).astype(BF16)
    upper_b = jnp.where(upper, 1.0, 0.0).astype(BF16)

    def chunk_rows(a, row):
        return [a[c * L + row:c * L + row + 1, :] for c in range(cpt)]

    def spread(rows):
        return jnp.concatenate([jnp.broadcast_to(rw, (L, LANES)) for rw in rows], axis=0)

    scale = NA_HEAD_DIM ** -0.5

    def na_project_tile(r, rows):
        res = _dot(h_scr[pl.ds(r, rows), :], win_ref[:, NA_COL0:NA_COL0 + 3 * NA_DIM])
        p_na[pl.ds(r, rows), 0:NA_DIM] = res[:, 0:NA_DIM] * scale
        h_scr[pl.ds(r, rows), 0:2 * NA_DIM] = res[:, NA_DIM:3 * NA_DIM].astype(BF16)

    def conv_tile(r):
        off = CONV_PAD - CONV_WIDTH // 2
        groups = -(-CONV_WIDTH // SUBLANES)
        win_rows = CONV_TILE + 2 * CONV_PAD
        halves = []
        for c0 in range(0, CONV_DIM, LANES):
            win = upad_scr[pl.ds(r, win_rows), c0:c0 + LANES]
            acc = jnp.zeros((CONV_TILE, LANES), F32) + cb_ref[:, c0:c0 + LANES]
            shifted = win
            for s in range(SUBLANES):
                step = off if s == 0 else 1
                if step:
                    shifted = pltpu.roll(shifted, win_rows - step, axis=0)
                for a in range(groups):
                    j = SUBLANES * a + s
                    if j < CONV_WIDTH:
                        acc = acc + shifted[SUBLANES * a:SUBLANES * a + CONV_TILE, :] * cw_ref[j:j + 1, c0:c0 + LANES]
            halves.append(acc)
        acc = jnp.concatenate(halves, axis=1)
        mu = jnp.mean(acc, axis=-1, keepdims=True)
        d = acc - mu
        var = jnp.mean(d * d, axis=-1, keepdims=True)
        u = d * lax.rsqrt(var + EPS) * lng_ref[...] + lnb_ref[...]
        y_scr[pl.ds(r, CONV_TILE), HG_DIM + NA_DIM:D_MODEL] = (u * _sigmoid(u)).astype(BF16)

    assert HG_HEADS * (SEQ // HG_TILE) == SEQ // CONV_TILE
    for hd in range(HG_HEADS):
        sl = slice(hd * HG_HEAD_DIM, (hd + 1) * HG_HEAD_DIM)
        pc = p_bufs[hd % 2]
        if hd + 1 < HG_HEADS:
            load_head_weights(hd + 1)
        directions = (
            (HG_HEAD_DIM, lbs[0:1, sl], lower, lower_b, L // 2 - 1, L - 1),
            (2 * HG_HEAD_DIM, lbs[1:2, sl], upper, upper_b, L // 2, 0),
        )

        def tile_start(i):
            return i * TL if isinstance(i, int) else pl.multiple_of(i * TL, TL)

        def hg_scale(i):
            r = tile_start(i)
            q_raw = pc[pl.ds(r, TL), 0:HG_HEAD_DIM]
            q = q_raw * _sigmoid(q_raw)
            gates = []
            for zcol, lb, mask, cum_op, ref_row, last_row in directions:
                z = pc[pl.ds(r, TL), zcol:zcol + HG_HEAD_DIM]
                t = jnp.exp(-jnp.abs(z))
                big = 1.0 / (1.0 + t)
                small = t * big
                nonneg = z >= 0.0
                f = lb + (1.0 - lb) * jnp.where(nonneg, big, small)
                k = (1.0 - lb) * jnp.where(nonneg, small, big)
                gates.append((k, _dot(cum_op, jnp.concatenate(_split3(jnp.log(f)), axis=1))))
            for d, ((zcol, lb, mask, cum_op, ref_row, last_row), (k, c3)) in enumerate(zip(directions, gates)):
                cum = c3[:, 0:LANES] + c3[:, LANES:2 * LANES] + c3[:, 2 * LANES:3 * LANES]
                ref_rows = chunk_rows(cum, ref_row)
                last_rows = chunk_rows(cum, last_row)
                rel = cum - spread(ref_rows)
                q_in = q * jnp.exp(rel)
                k_in = k * jnp.exp(-rel)
                qa_scr[d, pl.ds(r, TL), :] = (q_in * spread([jnp.exp(rw) for rw in ref_rows])).astype(BF16)
                k_d = (k_in * spread([jnp.exp(lr - rr) for lr, rr in zip(last_rows, ref_rows)])).astype(BF16)
                for cc in range(cpt):
                    dec_scr[d, i * cpt + cc] = jnp.broadcast_to(jnp.exp(last_rows[cc]), (SUBLANES, LANES))
                stage_scr[d, 0] = q_in.astype(BF16)
                stage_scr[d, 1] = k_in.astype(BF16)
                stage_scr[d, 2] = k_d

        def hg_contract(i):
            r = tile_start(i)
            vb = pc[pl.ds(r, TL), 3 * HG_HEAD_DIM:4 * HG_HEAD_DIM].astype(BF16)
            qk = [_dot_nt(stage_scr[d, 0], stage_scr[d, 1]) for d in range(len(directions))]
            masked = [jnp.where(mask, s, 0.0) for (zcol, lb, mask, cum_op, ref_row, last_row), s in zip(directions, qk)]
            oi_scr[pl.ds(r, TL), :] = _dot(functools.reduce(lambda a, b: a + b, masked).astype(BF16), vb)
            k_d = jnp.concatenate([stage_scr[d, 2] for d in range(len(directions))], axis=1)
            for cc in range(cpt):
                inc = _dot_tn(vb[cc * L:(cc + 1) * L], k_d[cc * L:(cc + 1) * L])
                for d in range(len(directions)):
                    d_scr[d, i * cpt + cc] = inc[:, d * HG_HEAD_DIM:(d + 1) * HG_HEAD_DIM]

        def side_work(i, hd=hd):
            r = tile_start(i)
            if hd + 1 < HG_HEADS:
                p_bufs[(hd + 1) % 2][pl.ds(r, TL), 0:HG_COLS] = _dot(h_scr[pl.ds(r, TL), :], wh_bufs[(hd + 1) % 2][...])
            else:
                na_project_tile(r, TL)
            slot = hd * (SEQ // TL) + i
            conv_tile(slot * CONV_TILE if isinstance(slot, int) else pl.multiple_of(slot * CONV_TILE, CONV_TILE))

        def hg_local_body(i, c):
            hg_contract(i)
            hg_scale(i + 1)
            side_work(i + 1)
            return c

        hg_scale(0)
        side_work(0)
        lax.fori_loop(0, SEQ // TL - 1, hg_local_body, 0, unroll=HG_LOCAL_UNROLL)
        hg_contract(SEQ // TL - 1)

        def hg_scan_body(n, carry):
            s_f, s_b = carry
            m = HG_NCHUNK - 1 - n
            st_scr[0, n] = s_f.astype(BF16)
            st_scr[1, m] = s_b.astype(BF16)
            s_f = s_f * dec_scr[0, n][0:1, :] + d_scr[0, n]
            s_b = s_b * dec_scr[1, m][0:1, :] + d_scr[1, m]
            return s_f, s_b

        s_zero = jnp.zeros((HG_HEAD_DIM, HG_HEAD_DIM), F32)
        lax.fori_loop(0, HG_NCHUNK, hg_scan_body, (s_zero, s_zero))

        def hg_out_body(i, c):
            r = pl.multiple_of(i * HG_OUT_TILE, HG_OUT_TILE)
            inter = []
            for cc in range(HG_OUT_TILE // L):
                n = i * (HG_OUT_TILE // L) + cc
                rc = pl.multiple_of(r + cc * L, L)
                qa = jnp.concatenate([qa_scr[0, pl.ds(rc, L), :], qa_scr[1, pl.ds(rc, L), :]], axis=1)
                st = jnp.concatenate([st_scr[0, n], st_scr[1, n]], axis=1)
                inter.append(_dot_nt(qa, st))
            o = oi_scr[pl.ds(r, HG_OUT_TILE), :] + jnp.concatenate(inter, axis=0)
            o = o * lax.rsqrt(jnp.mean(o * o, axis=-1, keepdims=True) + EPS) * hgn_ref[...]
            gate = pc[pl.ds(r, HG_OUT_TILE), 4 * HG_HEAD_DIM:5 * HG_HEAD_DIM]
            y_scr[pl.ds(r, HG_OUT_TILE), sl] = (o * (gate * _sigmoid(gate))).astype(BF16)
            return c

        lax.fori_loop(0, SEQ // HG_OUT_TILE, hg_out_body, 0, unroll=2)

    lane = lax.broadcasted_iota(jnp.int32, (1, LANES), 1)
    first_half = lane < NA_HEAD_DIM

    n_na_steps = ROWS // NA_ROWS_PER_STEP

    def na_units(step):
        units = []
        for dr in range(NA_ROWS_PER_STEP):
            r = step * NA_ROWS_PER_STEP + dr
            if isinstance(r, int):
                r0 = min(max(r - NA_KH // 2, 0), ROWS - NA_KH)
                qrow, krow = r * GRID_W, r0 * GRID_W
            else:
                r0 = jnp.clip(r - NA_KH // 2, 0, ROWS - NA_KH)
                qrow, krow = pl.multiple_of(r * GRID_W, GRID_W), pl.multiple_of(r0 * GRID_W, GRID_W)
            for hp in range(NA_HEADS // 2):
                units.append((qrow, krow, r - r0, hp))
        return units

    def na_scores(units):
        for u, (qrow, krow, var, hp) in enumerate(units):
            c0 = hp * LANES
            qt = p_na[pl.ds(qrow, GRID_W), c0:c0 + LANES]
            qm = jnp.concatenate([jnp.where(first_half, qt, 0.0), jnp.where(first_half, 0.0, qt)], axis=0).astype(BF16)
            qk_scr[u] = _dot_nt(qm, h_scr[pl.ds(krow, NA_KEYS), c0:c0 + LANES])

    na_scores(na_units(0))

    def na_body(i, c):
        units = na_units(i)
        probs = []
        for u, (qrow, krow, var, hp) in enumerate(units):
            bt = bias_ref[hp, var]
            s = jnp.where(bt > 0.5 * NEG_INF, qk_scr[u] + bt, NEG_INF)
            probs.append(jnp.exp(s - jnp.max(s, axis=-1, keepdims=True)).astype(BF16))
        na_scores(na_units(jnp.minimum(i + 1, n_na_steps - 1)))
        ones = jnp.ones((NA_KEYS, LANES), BF16)
        for (qrow, krow, var, hp), e in zip(units, probs):
            c0 = hp * LANES
            v_ones = jnp.concatenate([h_scr[pl.ds(krow, NA_KEYS), NA_DIM + c0:NA_DIM + c0 + LANES], ones], axis=1)
            pv = _dot(e, v_ones)
            o = pv[:, 0:LANES] / pv[:, LANES:2 * LANES]
            y_scr[pl.ds(qrow, GRID_W), HG_DIM + c0:HG_DIM + c0 + LANES] = jnp.where(
                first_half, o[0:GRID_W], o[GRID_W:2 * GRID_W]).astype(BF16)
        return c

    lax.fori_loop(0, n_na_steps, na_body, 0)


def _ffn_kernel(final, x_ref, y_ref, wout_ref, ng_ref, wgu_ref, wd_ref, fg_ref, o_ref, h_scr, acc_scr, x_scr):
    x = x_ref[...] + _dot(y_ref[...], wout_ref[...].astype(BF16))
    x_scr[...] = x
    ms = jnp.mean(x * x, axis=-1, keepdims=True)
    h_scr[...] = (x * lax.rsqrt(ms + EPS) * ng_ref[...]).astype(BF16)
    for c in range(FFN_NCHUNK):
        gt = _dot(h_scr[...], wgu_ref[:, FFN_CHUNK * c:FFN_CHUNK * (c + 1)].astype(BF16))
        up = _dot(h_scr[...], wgu_ref[:, FFN_DIM + FFN_CHUNK * c:FFN_DIM + FFN_CHUNK * (c + 1)].astype(BF16))
        a = (gt * _sigmoid(gt) * up).astype(BF16)
        d = _dot(a, wd_ref[FFN_CHUNK * c:FFN_CHUNK * (c + 1), :].astype(BF16))
        if c == 0:
            acc_scr[...] = d
        else:
            acc_scr[...] += d
    y = x_scr[...] + acc_scr[...]
    if final:
        y = y * lax.rsqrt(jnp.mean(y * y, axis=-1, keepdims=True) + EPS) * fg_ref[...]
    o_ref[...] = y


def _resident(shape):
    nd = len(shape)
    return pl.BlockSpec(shape, lambda *_: (0,) * nd, pipeline_mode=pl.Buffered(1))


def _layer_slice(layer, shape):
    nd = len(shape)
    return pl.BlockSpec((None,) + tuple(shape[1:]), lambda *_: (layer,) + (0,) * (nd - 1), pipeline_mode=pl.Buffered(1))


def _mixer_call(layer, x, ng, win, hlb, hgn, rpbp, cw, cb, lng, lnb):
    batch = x.shape[0]
    stacked = (ng, win, None, hgn, rpbp, cw, cb, lng, lnb)
    return pl.pallas_call(
        functools.partial(_mixer_kernel, layer),
        grid=(batch, SEQ // X_TILE),
        in_specs=[pl.BlockSpec((None, X_TILE, D_MODEL), lambda b, t: (b, t, 0))]
        + [_resident(hlb.shape) if a is None else _layer_slice(layer, a.shape) for a in stacked],
        out_specs=pl.BlockSpec((None, SEQ, D_MODEL), lambda b, t: (b, 0, 0)),
        out_shape=jax.ShapeDtypeStruct(x.shape, BF16),
        scratch_shapes=[
            pltpu.VMEM((SEQ, D_MODEL), BF16),
            pltpu.VMEM((2, SEQ, P_COLS), F32),
            pltpu.VMEM((SEQ, HG_HEAD_DIM), F32),
            pltpu.VMEM((2, SEQ, HG_HEAD_DIM), BF16),
            pltpu.VMEM((2, HG_NCHUNK, HG_HEAD_DIM, HG_HEAD_DIM), F32),
            pltpu.VMEM((2, HG_NCHUNK, HG_HEAD_DIM, HG_HEAD_DIM), BF16),
            pltpu.VMEM((2, HG_NCHUNK, SUBLANES, LANES), F32),
            pltpu.VMEM((SEQ + 2 * CONV_PAD, CONV_DIM), F32),
            pltpu.VMEM((NA_HEADS // 2, NA_KH, 2 * GRID_W, NA_KEYS), F32),
            pltpu.VMEM((2, D_MODEL, HG_COLS), BF16),
            pltpu.VMEM((NA_ROWS_PER_STEP * NA_HEADS // 2, 2 * GRID_W, NA_KEYS), F32),
            pltpu.VMEM((2, 3, HG_TILE, HG_HEAD_DIM), BF16),
        ],
        compiler_params=pltpu.CompilerParams(dimension_semantics=("arbitrary", "arbitrary"),
                                             vmem_limit_bytes=VMEM_LIMIT_BYTES),
        name=f"mixer_l{layer}",
    )(x, ng, win, hlb, hgn, rpbp, cw, cb, lng, lnb)


def _ffn_call(layer, final, x2d, y2d, wout, ng, wgu, wd, fg):
    n_tok = x2d.shape[0]
    tile = pl.BlockSpec((FFN_TILE, D_MODEL), lambda i: (i, 0))
    return pl.pallas_call(
        functools.partial(_ffn_kernel, final),
        grid=(n_tok // FFN_TILE,),
        in_specs=[tile, tile] + [_layer_slice(layer, a.shape) for a in (wout, ng, wgu, wd)] + [_resident(fg.shape)],
        out_specs=tile,
        out_shape=jax.ShapeDtypeStruct(x2d.shape, x2d.dtype),
        scratch_shapes=[pltpu.VMEM((FFN_TILE, D_MODEL), BF16), pltpu.VMEM((FFN_TILE, D_MODEL), F32),
                        pltpu.VMEM((FFN_TILE, D_MODEL), F32)],
        compiler_params=pltpu.CompilerParams(dimension_semantics=("arbitrary",), vmem_limit_bytes=VMEM_LIMIT_BYTES),
        name=f"ffn_l{layer}",
    )(x2d, y2d, wout, ng, wgu, wd, fg)


def _pad_rpb(rpb):
    v = jnp.pad(rpb, ((0, 0), (0, 0), (0, RPB_ROWS - NA_NREL_H), (RPB_LEAD, LANES - RPB_LEAD - NA_NREL_W)))
    return v.reshape(DEPTH, NA_HEADS * RPB_ROWS, LANES)


def kernel(x, mix_norm_g, w_in, hg_lower_bounds, hg_norm_g, na_rpb, conv_w, conv_b, conv_ln_g, conv_ln_b, w_out,
           ffn_norm_g, w_gate_up, w_down, final_norm_g):
    batch, seq, d = x.shape
    assert (seq, d) == (SEQ, D_MODEL) and w_in.shape == (DEPTH, D_MODEL, D_IN)
    row = lambda p: p.astype(F32).reshape(DEPTH, 1, p.shape[-1])
    win = w_in.astype(BF16)
    wout, wgu, wd = (w.astype(F32) for w in (w_out, w_gate_up, w_down))
    hlb = hg_lower_bounds.astype(F32)
    rpbp = _pad_rpb(na_rpb.astype(F32))
    mix_g, hgn, cb, lng, lnb, ffn_g = (row(p) for p in (mix_norm_g, hg_norm_g, conv_b, conv_ln_g, conv_ln_b, ffn_norm_g))
    fg = final_norm_g.astype(F32).reshape(1, D_MODEL)
    for l in range(DEPTH):
        y = _mixer_call(l, x, mix_g, win, hlb, hgn, rpbp, conv_w.astype(F32), cb, lng, lnb)
        x2d = _ffn_call(l, l == DEPTH - 1, x.reshape(batch * seq, d), y.reshape(batch * seq, d), wout, ffn_g, wgu, wd, fg)
        x = x2d.reshape(batch, seq, d)
    return x
```

```python
import functools

import jax
import jax.numpy as jnp
from jax import lax
from jax.experimental import pallas as pl
from jax.experimental.pallas import tpu as pltpu

D_MODEL = 1024
SEQ = 2048
DEPTH = 2
GRID_W = 64
ROWS = SEQ // GRID_W
HG_DIM = 512
HG_HEADS = 4
HG_HEAD_DIM = 128
NA_DIM = 256
NA_HEADS = 4
NA_HEAD_DIM = 64
NA_KH = 8
NA_KW = 16
CONV_DIM = 256
CONV_WIDTH = 31
D_IN = 5 * HG_DIM + 3 * NA_DIM + 2 * CONV_DIM
FFN_DIM = 2816
EPS = 1e-6
NEG_INF = -1e30
LOG2_E = 1.4426950408889634

LANES = 128
SUBLANES = 8
VMEM_LIMIT_BYTES = 60 * 1024 * 1024

HG_CHUNK = 64
HG_CHUNK_LOG2 = HG_CHUNK.bit_length() - 1
assert HG_CHUNK == 1 << HG_CHUNK_LOG2
HG_NCHUNK = SEQ // HG_CHUNK
HG_TILE = 256
HG_OUT_TILE = 512
X_TILE = 512
HG_COLS = 5 * HG_HEAD_DIM
NA_COL0 = 5 * HG_DIM
CV_COL0 = NA_COL0 + 3 * NA_DIM
NA_KEYS = NA_KH * GRID_W
NA_NREL_H = 2 * NA_KH - 1
NA_NREL_W = 2 * NA_KW - 1
RPB_LEAD = GRID_W - NA_KW
RPB_ROWS = 16
assert RPB_ROWS >= NA_NREL_H and RPB_ROWS % SUBLANES == 0
NA_ROWS_PER_STEP = 2
CONV_TILE = 64
CONV_PAD = 16
P_COLS = HG_COLS
assert P_COLS >= 2 * CONV_DIM and P_COLS >= NA_DIM
FFN_TILE = 512
FFN_CHUNK = 256
FFN_NCHUNK = FFN_DIM // FFN_CHUNK

F32 = jnp.float32
BF16 = jnp.bfloat16


def _sigmoid(x):
    return 1.0 / (1.0 + jnp.exp(-x))


def _dot(a, b):
    return jnp.dot(a, b, preferred_element_type=F32)


def _dot_nt(a, b):
    return lax.dot_general(a, b, (((1,), (1,)), ((), ())), preferred_element_type=F32)


def _dot_tn(a, b):
    return lax.dot_general(a, b, (((0,), (0,)), ((), ())), preferred_element_type=F32)


def _split3(x):
    hi = x.astype(BF16)
    r1 = x - hi.astype(F32)
    mid = r1.astype(BF16)
    lo = (r1 - mid.astype(F32)).astype(BF16)
    return hi, mid, lo


def _build_na_bias(rpbp_ref, bias_ref):
    q_i = lax.broadcasted_iota(jnp.int32, (GRID_W, LANES), 0)
    l_i = lax.broadcasted_iota(jnp.int32, (GRID_W, LANES), 1)
    kc = jnp.bitwise_and(l_i, GRID_W - 1)
    q_start = jnp.clip(q_i - NA_KW // 2, 0, GRID_W - NA_KW)
    visible = (kc >= q_start) & (kc < q_start + NA_KW)
    left = l_i < GRID_W
    q_all = jnp.bitwise_and(lax.broadcasted_iota(jnp.int32, (RPB_ROWS * GRID_W, LANES), 0), GRID_W - 1)

    def head_body(h, c):
        rows = rpbp_ref[pl.ds(pl.multiple_of(h * RPB_ROWS, RPB_ROWS), RPB_ROWS), :]
        x = jnp.concatenate([jnp.broadcast_to(rows[ro:ro + 1, :], (GRID_W, LANES)) for ro in range(RPB_ROWS)], axis=0)
        even = pltpu.roll(x, LANES - (GRID_W - 1), axis=1)
        for b in range(GRID_W.bit_length() - 1):
            even = jnp.where(jnp.bitwise_and(q_all, 1 << b) != 0, pltpu.roll(even, 1 << b, axis=1), even)
        odd = pltpu.roll(even, GRID_W, axis=1)
        hp = h // 2
        q0 = pl.multiple_of((h % 2) * GRID_W, GRID_W)
        for var in range(NA_KH):
            for kp in range(NA_KH // 2):
                ro = 2 * kp - var + NA_KH - 1
                tile = jnp.where(left, even[ro * GRID_W:(ro + 1) * GRID_W], odd[(ro + 1) * GRID_W:(ro + 2) * GRID_W])
                bias_ref[hp, var, pl.ds(q0, GRID_W), kp * LANES:(kp + 1) * LANES] = jnp.where(visible, tile * LOG2_E, NEG_INF)
        return c

    lax.fori_loop(0, NA_HEADS, head_body, 0)


def _load_head_weights(win_ref, wh_ref, hd):
    for grp in range(HG_COLS // HG_HEAD_DIM):
        col = grp * HG_DIM + hd * HG_HEAD_DIM
        wh_ref[:, grp * HG_HEAD_DIM:(grp + 1) * HG_HEAD_DIM] = win_ref[:, col:col + HG_HEAD_DIM]


def _mixer_kernel(layer, x_ref, ng_ref, win_ref, hlb_ref, hgn_ref, rpbp_ref, cw_ref, cb_ref, lng_ref, lnb_ref,
                  y_ref, h_scr, p_scr, oi_scr, qa_scr, d_scr, st_scr, dec_scr, upad_scr, bias_ref,
                  wh_scr, qk_scr, stage_scr):
    t = pl.program_id(1)

    @pl.when((pl.program_id(0) == 0) & (t == 0))
    def _():
        _build_na_bias(rpbp_ref, bias_ref)

    @pl.when(t == 0)
    def _():
        _load_head_weights(win_ref, wh_scr.at[0], 0)
        upad_scr[0:CONV_PAD, :] = jnp.zeros((CONV_PAD, CONV_DIM), F32)
        upad_scr[CONV_PAD + SEQ:CONV_PAD + SEQ + CONV_PAD, :] = jnp.zeros((CONV_PAD, CONV_DIM), F32)

    row0 = pl.multiple_of(t * X_TILE, X_TILE)
    xt = x_ref[...]
    ms = jnp.mean(xt * xt, axis=-1, keepdims=True)
    ht = (xt * lax.rsqrt(ms + EPS) * ng_ref[...]).astype(BF16)
    h_scr[pl.ds(row0, X_TILE), :] = ht
    p_scr[0, pl.ds(row0, X_TILE), 0:HG_COLS] = _dot(ht, wh_scr[0])
    cv = _dot(ht, win_ref[:, CV_COL0:CV_COL0 + 2 * CONV_DIM])
    upad_scr[pl.ds(pl.multiple_of(CONV_PAD + row0, SUBLANES), X_TILE), :] = (
        cv[:, 0:CONV_DIM] * _sigmoid(cv[:, CONV_DIM:2 * CONV_DIM]))

    @pl.when(t == SEQ // X_TILE - 1)
    def _():
        _mix_sequence(layer, win_ref, hlb_ref, hgn_ref, cw_ref, cb_ref, lng_ref, lnb_ref, y_ref, h_scr, p_scr, oi_scr,
                      qa_scr, d_scr, st_scr, dec_scr, upad_scr, bias_ref, wh_scr, qk_scr, stage_scr)


def _mix_sequence(layer, win_ref, hlb_ref, hgn_ref, cw_ref, cb_ref, lng_ref, lnb_ref, y_scr, h_scr, p_scr, oi_scr,
                  qa_scr, d_scr, st_scr, dec_scr, upad_scr, bias_ref, wh_scr, qk_scr, stage_scr):
    p_bufs = (p_scr.at[0], p_scr.at[1])
    wh_bufs = (wh_scr.at[0], wh_scr.at[1])
    p_na = p_bufs[HG_HEADS % 2]

    def load_head_weights(hd):
        _load_head_weights(win_ref, wh_bufs[hd % 2], hd)

    hlb = [hlb_ref[d] for d in range(DEPTH)]
    hmax = functools.reduce(jnp.maximum, hlb)
    hexp = [jnp.exp(v - hmax) for v in hlb]
    hsum = functools.reduce(lambda a, b: a + b, hexp)
    hsm = [e / hsum for e in hexp]
    lbs = functools.reduce(lambda a, b: a + b, hsm[:layer + 1]) - hsm[0]

    L = HG_CHUNK
    TL = HG_TILE
    cpt = TL // L
    row_c = lax.broadcasted_iota(jnp.int32, (TL, TL), 0)
    col_c = lax.broadcasted_iota(jnp.int32, (TL, TL), 1)
    same_chunk = jnp.right_shift(row_c, HG_CHUNK_LOG2) == jnp.right_shift(col_c, HG_CHUNK_LOG2)
    lower = same_chunk & (row_c >= col_c)
    upper = same_chunk & (col_c >= row_c)
    lower_b = jnp.where(lower, 1.0, 0.0).astype(BF16)
    upper_b = jnp.where(upper, 1.0, 0.0).astype(BF16)

    def chunk_rows(a, row):
        return [a[c * L + row:c * L + row + 1, :] for c in range(cpt)]

    def spread(rows):
        return jnp.concatenate([jnp.broadcast_to(rw, (L, LANES)) for rw in rows], axis=0)

    scale = NA_HEAD_DIM ** -0.5 * LOG2_E

    def na_project_tile(r, rows):
        res = _dot(h_scr[pl.ds(r, rows), :], win_ref[:, NA_COL0:NA_COL0 + 3 * NA_DIM])
        p_na[pl.ds(r, rows), 0:NA_DIM] = res[:, 0:NA_DIM] * scale
        h_scr[pl.ds(r, rows), 0:2 * NA_DIM] = res[:, NA_DIM:3 * NA_DIM].astype(BF16)

    def conv_tile(r):
        off = CONV_PAD - CONV_WIDTH // 2
        groups = -(-CONV_WIDTH // SUBLANES)
        win_rows = CONV_TILE + 2 * CONV_PAD
        halves = []
        for c0 in range(0, CONV_DIM, LANES):
            win = upad_scr[pl.ds(r, win_rows), c0:c0 + LANES]
            acc = jnp.zeros((CONV_TILE, LANES), F32) + cb_ref[:, c0:c0 + LANES]
            shifted = win
            for s in range(SUBLANES):
                step = off if s == 0 else 1
                if step:
                    shifted = pltpu.roll(shifted, win_rows - step, axis=0)
                for a in range(groups):
                    j = SUBLANES * a + s
                    if j < CONV_WIDTH:
                        acc = acc + shifted[SUBLANES * a:SUBLANES * a + CONV_TILE, :] * cw_ref[j:j + 1, c0:c0 + LANES]
            halves.append(acc)
        acc = jnp.concatenate(halves, axis=1)
        mu = jnp.mean(acc, axis=-1, keepdims=True)
        d = acc - mu
        var = jnp.mean(d * d, axis=-1, keepdims=True)
        u = d * lax.rsqrt(var + EPS) * lng_ref[...] + lnb_ref[...]
        y_scr[pl.ds(r, CONV_TILE), HG_DIM + NA_DIM:D_MODEL] = (u * _sigmoid(u)).astype(BF16)

    assert HG_HEADS * (SEQ // HG_TILE) == SEQ // CONV_TILE
    for hd in range(HG_HEADS):
        sl = slice(hd * HG_HEAD_DIM, (hd + 1) * HG_HEAD_DIM)
        pc = p_bufs[hd % 2]
        if hd + 1 < HG_HEADS:
            load_head_weights(hd + 1)
        directions = (
            (HG_HEAD_DIM, lbs[0:1, sl], lower, lower_b, L // 2 - 1, L - 1),
            (2 * HG_HEAD_DIM, lbs[1:2, sl], upper, upper_b, L // 2, 0),
        )

        def tile_start(i):
            return i * TL if isinstance(i, int) else pl.multiple_of(i * TL, TL)

        def hg_scale(i):
            r = tile_start(i)
            q_raw = pc[pl.ds(r, TL), 0:HG_HEAD_DIM]
            q = q_raw * _sigmoid(q_raw)
            gates = []
            for zcol, lb, mask, cum_op, ref_row, last_row in directions:
                z = pc[pl.ds(r, TL), zcol:zcol + HG_HEAD_DIM]
                t = jnp.exp(-jnp.abs(z))
                big = 1.0 / (1.0 + t)
                small = t * big
                nonneg = z >= 0.0
                f = lb + (1.0 - lb) * jnp.where(nonneg, big, small)
                k = (1.0 - lb) * jnp.where(nonneg, small, big)
                gates.append((k, _dot(cum_op, jnp.concatenate(_split3(jnp.log(f)), axis=1))))
            for d, ((zcol, lb, mask, cum_op, ref_row, last_row), (k, c3)) in enumerate(zip(directions, gates)):
                cum = c3[:, 0:LANES] + c3[:, LANES:2 * LANES] + c3[:, 2 * LANES:3 * LANES]
                ref_rows = chunk_rows(cum, ref_row)
                last_rows = chunk_rows(cum, last_row)
                rel = cum - spread(ref_rows)
                q_in = q * jnp.exp(rel)
                k_in = k * jnp.exp(-rel)
                qa_scr[d, pl.ds(r, TL), :] = (q_in * spread([jnp.exp(rw) for rw in ref_rows])).astype(BF16)
                k_d = (k_in * spread([jnp.exp(lr - rr) for lr, rr in zip(last_rows, ref_rows)])).astype(BF16)
                for cc in range(cpt):
                    dec_scr[d, i * cpt + cc] = jnp.broadcast_to(jnp.exp(last_rows[cc]), (SUBLANES, LANES))
                stage_scr[d, 0] = q_in.astype(BF16)
                stage_scr[d, 1] = k_in.astype(BF16)
                stage_scr[d, 2] = k_d

        def hg_contract(i):
            r = tile_start(i)
            vb = pc[pl.ds(r, TL), 3 * HG_HEAD_DIM:4 * HG_HEAD_DIM].astype(BF16)
            qk = [_dot_nt(stage_scr[d, 0], stage_scr[d, 1]) for d in range(len(directions))]
            masked = [jnp.where(mask, s, 0.0) for (zcol, lb, mask, cum_op, ref_row, last_row), s in zip(directions, qk)]
            oi_scr[pl.ds(r, TL), :] = _dot(functools.reduce(lambda a, b: a + b, masked).astype(BF16), vb)
            k_d = jnp.concatenate([stage_scr[d, 2] for d in range(len(directions))], axis=1)
            for cc in range(cpt):
                inc = _dot_tn(vb[cc * L:(cc + 1) * L], k_d[cc * L:(cc + 1) * L])
                for d in range(len(directions)):
                    d_scr[d, i * cpt + cc] = inc[:, d * HG_HEAD_DIM:(d + 1) * HG_HEAD_DIM]

        def side_work(i, hd=hd):
            r = tile_start(i)
            if hd + 1 < HG_HEADS:
                p_bufs[(hd + 1) % 2][pl.ds(r, TL), 0:HG_COLS] = _dot(h_scr[pl.ds(r, TL), :], wh_bufs[(hd + 1) % 2][...])
            else:
                na_project_tile(r, TL)
            slot = hd * (SEQ // TL) + i
            conv_tile(slot * CONV_TILE if isinstance(slot, int) else pl.multiple_of(slot * CONV_TILE, CONV_TILE))

        def hg_local_body(i, c):
            hg_contract(i)
            hg_scale(i + 1)
            side_work(i + 1)
            return c

        hg_scale(0)
        side_work(0)
        lax.fori_loop(0, SEQ // TL - 1, hg_local_body, 0)
        hg_contract(SEQ // TL - 1)

        def hg_scan_body(n, carry):
            s_f, s_b = carry
            m = HG_NCHUNK - 1 - n
            st_scr[0, n] = s_f.astype(BF16)
            st_scr[1, m] = s_b.astype(BF16)
            s_f = s_f * dec_scr[0, n][0:1, :] + d_scr[0, n]
            s_b = s_b * dec_scr[1, m][0:1, :] + d_scr[1, m]
            return s_f, s_b

        s_zero = jnp.zeros((HG_HEAD_DIM, HG_HEAD_DIM), F32)
        lax.fori_loop(0, HG_NCHUNK, hg_scan_body, (s_zero, s_zero))

        def hg_out_body(i, c):
            r = pl.multiple_of(i * HG_OUT_TILE, HG_OUT_TILE)
            inter = []
            for cc in range(HG_OUT_TILE // L):
                n = i * (HG_OUT_TILE // L) + cc
                rc = pl.multiple_of(r + cc * L, L)
                qa = jnp.concatenate([qa_scr[0, pl.ds(rc, L), :], qa_scr[1, pl.ds(rc, L), :]], axis=1)
                st = jnp.concatenate([st_scr[0, n], st_scr[1, n]], axis=1)
                inter.append(_dot_nt(qa, st))
            o = oi_scr[pl.ds(r, HG_OUT_TILE), :] + jnp.concatenate(inter, axis=0)
            o = o * lax.rsqrt(jnp.mean(o * o, axis=-1, keepdims=True) + EPS) * hgn_ref[...]
            gate = pc[pl.ds(r, HG_OUT_TILE), 4 * HG_HEAD_DIM:5 * HG_HEAD_DIM]
            y_scr[pl.ds(r, HG_OUT_TILE), sl] = (o * (gate * _sigmoid(gate))).astype(BF16)
            return c

        lax.fori_loop(0, SEQ // HG_OUT_TILE, hg_out_body, 0, unroll=2)

    lane = lax.broadcasted_iota(jnp.int32, (1, LANES), 1)
    first_half = lane < NA_HEAD_DIM

    n_na_steps = ROWS // NA_ROWS_PER_STEP

    def na_units(step):
        units = []
        for dr in range(NA_ROWS_PER_STEP):
            r = step * NA_ROWS_PER_STEP + dr
            if isinstance(r, int):
                r0 = min(max(r - NA_KH // 2, 0), ROWS - NA_KH)
                qrow, krow = r * GRID_W, r0 * GRID_W
            else:
                r0 = jnp.clip(r - NA_KH // 2, 0, ROWS - NA_KH)
                qrow, krow = pl.multiple_of(r * GRID_W, GRID_W), pl.multiple_of(r0 * GRID_W, GRID_W)
            for hp in range(NA_HEADS // 2):
                units.append((qrow, krow, r - r0, hp))
        return units

    def na_scores(units):
        for u, (qrow, krow, var, hp) in enumerate(units):
            c0 = hp * LANES
            qt = p_na[pl.ds(qrow, GRID_W), c0:c0 + LANES]
            qm = jnp.concatenate([jnp.where(first_half, qt, 0.0), jnp.where(first_half, 0.0, qt)], axis=0).astype(BF16)
            qk_scr[u] = _dot_nt(qm, h_scr[pl.ds(krow, NA_KEYS), c0:c0 + LANES])

    na_scores(na_units(0))

    def na_body(i, c):
        units = na_units(i)
        probs = []
        for u, (qrow, krow, var, hp) in enumerate(units):
            bt = bias_ref[hp, var]
            s = jnp.where(bt > 0.5 * NEG_INF, qk_scr[u] + bt, NEG_INF)
            probs.append(jnp.exp2(s - jnp.max(s, axis=-1, keepdims=True)).astype(BF16))
        na_scores(na_units(jnp.minimum(i + 1, n_na_steps - 1)))
        ones = jnp.ones((NA_KEYS, LANES), BF16)
        for (qrow, krow, var, hp), e in zip(units, probs):
            c0 = hp * LANES
            v_ones = jnp.concatenate([h_scr[pl.ds(krow, NA_KEYS), NA_DIM + c0:NA_DIM + c0 + LANES], ones], axis=1)
            pv = _dot(e, v_ones)
            o = pv[:, 0:LANES] / pv[:, LANES:2 * LANES]
            y_scr[pl.ds(qrow, GRID_W), HG_DIM + c0:HG_DIM + c0 + LANES] = jnp.where(
                first_half, o[0:GRID_W], o[GRID_W:2 * GRID_W]).astype(BF16)
        return c

    lax.fori_loop(0, n_na_steps, na_body, 0)


def _ffn_kernel(final, x_ref, y_ref, wout_ref, ng_ref, wgu_ref, wd_ref, fg_ref, o_ref, h_scr, acc_scr, x_scr):
    x = x_ref[...] + _dot(y_ref[...], wout_ref[...].astype(BF16))
    x_scr[...] = x
    ms = jnp.mean(x * x, axis=-1, keepdims=True)
    h_scr[...] = (x * lax.rsqrt(ms + EPS) * ng_ref[...]).astype(BF16)
    for c in range(FFN_NCHUNK):
        gt = _dot(h_scr[...], wgu_ref[:, FFN_CHUNK * c:FFN_CHUNK * (c + 1)].astype(BF16))
        up = _dot(h_scr[...], wgu_ref[:, FFN_DIM + FFN_CHUNK * c:FFN_DIM + FFN_CHUNK * (c + 1)].astype(BF16))
        a = (gt * _sigmoid(gt) * up).astype(BF16)
        d = _dot(a, wd_ref[FFN_CHUNK * c:FFN_CHUNK * (c + 1), :].astype(BF16))
        if c == 0:
            acc_scr[...] = d
        else:
            acc_scr[...] += d
    y = x_scr[...] + acc_scr[...]
    if final:
        y = y * lax.rsqrt(jnp.mean(y * y, axis=-1, keepdims=True) + EPS) * fg_ref[...]
    o_ref[...] = y


def _resident(shape):
    nd = len(shape)
    return pl.BlockSpec(shape, lambda *_: (0,) * nd, pipeline_mode=pl.Buffered(1))


def _layer_slice(layer, shape):
    nd = len(shape)
    return pl.BlockSpec((None,) + tuple(shape[1:]), lambda *_: (layer,) + (0,) * (nd - 1), pipeline_mode=pl.Buffered(1))


def _mixer_call(layer, x, ng, win, hlb, hgn, rpbp, cw, cb, lng, lnb):
    batch = x.shape[0]
    stacked = (ng, win, None, hgn, rpbp, cw, cb, lng, lnb)
    return pl.pallas_call(
        functools.partial(_mixer_kernel, layer),
        grid=(batch, SEQ // X_TILE),
        in_specs=[pl.BlockSpec((None, X_TILE, D_MODEL), lambda b, t: (b, t, 0))]
        + [_resident(hlb.shape) if a is None else _layer_slice(layer, a.shape) for a in stacked],
        out_specs=pl.BlockSpec((None, SEQ, D_MODEL), lambda b, t: (b, 0, 0)),
        out_shape=jax.ShapeDtypeStruct(x.shape, BF16),
        scratch_shapes=[
            pltpu.VMEM((SEQ, D_MODEL), BF16),
            pltpu.VMEM((2, SEQ, P_COLS), F32),
            pltpu.VMEM((SEQ, HG_HEAD_DIM), F32),
            pltpu.VMEM((2, SEQ, HG_HEAD_DIM), BF16),
            pltpu.VMEM((2, HG_NCHUNK, HG_HEAD_DIM, HG_HEAD_DIM), F32),
            pltpu.VMEM((2, HG_NCHUNK, HG_HEAD_DIM, HG_HEAD_DIM), BF16),
            pltpu.VMEM((2, HG_NCHUNK, SUBLANES, LANES), F32),
            pltpu.VMEM((SEQ + 2 * CONV_PAD, CONV_DIM), F32),
            pltpu.VMEM((NA_HEADS // 2, NA_KH, 2 * GRID_W, NA_KEYS), F32),
            pltpu.VMEM((2, D_MODEL, HG_COLS), BF16),
            pltpu.VMEM((NA_ROWS_PER_STEP * NA_HEADS // 2, 2 * GRID_W, NA_KEYS), F32),
            pltpu.VMEM((2, 3, HG_TILE, HG_HEAD_DIM), BF16),
        ],
        compiler_params=pltpu.CompilerParams(dimension_semantics=("arbitrary", "arbitrary"),
                                             vmem_limit_bytes=VMEM_LIMIT_BYTES),
        name=f"mixer_l{layer}",
    )(x, ng, win, hlb, hgn, rpbp, cw, cb, lng, lnb)


def _ffn_call(layer, final, x2d, y2d, wout, ng, wgu, wd, fg):
    n_tok = x2d.shape[0]
    tile = pl.BlockSpec((FFN_TILE, D_MODEL), lambda i: (i, 0))
    return pl.pallas_call(
        functools.partial(_ffn_kernel, final),
        grid=(n_tok // FFN_TILE,),
        in_specs=[tile, tile] + [_layer_slice(layer, a.shape) for a in (wout, ng, wgu, wd)] + [_resident(fg.shape)],
        out_specs=tile,
        out_shape=jax.ShapeDtypeStruct(x2d.shape, x2d.dtype),
        scratch_shapes=[pltpu.VMEM((FFN_TILE, D_MODEL), BF16), pltpu.VMEM((FFN_TILE, D_MODEL), F32),
                        pltpu.VMEM((FFN_TILE, D_MODEL), F32)],
        compiler_params=pltpu.CompilerParams(dimension_semantics=("arbitrary",), vmem_limit_bytes=VMEM_LIMIT_BYTES),
        name=f"ffn_l{layer}",
    )(x2d, y2d, wout, ng, wgu, wd, fg)


def _pad_rpb(rpb):
    v = jnp.pad(rpb, ((0, 0), (0, 0), (0, RPB_ROWS - NA_NREL_H), (RPB_LEAD, LANES - RPB_LEAD - NA_NREL_W)))
    return v.reshape(DEPTH, NA_HEADS * RPB_ROWS, LANES)


def kernel(x, mix_norm_g, w_in, hg_lower_bounds, hg_norm_g, na_rpb, conv_w, conv_b, conv_ln_g, conv_ln_b, w_out,
           ffn_norm_g, w_gate_up, w_down, final_norm_g):
    batch, seq, d = x.shape
    assert (seq, d) == (SEQ, D_MODEL) and w_in.shape == (DEPTH, D_MODEL, D_IN)
    row = lambda p: p.astype(F32).reshape(DEPTH, 1, p.shape[-1])
    win = w_in.astype(BF16)
    wout, wgu, wd = (w.astype(F32) for w in (w_out, w_gate_up, w_down))
    hlb = hg_lower_bounds.astype(F32)
    rpbp = _pad_rpb(na_rpb.astype(F32))
    mix_g, hgn, cb, lng, lnb, ffn_g = (row(p) for p in (mix_norm_g, hg_norm_g, conv_b, conv_ln_g, conv_ln_b, ffn_norm_g))
    fg = final_norm_g.astype(F32).reshape(1, D_MODEL)
    for l in range(DEPTH):
        y = _mixer_call(l, x, mix_g, win, hlb, hgn, rpbp, conv_w.astype(F32), cb, lng, lnb)
        x2d = _ffn_call(l, l == DEPTH - 1, x.reshape(batch * seq, d), y.reshape(batch * seq, d), wout, ffn_g, wgu, wd, fg)
        x = x2d.reshape(batch, seq, d)
    return x
```

```python
import functools

import jax
import jax.numpy as jnp
from jax import lax
from jax.experimental import pallas as pl
from jax.experimental.pallas import tpu as pltpu

D_MODEL = 1024
SEQ = 2048
DEPTH = 2
GRID_W = 64
ROWS = SEQ // GRID_W
HG_DIM = 512
HG_HEADS = 4
HG_HEAD_DIM = 128
NA_DIM = 256
NA_HEADS = 4
NA_HEAD_DIM = 64
NA_KH = 8
NA_KW = 16
CONV_DIM = 256
CONV_WIDTH = 31
D_IN = 5 * HG_DIM + 3 * NA_DIM + 2 * CONV_DIM
FFN_DIM = 2816
EPS = 1e-6
NEG_INF = -1e30

LANES = 128
SUBLANES = 8
VMEM_LIMIT_BYTES = 60 * 1024 * 1024

HG_CHUNK = 64
HG_CHUNK_LOG2 = HG_CHUNK.bit_length() - 1
assert HG_CHUNK == 1 << HG_CHUNK_LOG2
HG_NCHUNK = SEQ // HG_CHUNK
HG_TILE = 256
HG_OUT_TILE = 512
X_TILE = 512
HG_COLS = 5 * HG_HEAD_DIM
NA_COL0 = 5 * HG_DIM
CV_COL0 = NA_COL0 + 3 * NA_DIM
NA_KEYS = NA_KH * GRID_W
NA_NREL_H = 2 * NA_KH - 1
NA_NREL_W = 2 * NA_KW - 1
RPB_LEAD = GRID_W - NA_KW
RPB_ROWS = 16
assert RPB_ROWS >= NA_NREL_H and RPB_ROWS % SUBLANES == 0
NA_ROWS_PER_STEP = 2
CONV_TILE = 64
CONV_PAD = 16
P_COLS = HG_COLS
assert P_COLS >= 2 * CONV_DIM and P_COLS >= NA_DIM
FFN_TILE = 512
FFN_CHUNK = 256
FFN_NCHUNK = FFN_DIM // FFN_CHUNK

F32 = jnp.float32
BF16 = jnp.bfloat16


def _sigmoid(x):
    return 1.0 / (1.0 + jnp.exp(-x))


def _dot(a, b):
    return jnp.dot(a, b, preferred_element_type=F32)


def _dot_nt(a, b):
    return lax.dot_general(a, b, (((1,), (1,)), ((), ())), preferred_element_type=F32)


def _dot_tn(a, b):
    return lax.dot_general(a, b, (((0,), (0,)), ((), ())), preferred_element_type=F32)


def _split3(x):
    hi = x.astype(BF16)
    r1 = x - hi.astype(F32)
    mid = r1.astype(BF16)
    lo = (r1 - mid.astype(F32)).astype(BF16)
    return hi, mid, lo


def _build_na_bias(rpbp_ref, bias_ref):
    q_i = lax.broadcasted_iota(jnp.int32, (GRID_W, LANES), 0)
    l_i = lax.broadcasted_iota(jnp.int32, (GRID_W, LANES), 1)
    kc = jnp.bitwise_and(l_i, GRID_W - 1)
    q_start = jnp.clip(q_i - NA_KW // 2, 0, GRID_W - NA_KW)
    visible = (kc >= q_start) & (kc < q_start + NA_KW)
    left = l_i < GRID_W
    q_all = jnp.bitwise_and(lax.broadcasted_iota(jnp.int32, (RPB_ROWS * GRID_W, LANES), 0), GRID_W - 1)

    def head_body(h, c):
        rows = rpbp_ref[pl.ds(pl.multiple_of(h * RPB_ROWS, RPB_ROWS), RPB_ROWS), :]
        x = jnp.concatenate([jnp.broadcast_to(rows[ro:ro + 1, :], (GRID_W, LANES)) for ro in range(RPB_ROWS)], axis=0)
        even = pltpu.roll(x, LANES - (GRID_W - 1), axis=1)
        for b in range(GRID_W.bit_length() - 1):
            even = jnp.where(jnp.bitwise_and(q_all, 1 << b) != 0, pltpu.roll(even, 1 << b, axis=1), even)
        odd = pltpu.roll(even, GRID_W, axis=1)
        hp = h // 2
        q0 = pl.multiple_of((h % 2) * GRID_W, GRID_W)
        for var in range(NA_KH):
            for kp in range(NA_KH // 2):
                ro = 2 * kp - var + NA_KH - 1
                tile = jnp.where(left, even[ro * GRID_W:(ro + 1) * GRID_W], odd[(ro + 1) * GRID_W:(ro + 2) * GRID_W])
                bias_ref[hp, var, pl.ds(q0, GRID_W), kp * LANES:(kp + 1) * LANES] = jnp.where(visible, tile, NEG_INF)
        return c

    lax.fori_loop(0, NA_HEADS, head_body, 0)


def _load_head_weights(win_ref, wh_ref, hd):
    for grp in range(HG_COLS // HG_HEAD_DIM):
        col = grp * HG_DIM + hd * HG_HEAD_DIM
        wh_ref[:, grp * HG_HEAD_DIM:(grp + 1) * HG_HEAD_DIM] = win_ref[:, col:col + HG_HEAD_DIM]


def _mixer_kernel(layer, x_ref, ng_ref, win_ref, hlb_ref, hgn_ref, rpbp_ref, cw_ref, cb_ref, lng_ref, lnb_ref,
                  y_ref, h_scr, p_scr, oi_scr, qa_scr, d_scr, st_scr, dec_scr, upad_scr, bias_ref,
                  wh_scr, qk_scr, stage_scr):
    t = pl.program_id(1)

    @pl.when((pl.program_id(0) == 0) & (t == 0))
    def _():
        _build_na_bias(rpbp_ref, bias_ref)

    @pl.when(t == 0)
    def _():
        _load_head_weights(win_ref, wh_scr.at[0], 0)
        upad_scr[0:CONV_PAD, :] = jnp.zeros((CONV_PAD, CONV_DIM), F32)
        upad_scr[CONV_PAD + SEQ:CONV_PAD + SEQ + CONV_PAD, :] = jnp.zeros((CONV_PAD, CONV_DIM), F32)

    row0 = pl.multiple_of(t * X_TILE, X_TILE)
    xt = x_ref[...]
    ms = jnp.mean(xt * xt, axis=-1, keepdims=True)
    ht = (xt * lax.rsqrt(ms + EPS) * ng_ref[...]).astype(BF16)
    h_scr[pl.ds(row0, X_TILE), :] = ht
    p_scr[0, pl.ds(row0, X_TILE), 0:HG_COLS] = _dot(ht, wh_scr[0])
    cv = _dot(ht, win_ref[:, CV_COL0:CV_COL0 + 2 * CONV_DIM])
    upad_scr[pl.ds(pl.multiple_of(CONV_PAD + row0, SUBLANES), X_TILE), :] = (
        cv[:, 0:CONV_DIM] * _sigmoid(cv[:, CONV_DIM:2 * CONV_DIM]))

    @pl.when(t == SEQ // X_TILE - 1)
    def _():
        _mix_sequence(layer, win_ref, hlb_ref, hgn_ref, cw_ref, cb_ref, lng_ref, lnb_ref, y_ref, h_scr, p_scr, oi_scr,
                      qa_scr, d_scr, st_scr, dec_scr, upad_scr, bias_ref, wh_scr, qk_scr, stage_scr)


def _mix_sequence(layer, win_ref, hlb_ref, hgn_ref, cw_ref, cb_ref, lng_ref, lnb_ref, y_scr, h_scr, p_scr, oi_scr,
                  qa_scr, d_scr, st_scr, dec_scr, upad_scr, bias_ref, wh_scr, qk_scr, stage_scr):
    p_bufs = (p_scr.at[0], p_scr.at[1])
    wh_bufs = (wh_scr.at[0], wh_scr.at[1])
    p_na = p_bufs[HG_HEADS % 2]

    def load_head_weights(hd):
        _load_head_weights(win_ref, wh_bufs[hd % 2], hd)

    hlb = [hlb_ref[d] for d in range(DEPTH)]
    hmax = functools.reduce(jnp.maximum, hlb)
    hexp = [jnp.exp(v - hmax) for v in hlb]
    hsum = functools.reduce(lambda a, b: a + b, hexp)
    hsm = [e / hsum for e in hexp]
    lbs = functools.reduce(lambda a, b: a + b, hsm[:layer + 1]) - hsm[0]

    L = HG_CHUNK
    TL = HG_TILE
    cpt = TL // L
    row_c = lax.broadcasted_iota(jnp.int32, (TL, TL), 0)
    col_c = lax.broadcasted_iota(jnp.int32, (TL, TL), 1)
    same_chunk = jnp.right_shift(row_c, HG_CHUNK_LOG2) == jnp.right_shift(col_c, HG_CHUNK_LOG2)
    lower = same_chunk & (row_c >= col_c)
    upper = same_chunk & (col_c >= row_c)
    lower_b = jnp.where(lower, 1.0, 0.0).astype(BF16)
    upper_b = jnp.where(upper, 1.0, 0.0).astype(BF16)

    def chunk_rows(a, row):
        return [a[c * L + row:c * L + row + 1, :] for c in range(cpt)]

    def spread(rows):
        return jnp.concatenate([jnp.broadcast_to(rw, (L, LANES)) for rw in rows], axis=0)

    scale = NA_HEAD_DIM ** -0.5

    def na_project_tile(r, rows):
        res = _dot(h_scr[pl.ds(r, rows), :], win_ref[:, NA_COL0:NA_COL0 + 3 * NA_DIM])
        p_na[pl.ds(r, rows), 0:NA_DIM] = res[:, 0:NA_DIM] * scale
        h_scr[pl.ds(r, rows), 0:2 * NA_DIM] = res[:, NA_DIM:3 * NA_DIM].astype(BF16)

    def conv_tile(r):
        off = CONV_PAD - CONV_WIDTH // 2
        groups = -(-CONV_WIDTH // SUBLANES)
        win_rows = CONV_TILE + 2 * CONV_PAD
        halves = []
        for c0 in range(0, CONV_DIM, LANES):
            win = upad_scr[pl.ds(r, win_rows), c0:c0 + LANES]
            acc = jnp.zeros((CONV_TILE, LANES), F32) + cb_ref[:, c0:c0 + LANES]
            shifted = win
            for s in range(SUBLANES):
                step = off if s == 0 else 1
                if step:
                    shifted = pltpu.roll(shifted, win_rows - step, axis=0)
                for a in range(groups):
                    j = SUBLANES * a + s
                    if j < CONV_WIDTH:
                        acc = acc + shifted[SUBLANES * a:SUBLANES * a + CONV_TILE, :] * cw_ref[j:j + 1, c0:c0 + LANES]
            halves.append(acc)
        acc = jnp.concatenate(halves, axis=1)
        mu = jnp.mean(acc, axis=-1, keepdims=True)
        d = acc - mu
        var = jnp.mean(d * d, axis=-1, keepdims=True)
        u = d * lax.rsqrt(var + EPS) * lng_ref[...] + lnb_ref[...]
        y_scr[pl.ds(r, CONV_TILE), HG_DIM + NA_DIM:D_MODEL] = (u * _sigmoid(u)).astype(BF16)

    assert HG_HEADS * (SEQ // HG_TILE) == SEQ // CONV_TILE
    for hd in range(HG_HEADS):
        sl = slice(hd * HG_HEAD_DIM, (hd + 1) * HG_HEAD_DIM)
        pc = p_bufs[hd % 2]
        if hd + 1 < HG_HEADS:
            load_head_weights(hd + 1)
        directions = (
            (HG_HEAD_DIM, lbs[0:1, sl], lower, lower_b, L // 2 - 1, L - 1),
            (2 * HG_HEAD_DIM, lbs[1:2, sl], upper, upper_b, L // 2, 0),
        )

        def tile_start(i):
            return i * TL if isinstance(i, int) else pl.multiple_of(i * TL, TL)

        def hg_scale(i):
            r = tile_start(i)
            q_raw = pc[pl.ds(r, TL), 0:HG_HEAD_DIM]
            q = q_raw * _sigmoid(q_raw)
            gates = []
            for zcol, lb, mask, cum_op, ref_row, last_row in directions:
                z = pc[pl.ds(r, TL), zcol:zcol + HG_HEAD_DIM]
                t = jnp.exp(-jnp.abs(z))
                big = 1.0 / (1.0 + t)
                small = t * big
                nonneg = z >= 0.0
                f = lb + (1.0 - lb) * jnp.where(nonneg, big, small)
                k = (1.0 - lb) * jnp.where(nonneg, small, big)
                gates.append((k, _dot(cum_op, jnp.concatenate(_split3(jnp.log(f)), axis=1))))
            for d, ((zcol, lb, mask, cum_op, ref_row, last_row), (k, c3)) in enumerate(zip(directions, gates)):
                cum = c3[:, 0:LANES] + c3[:, LANES:2 * LANES] + c3[:, 2 * LANES:3 * LANES]
                ref_rows = chunk_rows(cum, ref_row)
                last_rows = chunk_rows(cum, last_row)
                rel = cum - spread(ref_rows)
                q_in = q * jnp.exp(rel)
                k_in = k * jnp.exp(-rel)
                qa_scr[d, pl.ds(r, TL), :] = (q_in * spread([jnp.exp(rw) for rw in ref_rows])).astype(BF16)
                k_d = (k_in * spread([jnp.exp(lr - rr) for lr, rr in zip(last_rows, ref_rows)])).astype(BF16)
                for cc in range(cpt):
                    dec_scr[d, i * cpt + cc] = jnp.broadcast_to(jnp.exp(last_rows[cc]), (SUBLANES, LANES))
                stage_scr[d, 0] = q_in.astype(BF16)
                stage_scr[d, 1] = k_in.astype(BF16)
                stage_scr[d, 2] = k_d

        def hg_contract(i):
            r = tile_start(i)
            vb = pc[pl.ds(r, TL), 3 * HG_HEAD_DIM:4 * HG_HEAD_DIM].astype(BF16)
            qk = [_dot_nt(stage_scr[d, 0], stage_scr[d, 1]) for d in range(len(directions))]
            masked = [jnp.where(mask, s, 0.0) for (zcol, lb, mask, cum_op, ref_row, last_row), s in zip(directions, qk)]
            oi_scr[pl.ds(r, TL), :] = _dot(functools.reduce(lambda a, b: a + b, masked).astype(BF16), vb)
            k_d = jnp.concatenate([stage_scr[d, 2] for d in range(len(directions))], axis=1)
            for cc in range(cpt):
                inc = _dot_tn(vb[cc * L:(cc + 1) * L], k_d[cc * L:(cc + 1) * L])
                for d in range(len(directions)):
                    d_scr[d, i * cpt + cc] = inc[:, d * HG_HEAD_DIM:(d + 1) * HG_HEAD_DIM]

        def side_work(i, hd=hd):
            r = tile_start(i)
            if hd + 1 < HG_HEADS:
                p_bufs[(hd + 1) % 2][pl.ds(r, TL), 0:HG_COLS] = _dot(h_scr[pl.ds(r, TL), :], wh_bufs[(hd + 1) % 2][...])
            else:
                na_project_tile(r, TL)
            slot = hd * (SEQ // TL) + i
            conv_tile(slot * CONV_TILE if isinstance(slot, int) else pl.multiple_of(slot * CONV_TILE, CONV_TILE))

        def hg_local_body(i, c):
            hg_contract(i)
            hg_scale(i + 1)
            side_work(i + 1)
            return c

        hg_scale(0)
        side_work(0)
        lax.fori_loop(0, SEQ // TL - 1, hg_local_body, 0)
        hg_contract(SEQ // TL - 1)

        def hg_scan_body(n, carry):
            s_f, s_b = carry
            m = HG_NCHUNK - 1 - n
            st_scr[0, n] = s_f.astype(BF16)
            st_scr[1, m] = s_b.astype(BF16)
            s_f = s_f * dec_scr[0, n][0:1, :] + d_scr[0, n]
            s_b = s_b * dec_scr[1, m][0:1, :] + d_scr[1, m]
            return s_f, s_b

        s_zero = jnp.zeros((HG_HEAD_DIM, HG_HEAD_DIM), F32)
        lax.fori_loop(0, HG_NCHUNK, hg_scan_body, (s_zero, s_zero))

        def hg_out_body(i, c):
            r = pl.multiple_of(i * HG_OUT_TILE, HG_OUT_TILE)
            inter = []
            for cc in range(HG_OUT_TILE // L):
                n = i * (HG_OUT_TILE // L) + cc
                rc = pl.multiple_of(r + cc * L, L)
                qa = jnp.concatenate([qa_scr[0, pl.ds(rc, L), :], qa_scr[1, pl.ds(rc, L), :]], axis=1)
                st = jnp.concatenate([st_scr[0, n], st_scr[1, n]], axis=1)
                inter.append(_dot_nt(qa, st))
            o = oi_scr[pl.ds(r, HG_OUT_TILE), :] + jnp.concatenate(inter, axis=0)
            o = o * lax.rsqrt(jnp.mean(o * o, axis=-1, keepdims=True) + EPS) * hgn_ref[...]
            gate = pc[pl.ds(r, HG_OUT_TILE), 4 * HG_HEAD_DIM:5 * HG_HEAD_DIM]
            y_scr[pl.ds(r, HG_OUT_TILE), sl] = (o * (gate * _sigmoid(gate))).astype(BF16)
            return c

        lax.fori_loop(0, SEQ // HG_OUT_TILE, hg_out_body, 0, unroll=2)

    lane = lax.broadcasted_iota(jnp.int32, (1, LANES), 1)
    first_half = lane < NA_HEAD_DIM

    n_na_steps = ROWS // NA_ROWS_PER_STEP

    def na_units(step):
        units = []
        for dr in range(NA_ROWS_PER_STEP):
            r = step * NA_ROWS_PER_STEP + dr
            if isinstance(r, int):
                r0 = min(max(r - NA_KH // 2, 0), ROWS - NA_KH)
                qrow, krow = r * GRID_W, r0 * GRID_W
            else:
                r0 = jnp.clip(r - NA_KH // 2, 0, ROWS - NA_KH)
                qrow, krow = pl.multiple_of(r * GRID_W, GRID_W), pl.multiple_of(r0 * GRID_W, GRID_W)
            for hp in range(NA_HEADS // 2):
                units.append((qrow, krow, r - r0, hp))
        return units

    def na_scores(units):
        for u, (qrow, krow, var, hp) in enumerate(units):
            c0 = hp * LANES
            qt = p_na[pl.ds(qrow, GRID_W), c0:c0 + LANES]
            qm = jnp.concatenate([jnp.where(first_half, qt, 0.0), jnp.where(first_half, 0.0, qt)], axis=0).astype(BF16)
            qk_scr[u] = _dot_nt(qm, h_scr[pl.ds(krow, NA_KEYS), c0:c0 + LANES])

    na_scores(na_units(0))

    def na_body(i, c):
        units = na_units(i)
        probs = []
        for u, (qrow, krow, var, hp) in enumerate(units):
            bt = bias_ref[hp, var]
            s = jnp.where(bt > 0.5 * NEG_INF, qk_scr[u] + bt, NEG_INF)
            probs.append(jnp.exp(s - jnp.max(s, axis=-1, keepdims=True)).astype(BF16))
        na_scores(na_units(jnp.minimum(i + 1, n_na_steps - 1)))
        ones = jnp.ones((NA_KEYS, LANES), BF16)
        for (qrow, krow, var, hp), e in zip(units, probs):
            c0 = hp * LANES
            v_ones = jnp.concatenate([h_scr[pl.ds(krow, NA_KEYS), NA_DIM + c0:NA_DIM + c0 + LANES], ones], axis=1)
            pv = _dot(e, v_ones)
            o = pv[:, 0:LANES] / pv[:, LANES:2 * LANES]
            y_scr[pl.ds(qrow, GRID_W), HG_DIM + c0:HG_DIM + c0 + LANES] = jnp.where(
                first_half, o[0:GRID_W], o[GRID_W:2 * GRID_W]).astype(BF16)
        return c

    lax.fori_loop(0, n_na_steps, na_body, 0)


def _ffn_kernel(final, x_ref, y_ref, wout_ref, ng_ref, wgu_ref, wd_ref, fg_ref, o_ref, h_scr, acc_scr, x_scr):
    x = x_ref[...] + _dot(y_ref[...], wout_ref[...].astype(BF16))
    x_scr[...] = x
    ms = jnp.mean(x * x, axis=-1, keepdims=True)
    h_scr[...] = (x * lax.rsqrt(ms + EPS) * ng_ref[...]).astype(BF16)
    for c in range(FFN_NCHUNK):
        gt = _dot(h_scr[...], wgu_ref[:, FFN_CHUNK * c:FFN_CHUNK * (c + 1)].astype(BF16))
        up = _dot(h_scr[...], wgu_ref[:, FFN_DIM + FFN_CHUNK * c:FFN_DIM + FFN_CHUNK * (c + 1)].astype(BF16))
        a = (gt * _sigmoid(gt) * up).astype(BF16)
        d = _dot(a, wd_ref[FFN_CHUNK * c:FFN_CHUNK * (c + 1), :].astype(BF16))
        if c == 0:
            acc_scr[...] = d
        else:
            acc_scr[...] += d
    y = x_scr[...] + acc_scr[...]
    if final:
        y = y * lax.rsqrt(jnp.mean(y * y, axis=-1, keepdims=True) + EPS) * fg_ref[...]
    o_ref[...] = y


def _resident(shape):
    nd = len(shape)
    return pl.BlockSpec(shape, lambda *_: (0,) * nd, pipeline_mode=pl.Buffered(1))


def _layer_slice(layer, shape):
    nd = len(shape)
    return pl.BlockSpec((None,) + tuple(shape[1:]), lambda *_: (layer,) + (0,) * (nd - 1), pipeline_mode=pl.Buffered(1))


def _mixer_call(layer, x, ng, win, hlb, hgn, rpbp, cw, cb, lng, lnb):
    batch = x.shape[0]
    stacked = (ng, win, None, hgn, rpbp, cw, cb, lng, lnb)
    return pl.pallas_call(
        functools.partial(_mixer_kernel, layer),
        grid=(batch, SEQ // X_TILE),
        in_specs=[pl.BlockSpec((None, X_TILE, D_MODEL), lambda b, t: (b, t, 0))]
        + [_resident(hlb.shape) if a is None else _layer_slice(layer, a.shape) for a in stacked],
        out_specs=pl.BlockSpec((None, SEQ, D_MODEL), lambda b, t: (b, 0, 0)),
        out_shape=jax.ShapeDtypeStruct(x.shape, BF16),
        scratch_shapes=[
            pltpu.VMEM((SEQ, D_MODEL), BF16),
            pltpu.VMEM((2, SEQ, P_COLS), F32),
            pltpu.VMEM((SEQ, HG_HEAD_DIM), F32),
            pltpu.VMEM((2, SEQ, HG_HEAD_DIM), BF16),
            pltpu.VMEM((2, HG_NCHUNK, HG_HEAD_DIM, HG_HEAD_DIM), F32),
            pltpu.VMEM((2, HG_NCHUNK, HG_HEAD_DIM, HG_HEAD_DIM), BF16),
            pltpu.VMEM((2, HG_NCHUNK, SUBLANES, LANES), F32),
            pltpu.VMEM((SEQ + 2 * CONV_PAD, CONV_DIM), F32),
            pltpu.VMEM((NA_HEADS // 2, NA_KH, 2 * GRID_W, NA_KEYS), F32),
            pltpu.VMEM((2, D_MODEL, HG_COLS), BF16),
            pltpu.VMEM((NA_ROWS_PER_STEP * NA_HEADS // 2, 2 * GRID_W, NA_KEYS), F32),
            pltpu.VMEM((2, 3, HG_TILE, HG_HEAD_DIM), BF16),
        ],
        compiler_params=pltpu.CompilerParams(dimension_semantics=("arbitrary", "arbitrary"),
                                             vmem_limit_bytes=VMEM_LIMIT_BYTES),
        name=f"mixer_l{layer}",
    )(x, ng, win, hlb, hgn, rpbp, cw, cb, lng, lnb)


def _ffn_call(layer, final, x2d, y2d, wout, ng, wgu, wd, fg):
    n_tok = x2d.shape[0]
    tile = pl.BlockSpec((FFN_TILE, D_MODEL), lambda i: (i, 0))
    return pl.pallas_call(
        functools.partial(_ffn_kernel, final),
        grid=(n_tok // FFN_TILE,),
        in_specs=[tile, tile] + [_layer_slice(layer, a.shape) for a in (wout, ng, wgu, wd)] + [_resident(fg.shape)],
        out_specs=tile,
        out_shape=jax.ShapeDtypeStruct(x2d.shape, x2d.dtype),
        scratch_shapes=[pltpu.VMEM((FFN_TILE, D_MODEL), BF16), pltpu.VMEM((FFN_TILE, D_MODEL), F32),
                        pltpu.VMEM((FFN_TILE, D_MODEL), F32)],
        compiler_params=pltpu.CompilerParams(dimension_semantics=("arbitrary",), vmem_limit_bytes=VMEM_LIMIT_BYTES),
        name=f"ffn_l{layer}",
    )(x2d, y2d, wout, ng, wgu, wd, fg)


def _pad_rpb(rpb):
    v = jnp.pad(rpb, ((0, 0), (0, 0), (0, RPB_ROWS - NA_NREL_H), (RPB_LEAD, LANES - RPB_LEAD - NA_NREL_W)))
    return v.reshape(DEPTH, NA_HEADS * RPB_ROWS, LANES)


def kernel(x, mix_norm_g, w_in, hg_lower_bounds, hg_norm_g, na_rpb, conv_w, conv_b, conv_ln_g, conv_ln_b, w_out,
           ffn_norm_g, w_gate_up, w_down, final_norm_g):
    batch, seq, d = x.shape
    assert (seq, d) == (SEQ, D_MODEL) and w_in.shape == (DEPTH, D_MODEL, D_IN)
    row = lambda p: p.astype(F32).reshape(DEPTH, 1, p.shape[-1])
    win = w_in.astype(BF16)
    wout, wgu, wd = (w.astype(F32) for w in (w_out, w_gate_up, w_down))
    hlb = hg_lower_bounds.astype(F32)
    rpbp = _pad_rpb(na_rpb.astype(F32))
    mix_g, hgn, cb, lng, lnb, ffn_g = (row(p) for p in (mix_norm_g, hg_norm_g, conv_b, conv_ln_g, conv_ln_b, ffn_norm_g))
    fg = final_norm_g.astype(F32).reshape(1, D_MODEL)
    for l in range(DEPTH):
        y = _mixer_call(l, x, mix_g, win, hlb, hgn, rpbp, conv_w.astype(F32), cb, lng, lnb)
        x2d = _ffn_call(l, l == DEPTH - 1, x.reshape(batch * seq, d), y.reshape(batch * seq, d), wout, ffn_g, wgu, wd, fg)
        x = x2d.reshape(batch, seq, d)
    return x
```

```python
import functools

import jax
import jax.numpy as jnp
from jax import lax
from jax.experimental import pallas as pl
from jax.experimental.pallas import tpu as pltpu

D_MODEL = 1024
SEQ = 2048
DEPTH = 2
GRID_W = 64
ROWS = SEQ // GRID_W
HG_DIM = 512
HG_HEADS = 4
HG_HEAD_DIM = 128
NA_DIM = 256
NA_HEADS = 4
NA_HEAD_DIM = 64
NA_KH = 8
NA_KW = 16
CONV_DIM = 256
CONV_WIDTH = 31
D_IN = 5 * HG_DIM + 3 * NA_DIM + 2 * CONV_DIM
FFN_DIM = 2816
EPS = 1e-6
NEG_INF = -1e30

LANES = 128
SUBLANES = 8
VMEM_LIMIT_BYTES = 60 * 1024 * 1024

HG_CHUNK = 64
HG_CHUNK_LOG2 = HG_CHUNK.bit_length() - 1
assert HG_CHUNK == 1 << HG_CHUNK_LOG2
HG_NCHUNK = SEQ // HG_CHUNK
HG_TILE = 256
HG_OUT_TILE = 1024
X_TILE = 512
HG_COLS = 5 * HG_HEAD_DIM
NA_COL0 = 5 * HG_DIM
CV_COL0 = NA_COL0 + 3 * NA_DIM
NA_KEYS = NA_KH * GRID_W
NA_NREL_H = 2 * NA_KH - 1
NA_NREL_W = 2 * NA_KW - 1
RPB_LEAD = GRID_W - NA_KW
RPB_ROWS = 16
assert RPB_ROWS >= NA_NREL_H and RPB_ROWS % SUBLANES == 0
NA_ROWS_PER_STEP = 2
CONV_TILE = 64
CONV_PAD = 16
P_COLS = HG_COLS
assert P_COLS >= 2 * CONV_DIM and P_COLS >= NA_DIM
FFN_TILE = 512
FFN_CHUNK = 256
FFN_NCHUNK = FFN_DIM // FFN_CHUNK

F32 = jnp.float32
BF16 = jnp.bfloat16


def _sigmoid(x):
    return 1.0 / (1.0 + jnp.exp(-x))


def _dot(a, b):
    return jnp.dot(a, b, preferred_element_type=F32)


def _dot_nt(a, b):
    return lax.dot_general(a, b, (((1,), (1,)), ((), ())), preferred_element_type=F32)


def _dot_tn(a, b):
    return lax.dot_general(a, b, (((0,), (0,)), ((), ())), preferred_element_type=F32)


def _split3(x):
    hi = x.astype(BF16)
    r1 = x - hi.astype(F32)
    mid = r1.astype(BF16)
    lo = (r1 - mid.astype(F32)).astype(BF16)
    return hi, mid, lo


def _build_na_bias(rpbp_ref, bias_ref):
    q_i = lax.broadcasted_iota(jnp.int32, (GRID_W, LANES), 0)
    l_i = lax.broadcasted_iota(jnp.int32, (GRID_W, LANES), 1)
    kc = jnp.bitwise_and(l_i, GRID_W - 1)
    q_start = jnp.clip(q_i - NA_KW // 2, 0, GRID_W - NA_KW)
    visible = (kc >= q_start) & (kc < q_start + NA_KW)
    left = l_i < GRID_W
    q_all = jnp.bitwise_and(lax.broadcasted_iota(jnp.int32, (RPB_ROWS * GRID_W, LANES), 0), GRID_W - 1)

    def head_body(h, c):
        rows = rpbp_ref[pl.ds(pl.multiple_of(h * RPB_ROWS, RPB_ROWS), RPB_ROWS), :]
        x = jnp.concatenate([jnp.broadcast_to(rows[ro:ro + 1, :], (GRID_W, LANES)) for ro in range(RPB_ROWS)], axis=0)
        even = pltpu.roll(x, LANES - (GRID_W - 1), axis=1)
        for b in range(GRID_W.bit_length() - 1):
            even = jnp.where(jnp.bitwise_and(q_all, 1 << b) != 0, pltpu.roll(even, 1 << b, axis=1), even)
        odd = pltpu.roll(even, GRID_W, axis=1)
        hp = h // 2
        q0 = pl.multiple_of((h % 2) * GRID_W, GRID_W)
        for var in range(NA_KH):
            for kp in range(NA_KH // 2):
                ro = 2 * kp - var + NA_KH - 1
                tile = jnp.where(left, even[ro * GRID_W:(ro + 1) * GRID_W], odd[(ro + 1) * GRID_W:(ro + 2) * GRID_W])
                bias_ref[hp, var, pl.ds(q0, GRID_W), kp * LANES:(kp + 1) * LANES] = jnp.where(visible, tile, NEG_INF)
        return c

    lax.fori_loop(0, NA_HEADS, head_body, 0)


def _load_head_weights(win_ref, wh_ref, hd):
    for grp in range(HG_COLS // HG_HEAD_DIM):
        col = grp * HG_DIM + hd * HG_HEAD_DIM
        wh_ref[:, grp * HG_HEAD_DIM:(grp + 1) * HG_HEAD_DIM] = win_ref[:, col:col + HG_HEAD_DIM]


def _mixer_kernel(layer, x_ref, ng_ref, win_ref, hlb_ref, hgn_ref, rpbp_ref, cw_ref, cb_ref, lng_ref, lnb_ref,
                  y_ref, h_scr, p_scr, oi_scr, qa_scr, d_scr, st_scr, dec_scr, upad_scr, bias_ref,
                  wh_scr, qk_scr, stage_scr):
    t = pl.program_id(1)

    @pl.when((pl.program_id(0) == 0) & (t == 0))
    def _():
        _build_na_bias(rpbp_ref, bias_ref)

    @pl.when(t == 0)
    def _():
        _load_head_weights(win_ref, wh_scr.at[0], 0)
        upad_scr[0:CONV_PAD, :] = jnp.zeros((CONV_PAD, CONV_DIM), F32)
        upad_scr[CONV_PAD + SEQ:CONV_PAD + SEQ + CONV_PAD, :] = jnp.zeros((CONV_PAD, CONV_DIM), F32)

    row0 = pl.multiple_of(t * X_TILE, X_TILE)
    xt = x_ref[...]
    ms = jnp.mean(xt * xt, axis=-1, keepdims=True)
    ht = (xt * lax.rsqrt(ms + EPS) * ng_ref[...]).astype(BF16)
    h_scr[pl.ds(row0, X_TILE), :] = ht
    p_scr[0, pl.ds(row0, X_TILE), 0:HG_COLS] = _dot(ht, wh_scr[0])
    cv = _dot(ht, win_ref[:, CV_COL0:CV_COL0 + 2 * CONV_DIM])
    upad_scr[pl.ds(pl.multiple_of(CONV_PAD + row0, SUBLANES), X_TILE), :] = (
        cv[:, 0:CONV_DIM] * _sigmoid(cv[:, CONV_DIM:2 * CONV_DIM]))

    @pl.when(t == SEQ // X_TILE - 1)
    def _():
        _mix_sequence(layer, win_ref, hlb_ref, hgn_ref, cw_ref, cb_ref, lng_ref, lnb_ref, y_ref, h_scr, p_scr, oi_scr,
                      qa_scr, d_scr, st_scr, dec_scr, upad_scr, bias_ref, wh_scr, qk_scr, stage_scr)


def _mix_sequence(layer, win_ref, hlb_ref, hgn_ref, cw_ref, cb_ref, lng_ref, lnb_ref, y_scr, h_scr, p_scr, oi_scr,
                  qa_scr, d_scr, st_scr, dec_scr, upad_scr, bias_ref, wh_scr, qk_scr, stage_scr):
    p_bufs = (p_scr.at[0], p_scr.at[1])
    wh_bufs = (wh_scr.at[0], wh_scr.at[1])
    p_na = p_bufs[HG_HEADS % 2]

    def load_head_weights(hd):
        _load_head_weights(win_ref, wh_bufs[hd % 2], hd)

    hlb = [hlb_ref[d] for d in range(DEPTH)]
    hmax = functools.reduce(jnp.maximum, hlb)
    hexp = [jnp.exp(v - hmax) for v in hlb]
    hsum = functools.reduce(lambda a, b: a + b, hexp)
    hsm = [e / hsum for e in hexp]
    lbs = functools.reduce(lambda a, b: a + b, hsm[:layer + 1]) - hsm[0]

    L = HG_CHUNK
    TL = HG_TILE
    cpt = TL // L
    row_c = lax.broadcasted_iota(jnp.int32, (TL, TL), 0)
    col_c = lax.broadcasted_iota(jnp.int32, (TL, TL), 1)
    same_chunk = jnp.right_shift(row_c, HG_CHUNK_LOG2) == jnp.right_shift(col_c, HG_CHUNK_LOG2)
    lower = same_chunk & (row_c >= col_c)
    upper = same_chunk & (col_c >= row_c)
    lower_b = jnp.where(lower, 1.0, 0.0).astype(BF16)
    upper_b = jnp.where(upper, 1.0, 0.0).astype(BF16)

    def chunk_rows(a, row):
        return [a[c * L + row:c * L + row + 1, :] for c in range(cpt)]

    def spread(rows):
        return jnp.concatenate([jnp.broadcast_to(rw, (L, LANES)) for rw in rows], axis=0)

    scale = NA_HEAD_DIM ** -0.5

    def na_project_tile(r, rows):
        res = _dot(h_scr[pl.ds(r, rows), :], win_ref[:, NA_COL0:NA_COL0 + 3 * NA_DIM])
        p_na[pl.ds(r, rows), 0:NA_DIM] = res[:, 0:NA_DIM] * scale
        h_scr[pl.ds(r, rows), 0:2 * NA_DIM] = res[:, NA_DIM:3 * NA_DIM].astype(BF16)

    def conv_tile(r):
        off = CONV_PAD - CONV_WIDTH // 2
        groups = -(-CONV_WIDTH // SUBLANES)
        win_rows = CONV_TILE + 2 * CONV_PAD
        halves = []
        for c0 in range(0, CONV_DIM, LANES):
            win = upad_scr[pl.ds(r, win_rows), c0:c0 + LANES]
            acc = jnp.zeros((CONV_TILE, LANES), F32) + cb_ref[:, c0:c0 + LANES]
            shifted = win
            for s in range(SUBLANES):
                step = off if s == 0 else 1
                if step:
                    shifted = pltpu.roll(shifted, win_rows - step, axis=0)
                for a in range(groups):
                    j = SUBLANES * a + s
                    if j < CONV_WIDTH:
                        acc = acc + shifted[SUBLANES * a:SUBLANES * a + CONV_TILE, :] * cw_ref[j:j + 1, c0:c0 + LANES]
            halves.append(acc)
        acc = jnp.concatenate(halves, axis=1)
        mu = jnp.mean(acc, axis=-1, keepdims=True)
        d = acc - mu
        var = jnp.mean(d * d, axis=-1, keepdims=True)
        u = d * lax.rsqrt(var + EPS) * lng_ref[...] + lnb_ref[...]
        y_scr[pl.ds(r, CONV_TILE), HG_DIM + NA_DIM:D_MODEL] = (u * _sigmoid(u)).astype(BF16)

    assert HG_HEADS * (SEQ // HG_TILE) == SEQ // CONV_TILE
    for hd in range(HG_HEADS):
        sl = slice(hd * HG_HEAD_DIM, (hd + 1) * HG_HEAD_DIM)
        pc = p_bufs[hd % 2]
        if hd + 1 < HG_HEADS:
            load_head_weights(hd + 1)
        directions = (
            (HG_HEAD_DIM, lbs[0:1, sl], lower, lower_b, L // 2 - 1, L - 1),
            (2 * HG_HEAD_DIM, lbs[1:2, sl], upper, upper_b, L // 2, 0),
        )

        def tile_start(i):
            return i * TL if isinstance(i, int) else pl.multiple_of(i * TL, TL)

        def hg_scale(i):
            r = tile_start(i)
            q_raw = pc[pl.ds(r, TL), 0:HG_HEAD_DIM]
            q = q_raw * _sigmoid(q_raw)
            gates = []
            for zcol, lb, mask, cum_op, ref_row, last_row in directions:
                z = pc[pl.ds(r, TL), zcol:zcol + HG_HEAD_DIM]
                t = jnp.exp(-jnp.abs(z))
                big = 1.0 / (1.0 + t)
                small = t * big
                nonneg = z >= 0.0
                f = lb + (1.0 - lb) * jnp.where(nonneg, big, small)
                k = (1.0 - lb) * jnp.where(nonneg, small, big)
                gates.append((k, _dot(cum_op, jnp.concatenate(_split3(jnp.log(f)), axis=1))))
            for d, ((zcol, lb, mask, cum_op, ref_row, last_row), (k, c3)) in enumerate(zip(directions, gates)):
                cum = c3[:, 0:LANES] + c3[:, LANES:2 * LANES] + c3[:, 2 * LANES:3 * LANES]
                ref_rows = chunk_rows(cum, ref_row)
                last_rows = chunk_rows(cum, last_row)
                rel = cum - spread(ref_rows)
                q_in = q * jnp.exp(rel)
                k_in = k * jnp.exp(-rel)
                qa_scr[d, pl.ds(r, TL), :] = (q_in * spread([jnp.exp(rw) for rw in ref_rows])).astype(BF16)
                k_d = (k_in * spread([jnp.exp(lr - rr) for lr, rr in zip(last_rows, ref_rows)])).astype(BF16)
                for cc in range(cpt):
                    dec_scr[d, i * cpt + cc] = jnp.broadcast_to(jnp.exp(last_rows[cc]), (SUBLANES, LANES))
                stage_scr[d, 0] = q_in.astype(BF16)
                stage_scr[d, 1] = k_in.astype(BF16)
                stage_scr[d, 2] = k_d

        def hg_contract(i):
            r = tile_start(i)
            vb = pc[pl.ds(r, TL), 3 * HG_HEAD_DIM:4 * HG_HEAD_DIM].astype(BF16)
            qk = [_dot_nt(stage_scr[d, 0], stage_scr[d, 1]) for d in range(len(directions))]
            masked = [jnp.where(mask, s, 0.0) for (zcol, lb, mask, cum_op, ref_row, last_row), s in zip(directions, qk)]
            oi_scr[pl.ds(r, TL), :] = _dot(functools.reduce(lambda a, b: a + b, masked).astype(BF16), vb)
            k_d = jnp.concatenate([stage_scr[d, 2] for d in range(len(directions))], axis=1)
            for cc in range(cpt):
                inc = _dot_tn(vb[cc * L:(cc + 1) * L], k_d[cc * L:(cc + 1) * L])
                for d in range(len(directions)):
                    d_scr[d, i * cpt + cc] = inc[:, d * HG_HEAD_DIM:(d + 1) * HG_HEAD_DIM]

        def side_work(i, hd=hd):
            r = tile_start(i)
            if hd + 1 < HG_HEADS:
                p_bufs[(hd + 1) % 2][pl.ds(r, TL), 0:HG_COLS] = _dot(h_scr[pl.ds(r, TL), :], wh_bufs[(hd + 1) % 2][...])
            else:
                na_project_tile(r, TL)
            slot = hd * (SEQ // TL) + i
            conv_tile(slot * CONV_TILE if isinstance(slot, int) else pl.multiple_of(slot * CONV_TILE, CONV_TILE))

        def hg_local_body(i, c):
            hg_contract(i)
            hg_scale(i + 1)
            side_work(i + 1)
            return c

        hg_scale(0)
        side_work(0)
        lax.fori_loop(0, SEQ // TL - 1, hg_local_body, 0)
        hg_contract(SEQ // TL - 1)

        def hg_scan_body(n, carry):
            s_f, s_b = carry
            m = HG_NCHUNK - 1 - n
            st_scr[0, n] = s_f.astype(BF16)
            st_scr[1, m] = s_b.astype(BF16)
            s_f = s_f * dec_scr[0, n][0:1, :] + d_scr[0, n]
            s_b = s_b * dec_scr[1, m][0:1, :] + d_scr[1, m]
            return s_f, s_b

        s_zero = jnp.zeros((HG_HEAD_DIM, HG_HEAD_DIM), F32)
        lax.fori_loop(0, HG_NCHUNK, hg_scan_body, (s_zero, s_zero), unroll=HG_NCHUNK)

        def hg_out_body(i, c):
            r = pl.multiple_of(i * HG_OUT_TILE, HG_OUT_TILE)
            inter = []
            for cc in range(HG_OUT_TILE // L):
                n = i * (HG_OUT_TILE // L) + cc
                rc = pl.multiple_of(r + cc * L, L)
                qa = jnp.concatenate([qa_scr[0, pl.ds(rc, L), :], qa_scr[1, pl.ds(rc, L), :]], axis=1)
                st = jnp.concatenate([st_scr[0, n], st_scr[1, n]], axis=1)
                inter.append(_dot_nt(qa, st))
            o = oi_scr[pl.ds(r, HG_OUT_TILE), :] + jnp.concatenate(inter, axis=0)
            o = o * lax.rsqrt(jnp.mean(o * o, axis=-1, keepdims=True) + EPS) * hgn_ref[...]
            gate = pc[pl.ds(r, HG_OUT_TILE), 4 * HG_HEAD_DIM:5 * HG_HEAD_DIM]
            y_scr[pl.ds(r, HG_OUT_TILE), sl] = (o * (gate * _sigmoid(gate))).astype(BF16)
            return c

        lax.fori_loop(0, SEQ // HG_OUT_TILE, hg_out_body, 0, unroll=2)

    lane = lax.broadcasted_iota(jnp.int32, (1, LANES), 1)
    first_half = lane < NA_HEAD_DIM

    n_na_steps = ROWS // NA_ROWS_PER_STEP

    def na_units(step):
        units = []
        for dr in range(NA_ROWS_PER_STEP):
            r = step * NA_ROWS_PER_STEP + dr
            if isinstance(r, int):
                r0 = min(max(r - NA_KH // 2, 0), ROWS - NA_KH)
                qrow, krow = r * GRID_W, r0 * GRID_W
            else:
                r0 = jnp.clip(r - NA_KH // 2, 0, ROWS - NA_KH)
                qrow, krow = pl.multiple_of(r * GRID_W, GRID_W), pl.multiple_of(r0 * GRID_W, GRID_W)
            for hp in range(NA_HEADS // 2):
                units.append((qrow, krow, r - r0, hp))
        return units

    def na_scores(units):
        for u, (qrow, krow, var, hp) in enumerate(units):
            c0 = hp * LANES
            qt = p_na[pl.ds(qrow, GRID_W), c0:c0 + LANES]
            qm = jnp.concatenate([jnp.where(first_half, qt, 0.0), jnp.where(first_half, 0.0, qt)], axis=0).astype(BF16)
            qk_scr[u] = _dot_nt(qm, h_scr[pl.ds(krow, NA_KEYS), c0:c0 + LANES])

    na_scores(na_units(0))

    def na_body(i, c):
        units = na_units(i)
        probs = []
        for u, (qrow, krow, var, hp) in enumerate(units):
            bt = bias_ref[hp, var]
            s = jnp.where(bt > 0.5 * NEG_INF, qk_scr[u] + bt, NEG_INF)
            probs.append(jnp.exp(s - jnp.max(s, axis=-1, keepdims=True)).astype(BF16))
        na_scores(na_units(jnp.minimum(i + 1, n_na_steps - 1)))
        ones = jnp.ones((NA_KEYS, LANES), BF16)
        for (qrow, krow, var, hp), e in zip(units, probs):
            c0 = hp * LANES
            v_ones = jnp.concatenate([h_scr[pl.ds(krow, NA_KEYS), NA_DIM + c0:NA_DIM + c0 + LANES], ones], axis=1)
            pv = _dot(e, v_ones)
            o = pv[:, 0:LANES] / pv[:, LANES:2 * LANES]
            y_scr[pl.ds(qrow, GRID_W), HG_DIM + c0:HG_DIM + c0 + LANES] = jnp.where(
                first_half, o[0:GRID_W], o[GRID_W:2 * GRID_W]).astype(BF16)
        return c

    lax.fori_loop(0, n_na_steps, na_body, 0)


def _ffn_kernel(final, x_ref, y_ref, wout_ref, ng_ref, wgu_ref, wd_ref, fg_ref, o_ref, h_scr, acc_scr, x_scr):
    x = x_ref[...] + _dot(y_ref[...], wout_ref[...].astype(BF16))
    x_scr[...] = x
    ms = jnp.mean(x * x, axis=-1, keepdims=True)
    h_scr[...] = (x * lax.rsqrt(ms + EPS) * ng_ref[...]).astype(BF16)
    for c in range(FFN_NCHUNK):
        gt = _dot(h_scr[...], wgu_ref[:, FFN_CHUNK * c:FFN_CHUNK * (c + 1)].astype(BF16))
        up = _dot(h_scr[...], wgu_ref[:, FFN_DIM + FFN_CHUNK * c:FFN_DIM + FFN_CHUNK * (c + 1)].astype(BF16))
        a = (gt * _sigmoid(gt) * up).astype(BF16)
        d = _dot(a, wd_ref[FFN_CHUNK * c:FFN_CHUNK * (c + 1), :].astype(BF16))
        if c == 0:
            acc_scr[...] = d
        else:
            acc_scr[...] += d
    y = x_scr[...] + acc_scr[...]
    if final:
        y = y * lax.rsqrt(jnp.mean(y * y, axis=-1, keepdims=True) + EPS) * fg_ref[...]
    o_ref[...] = y


def _resident(shape):
    nd = len(shape)
    return pl.BlockSpec(shape, lambda *_: (0,) * nd, pipeline_mode=pl.Buffered(1))


def _layer_slice(layer, shape):
    nd = len(shape)
    return pl.BlockSpec((None,) + tuple(shape[1:]), lambda *_: (layer,) + (0,) * (nd - 1), pipeline_mode=pl.Buffered(1))


def _mixer_call(layer, x, ng, win, hlb, hgn, rpbp, cw, cb, lng, lnb):
    batch = x.shape[0]
    stacked = (ng, win, None, hgn, rpbp, cw, cb, lng, lnb)
    return pl.pallas_call(
        functools.partial(_mixer_kernel, layer),
        grid=(batch, SEQ // X_TILE),
        in_specs=[pl.BlockSpec((None, X_TILE, D_MODEL), lambda b, t: (b, t, 0))]
        + [_resident(hlb.shape) if a is None else _layer_slice(layer, a.shape) for a in stacked],
        out_specs=pl.BlockSpec((None, SEQ, D_MODEL), lambda b, t: (b, 0, 0)),
        out_shape=jax.ShapeDtypeStruct(x.shape, BF16),
        scratch_shapes=[
            pltpu.VMEM((SEQ, D_MODEL), BF16),
            pltpu.VMEM((2, SEQ, P_COLS), F32),
            pltpu.VMEM((SEQ, HG_HEAD_DIM), F32),
            pltpu.VMEM((2, SEQ, HG_HEAD_DIM), BF16),
            pltpu.VMEM((2, HG_NCHUNK, HG_HEAD_DIM, HG_HEAD_DIM), F32),
            pltpu.VMEM((2, HG_NCHUNK, HG_HEAD_DIM, HG_HEAD_DIM), BF16),
            pltpu.VMEM((2, HG_NCHUNK, SUBLANES, LANES), F32),
            pltpu.VMEM((SEQ + 2 * CONV_PAD, CONV_DIM), F32),
            pltpu.VMEM((NA_HEADS // 2, NA_KH, 2 * GRID_W, NA_KEYS), F32),
            pltpu.VMEM((2, D_MODEL, HG_COLS), BF16),
            pltpu.VMEM((NA_ROWS_PER_STEP * NA_HEADS // 2, 2 * GRID_W, NA_KEYS), F32),
            pltpu.VMEM((2, 3, HG_TILE, HG_HEAD_DIM), BF16),
        ],
        compiler_params=pltpu.CompilerParams(dimension_semantics=("arbitrary", "arbitrary"),
                                             vmem_limit_bytes=VMEM_LIMIT_BYTES),
        name=f"mixer_l{layer}",
    )(x, ng, win, hlb, hgn, rpbp, cw, cb, lng, lnb)


def _ffn_call(layer, final, x2d, y2d, wout, ng, wgu, wd, fg):
    n_tok = x2d.shape[0]
    tile = pl.BlockSpec((FFN_TILE, D_MODEL), lambda i: (i, 0))
    return pl.pallas_call(
        functools.partial(_ffn_kernel, final),
        grid=(n_tok // FFN_TILE,),
        in_specs=[tile, tile] + [_layer_slice(layer, a.shape) for a in (wout, ng, wgu, wd)] + [_resident(fg.shape)],
        out_specs=tile,
        out_shape=jax.ShapeDtypeStruct(x2d.shape, x2d.dtype),
        scratch_shapes=[pltpu.VMEM((FFN_TILE, D_MODEL), BF16), pltpu.VMEM((FFN_TILE, D_MODEL), F32),
                        pltpu.VMEM((FFN_TILE, D_MODEL), F32)],
        compiler_params=pltpu.CompilerParams(dimension_semantics=("arbitrary",), vmem_limit_bytes=VMEM_LIMIT_BYTES),
        name=f"ffn_l{layer}",
    )(x2d, y2d, wout, ng, wgu, wd, fg)


def _pad_rpb(rpb):
    v = jnp.pad(rpb, ((0, 0), (0, 0), (0, RPB_ROWS - NA_NREL_H), (RPB_LEAD, LANES - RPB_LEAD - NA_NREL_W)))
    return v.reshape(DEPTH, NA_HEADS * RPB_ROWS, LANES)


def kernel(x, mix_norm_g, w_in, hg_lower_bounds, hg_norm_g, na_rpb, conv_w, conv_b, conv_ln_g, conv_ln_b, w_out,
           ffn_norm_g, w_gate_up, w_down, final_norm_g):
    batch, seq, d = x.shape
    assert (seq, d) == (SEQ, D_MODEL) and w_in.shape == (DEPTH, D_MODEL, D_IN)
    row = lambda p: p.astype(F32).reshape(DEPTH, 1, p.shape[-1])
    win = w_in.astype(BF16)
    wout, wgu, wd = (w.astype(F32) for w in (w_out, w_gate_up, w_down))
    hlb = hg_lower_bounds.astype(F32)
    rpbp = _pad_rpb(na_rpb.astype(F32))
    mix_g, hgn, cb, lng, lnb, ffn_g = (row(p) for p in (mix_norm_g, hg_norm_g, conv_b, conv_ln_g, conv_ln_b, ffn_norm_g))
    fg = final_norm_g.astype(F32).reshape(1, D_MODEL)
    for l in range(DEPTH):
        y = _mixer_call(l, x, mix_g, win, hlb, hgn, rpbp, conv_w.astype(F32), cb, lng, lnb)
        x2d = _ffn_call(l, l == DEPTH - 1, x.reshape(batch * seq, d), y.reshape(batch * seq, d), wout, ffn_g, wgu, wd, fg)
        x = x2d.reshape(batch, seq, d)
    return x
```

```python
import functools

import jax
import jax.numpy as jnp
from jax import lax
from jax.experimental import pallas as pl
from jax.experimental.pallas import tpu as pltpu

D_MODEL = 1024
SEQ = 2048
DEPTH = 2
GRID_W = 64
ROWS = SEQ // GRID_W
HG_DIM = 512
HG_HEADS = 4
HG_HEAD_DIM = 128
NA_DIM = 256
NA_HEADS = 4
NA_HEAD_DIM = 64
NA_KH = 8
NA_KW = 16
CONV_DIM = 256
CONV_WIDTH = 31
D_IN = 5 * HG_DIM + 3 * NA_DIM + 2 * CONV_DIM
FFN_DIM = 2816
EPS = 1e-6
NEG_INF = -1e30

LANES = 128
SUBLANES = 8
VMEM_LIMIT_BYTES = 60 * 1024 * 1024

HG_CHUNK = 64
HG_CHUNK_LOG2 = HG_CHUNK.bit_length() - 1
assert HG_CHUNK == 1 << HG_CHUNK_LOG2
HG_NCHUNK = SEQ // HG_CHUNK
HG_TILE = 256
HG_OUT_TILE = 1024
X_TILE = 512
HG_COLS = 5 * HG_HEAD_DIM
NA_COL0 = 5 * HG_DIM
CV_COL0 = NA_COL0 + 3 * NA_DIM
NA_KEYS = NA_KH * GRID_W
NA_NREL_H = 2 * NA_KH - 1
NA_NREL_W = 2 * NA_KW - 1
RPB_LEAD = GRID_W - NA_KW
RPB_ROWS = 16
assert RPB_ROWS >= NA_NREL_H and RPB_ROWS % SUBLANES == 0
NA_ROWS_PER_STEP = 2
CONV_TILE = 64
CONV_PAD = 16
P_COLS = HG_COLS
assert P_COLS >= 2 * CONV_DIM and P_COLS >= NA_DIM
FFN_TILE = 512
FFN_CHUNK = 256
FFN_NCHUNK = FFN_DIM // FFN_CHUNK

F32 = jnp.float32
BF16 = jnp.bfloat16


def _sigmoid(x):
    return 1.0 / (1.0 + jnp.exp(-x))


def _dot(a, b):
    return jnp.dot(a, b, preferred_element_type=F32)


def _dot_nt(a, b):
    return lax.dot_general(a, b, (((1,), (1,)), ((), ())), preferred_element_type=F32)


def _dot_tn(a, b):
    return lax.dot_general(a, b, (((0,), (0,)), ((), ())), preferred_element_type=F32)


def _split3(x):
    hi = x.astype(BF16)
    r1 = x - hi.astype(F32)
    mid = r1.astype(BF16)
    lo = (r1 - mid.astype(F32)).astype(BF16)
    return hi, mid, lo


def _build_na_bias(rpbp_ref, bias_ref):
    q_i = lax.broadcasted_iota(jnp.int32, (GRID_W, LANES), 0)
    l_i = lax.broadcasted_iota(jnp.int32, (GRID_W, LANES), 1)
    kc = jnp.bitwise_and(l_i, GRID_W - 1)
    q_start = jnp.clip(q_i - NA_KW // 2, 0, GRID_W - NA_KW)
    visible = (kc >= q_start) & (kc < q_start + NA_KW)
    left = l_i < GRID_W
    q_all = jnp.bitwise_and(lax.broadcasted_iota(jnp.int32, (RPB_ROWS * GRID_W, LANES), 0), GRID_W - 1)

    def head_body(h, c):
        rows = rpbp_ref[pl.ds(pl.multiple_of(h * RPB_ROWS, RPB_ROWS), RPB_ROWS), :]
        x = jnp.concatenate([jnp.broadcast_to(rows[ro:ro + 1, :], (GRID_W, LANES)) for ro in range(RPB_ROWS)], axis=0)
        even = pltpu.roll(x, LANES - (GRID_W - 1), axis=1)
        for b in range(GRID_W.bit_length() - 1):
            even = jnp.where(jnp.bitwise_and(q_all, 1 << b) != 0, pltpu.roll(even, 1 << b, axis=1), even)
        odd = pltpu.roll(even, GRID_W, axis=1)
        hp = h // 2
        q0 = pl.multiple_of((h % 2) * GRID_W, GRID_W)
        for var in range(NA_KH):
            for kp in range(NA_KH // 2):
                ro = 2 * kp - var + NA_KH - 1
                tile = jnp.where(left, even[ro * GRID_W:(ro + 1) * GRID_W], odd[(ro + 1) * GRID_W:(ro + 2) * GRID_W])
                bias_ref[hp, var, pl.ds(q0, GRID_W), kp * LANES:(kp + 1) * LANES] = jnp.where(visible, tile, NEG_INF)
        return c

    lax.fori_loop(0, NA_HEADS, head_body, 0)


def _load_head_weights(win_ref, wh_ref, hd):
    for grp in range(HG_COLS // HG_HEAD_DIM):
        col = grp * HG_DIM + hd * HG_HEAD_DIM
        wh_ref[:, grp * HG_HEAD_DIM:(grp + 1) * HG_HEAD_DIM] = win_ref[:, col:col + HG_HEAD_DIM]


def _mixer_kernel(layer, x_ref, ng_ref, win_ref, hlb_ref, hgn_ref, rpbp_ref, cw_ref, cb_ref, lng_ref, lnb_ref,
                  y_ref, h_scr, p_scr, oi_scr, qa_scr, d_scr, st_scr, dec_scr, upad_scr, bias_ref,
                  wh_scr, qk_scr, stage_scr):
    t = pl.program_id(1)

    @pl.when((pl.program_id(0) == 0) & (t == 0))
    def _():
        _build_na_bias(rpbp_ref, bias_ref)

    @pl.when(t == 0)
    def _():
        _load_head_weights(win_ref, wh_scr.at[0], 0)
        upad_scr[0:CONV_PAD, :] = jnp.zeros((CONV_PAD, CONV_DIM), F32)
        upad_scr[CONV_PAD + SEQ:CONV_PAD + SEQ + CONV_PAD, :] = jnp.zeros((CONV_PAD, CONV_DIM), F32)

    row0 = pl.multiple_of(t * X_TILE, X_TILE)
    xt = x_ref[...]
    ms = jnp.mean(xt * xt, axis=-1, keepdims=True)
    ht = (xt * lax.rsqrt(ms + EPS) * ng_ref[...]).astype(BF16)
    h_scr[pl.ds(row0, X_TILE), :] = ht
    p_scr[0, pl.ds(row0, X_TILE), 0:HG_COLS] = _dot(ht, wh_scr[0])
    cv = _dot(ht, win_ref[:, CV_COL0:CV_COL0 + 2 * CONV_DIM])
    upad_scr[pl.ds(pl.multiple_of(CONV_PAD + row0, SUBLANES), X_TILE), :] = (
        cv[:, 0:CONV_DIM] * _sigmoid(cv[:, CONV_DIM:2 * CONV_DIM]))

    @pl.when(t == SEQ // X_TILE - 1)
    def _():
        _mix_sequence(layer, win_ref, hlb_ref, hgn_ref, cw_ref, cb_ref, lng_ref, lnb_ref, y_ref, h_scr, p_scr, oi_scr,
                      qa_scr, d_scr, st_scr, dec_scr, upad_scr, bias_ref, wh_scr, qk_scr, stage_scr)


def _mix_sequence(layer, win_ref, hlb_ref, hgn_ref, cw_ref, cb_ref, lng_ref, lnb_ref, y_scr, h_scr, p_scr, oi_scr,
                  qa_scr, d_scr, st_scr, dec_scr, upad_scr, bias_ref, wh_scr, qk_scr, stage_scr):
    p_bufs = (p_scr.at[0], p_scr.at[1])
    wh_bufs = (wh_scr.at[0], wh_scr.at[1])
    p_na = p_bufs[HG_HEADS % 2]

    def load_head_weights(hd):
        _load_head_weights(win_ref, wh_bufs[hd % 2], hd)

    hlb = [hlb_ref[d] for d in range(DEPTH)]
    hmax = functools.reduce(jnp.maximum, hlb)
    hexp = [jnp.exp(v - hmax) for v in hlb]
    hsum = functools.reduce(lambda a, b: a + b, hexp)
    hsm = [e / hsum for e in hexp]
    lbs = functools.reduce(lambda a, b: a + b, hsm[:layer + 1]) - hsm[0]

    L = HG_CHUNK
    TL = HG_TILE
    cpt = TL // L
    row_c = lax.broadcasted_iota(jnp.int32, (TL, TL), 0)
    col_c = lax.broadcasted_iota(jnp.int32, (TL, TL), 1)
    same_chunk = jnp.right_shift(row_c, HG_CHUNK_LOG2) == jnp.right_shift(col_c, HG_CHUNK_LOG2)
    lower = same_chunk & (row_c >= col_c)
    upper = same_chunk & (col_c >= row_c)
    lower_b = jnp.where(lower, 1.0, 0.0).astype(BF16)
    upper_b = jnp.where(upper, 1.0, 0.0).astype(BF16)

    def chunk_rows(a, row):
        return [a[c * L + row:c * L + row + 1, :] for c in range(cpt)]

    def spread(rows):
        return jnp.concatenate([jnp.broadcast_to(rw, (L, LANES)) for rw in rows], axis=0)

    scale = NA_HEAD_DIM ** -0.5

    def na_project_tile(r, rows):
        res = _dot(h_scr[pl.ds(r, rows), :], win_ref[:, NA_COL0:NA_COL0 + 3 * NA_DIM])
        p_na[pl.ds(r, rows), 0:NA_DIM] = res[:, 0:NA_DIM] * scale
        h_scr[pl.ds(r, rows), 0:2 * NA_DIM] = res[:, NA_DIM:3 * NA_DIM].astype(BF16)

    def conv_tile(r):
        off = CONV_PAD - CONV_WIDTH // 2
        groups = -(-CONV_WIDTH // SUBLANES)
        win_rows = CONV_TILE + 2 * CONV_PAD
        halves = []
        for c0 in range(0, CONV_DIM, LANES):
            win = upad_scr[pl.ds(r, win_rows), c0:c0 + LANES]
            acc = jnp.zeros((CONV_TILE, LANES), F32) + cb_ref[:, c0:c0 + LANES]
            shifted = win
            for s in range(SUBLANES):
                step = off if s == 0 else 1
                if step:
                    shifted = pltpu.roll(shifted, win_rows - step, axis=0)
                for a in range(groups):
                    j = SUBLANES * a + s
                    if j < CONV_WIDTH:
                        acc = acc + shifted[SUBLANES * a:SUBLANES * a + CONV_TILE, :] * cw_ref[j:j + 1, c0:c0 + LANES]
            halves.append(acc)
        acc = jnp.concatenate(halves, axis=1)
        mu = jnp.mean(acc, axis=-1, keepdims=True)
        d = acc - mu
        var = jnp.mean(d * d, axis=-1, keepdims=True)
        u = d * lax.rsqrt(var + EPS) * lng_ref[...] + lnb_ref[...]
        y_scr[pl.ds(r, CONV_TILE), HG_DIM + NA_DIM:D_MODEL] = (u * _sigmoid(u)).astype(BF16)

    assert HG_HEADS * (SEQ // HG_TILE) == SEQ // CONV_TILE
    for hd in range(HG_HEADS):
        sl = slice(hd * HG_HEAD_DIM, (hd + 1) * HG_HEAD_DIM)
        pc = p_bufs[hd % 2]
        if hd + 1 < HG_HEADS:
            load_head_weights(hd + 1)
        directions = (
            (HG_HEAD_DIM, lbs[0:1, sl], lower, lower_b, L // 2 - 1, L - 1),
            (2 * HG_HEAD_DIM, lbs[1:2, sl], upper, upper_b, L // 2, 0),
        )

        def tile_start(i):
            return i * TL if isinstance(i, int) else pl.multiple_of(i * TL, TL)

        def hg_scale(i):
            r = tile_start(i)
            q_raw = pc[pl.ds(r, TL), 0:HG_HEAD_DIM]
            q = q_raw * _sigmoid(q_raw)
            gates = []
            for zcol, lb, mask, cum_op, ref_row, last_row in directions:
                z = pc[pl.ds(r, TL), zcol:zcol + HG_HEAD_DIM]
                t = jnp.exp(-jnp.abs(z))
                big = 1.0 / (1.0 + t)
                small = t * big
                nonneg = z >= 0.0
                f = lb + (1.0 - lb) * jnp.where(nonneg, big, small)
                k = (1.0 - lb) * jnp.where(nonneg, small, big)
                gates.append((k, _dot(cum_op, jnp.concatenate(_split3(jnp.log(f)), axis=1))))
            for d, ((zcol, lb, mask, cum_op, ref_row, last_row), (k, c3)) in enumerate(zip(directions, gates)):
                cum = c3[:, 0:LANES] + c3[:, LANES:2 * LANES] + c3[:, 2 * LANES:3 * LANES]
                ref_rows = chunk_rows(cum, ref_row)
                last_rows = chunk_rows(cum, last_row)
                rel = cum - spread(ref_rows)
                q_in = q * jnp.exp(rel)
                k_in = k * jnp.exp(-rel)
                qa_scr[d, pl.ds(r, TL), :] = (q_in * spread([jnp.exp(rw) for rw in ref_rows])).astype(BF16)
                k_d = (k_in * spread([jnp.exp(lr - rr) for lr, rr in zip(last_rows, ref_rows)])).astype(BF16)
                for cc in range(cpt):
                    dec_scr[d, i * cpt + cc] = jnp.broadcast_to(jnp.exp(last_rows[cc]), (SUBLANES, LANES))
                stage_scr[d, 0] = q_in.astype(BF16)
                stage_scr[d, 1] = k_in.astype(BF16)
                stage_scr[d, 2] = k_d

        def hg_contract(i):
            r = tile_start(i)
            vb = pc[pl.ds(r, TL), 3 * HG_HEAD_DIM:4 * HG_HEAD_DIM].astype(BF16)
            qk = [_dot_nt(stage_scr[d, 0], stage_scr[d, 1]) for d in range(len(directions))]
            masked = [jnp.where(mask, s, 0.0) for (zcol, lb, mask, cum_op, ref_row, last_row), s in zip(directions, qk)]
            oi_scr[pl.ds(r, TL), :] = _dot(functools.reduce(lambda a, b: a + b, masked).astype(BF16), vb)
            k_d = jnp.concatenate([stage_scr[d, 2] for d in range(len(directions))], axis=1)
            for cc in range(cpt):
                inc = _dot_tn(vb[cc * L:(cc + 1) * L], k_d[cc * L:(cc + 1) * L])
                for d in range(len(directions)):
                    d_scr[d, i * cpt + cc] = inc[:, d * HG_HEAD_DIM:(d + 1) * HG_HEAD_DIM]

        def side_work(i, hd=hd):
            r = tile_start(i)
            if hd + 1 < HG_HEADS:
                p_bufs[(hd + 1) % 2][pl.ds(r, TL), 0:HG_COLS] = _dot(h_scr[pl.ds(r, TL), :], wh_bufs[(hd + 1) % 2][...])
            else:
                na_project_tile(r, TL)
            slot = hd * (SEQ // TL) + i
            conv_tile(slot * CONV_TILE if isinstance(slot, int) else pl.multiple_of(slot * CONV_TILE, CONV_TILE))

        def hg_local_body(i, c):
            hg_contract(i)
            hg_scale(i + 1)
            side_work(i + 1)
            return c

        hg_scale(0)
        side_work(0)
        lax.fori_loop(0, SEQ // TL - 1, hg_local_body, 0)
        hg_contract(SEQ // TL - 1)

        def hg_scan_body(n, carry):
            s_f, s_b = carry
            m = HG_NCHUNK - 1 - n
            st_scr[0, n] = s_f.astype(BF16)
            st_scr[1, m] = s_b.astype(BF16)
            s_f = s_f * dec_scr[0, n][0:1, :] + d_scr[0, n]
            s_b = s_b * dec_scr[1, m][0:1, :] + d_scr[1, m]
            return s_f, s_b

        s_zero = jnp.zeros((HG_HEAD_DIM, HG_HEAD_DIM), F32)
        lax.fori_loop(0, HG_NCHUNK, hg_scan_body, (s_zero, s_zero), unroll=HG_NCHUNK)

        def hg_out_body(i, c):
            r = pl.multiple_of(i * HG_OUT_TILE, HG_OUT_TILE)
            inter = []
            for cc in range(HG_OUT_TILE // L):
                n = i * (HG_OUT_TILE // L) + cc
                rc = pl.multiple_of(r + cc * L, L)
                qa = jnp.concatenate([qa_scr[0, pl.ds(rc, L), :], qa_scr[1, pl.ds(rc, L), :]], axis=1)
                st = jnp.concatenate([st_scr[0, n], st_scr[1, n]], axis=1)
                inter.append(_dot_nt(qa, st))
            o = oi_scr[pl.ds(r, HG_OUT_TILE), :] + jnp.concatenate(inter, axis=0)
            o = o * lax.rsqrt(jnp.mean(o * o, axis=-1, keepdims=True) + EPS) * hgn_ref[...]
            gate = pc[pl.ds(r, HG_OUT_TILE), 4 * HG_HEAD_DIM:5 * HG_HEAD_DIM]
            y_scr[pl.ds(r, HG_OUT_TILE), sl] = (o * (gate * _sigmoid(gate))).astype(BF16)
            return c

        lax.fori_loop(0, SEQ // HG_OUT_TILE, hg_out_body, 0, unroll=2)

    lane = lax.broadcasted_iota(jnp.int32, (1, LANES), 1)
    first_half = lane < NA_HEAD_DIM

    n_na_steps = ROWS // NA_ROWS_PER_STEP

    def na_units(step):
        units = []
        for dr in range(NA_ROWS_PER_STEP):
            r = step * NA_ROWS_PER_STEP + dr
            if isinstance(r, int):
                r0 = min(max(r - NA_KH // 2, 0), ROWS - NA_KH)
                qrow, krow = r * GRID_W, r0 * GRID_W
            else:
                r0 = jnp.clip(r - NA_KH // 2, 0, ROWS - NA_KH)
                qrow, krow = pl.multiple_of(r * GRID_W, GRID_W), pl.multiple_of(r0 * GRID_W, GRID_W)
            for hp in range(NA_HEADS // 2):
                units.append((qrow, krow, r - r0, hp))
        return units

    def na_scores(units):
        for u, (qrow, krow, var, hp) in enumerate(units):
            c0 = hp * LANES
            qt = p_na[pl.ds(qrow, GRID_W), c0:c0 + LANES]
            qm = jnp.concatenate([jnp.where(first_half, qt, 0.0), jnp.where(first_half, 0.0, qt)], axis=0).astype(BF16)
            qk_scr[u] = _dot_nt(qm, h_scr[pl.ds(krow, NA_KEYS), c0:c0 + LANES])

    na_scores(na_units(0))

    def na_body(i, c):
        units = na_units(i)
        probs = []
        for u, (qrow, krow, var, hp) in enumerate(units):
            bt = bias_ref[hp, var]
            s = jnp.where(bt > 0.5 * NEG_INF, qk_scr[u] + bt, NEG_INF)
            probs.append(jnp.exp(s - jnp.max(s, axis=-1, keepdims=True)).astype(BF16))
        na_scores(na_units(jnp.minimum(i + 1, n_na_steps - 1)))
        ones = jnp.ones((NA_KEYS, LANES), BF16)
        for (qrow, krow, var, hp), e in zip(units, probs):
            c0 = hp * LANES
            v_ones = jnp.concatenate([h_scr[pl.ds(krow, NA_KEYS), NA_DIM + c0:NA_DIM + c0 + LANES], ones], axis=1)
            pv = _dot(e, v_ones)
            o = pv[:, 0:LANES] / pv[:, LANES:2 * LANES]
            y_scr[pl.ds(qrow, GRID_W), HG_DIM + c0:HG_DIM + c0 + LANES] = jnp.where(
                first_half, o[0:GRID_W], o[GRID_W:2 * GRID_W]).astype(BF16)
        return c

    lax.fori_loop(0, n_na_steps, na_body, 0, unroll=8)


def _ffn_kernel(final, x_ref, y_ref, wout_ref, ng_ref, wgu_ref, wd_ref, fg_ref, o_ref, h_scr, acc_scr, x_scr):
    x = x_ref[...] + _dot(y_ref[...], wout_ref[...].astype(BF16))
    x_scr[...] = x
    ms = jnp.mean(x * x, axis=-1, keepdims=True)
    h_scr[...] = (x * lax.rsqrt(ms + EPS) * ng_ref[...]).astype(BF16)
    for c in range(FFN_NCHUNK):
        gt = _dot(h_scr[...], wgu_ref[:, FFN_CHUNK * c:FFN_CHUNK * (c + 1)].astype(BF16))
        up = _dot(h_scr[...], wgu_ref[:, FFN_DIM + FFN_CHUNK * c:FFN_DIM + FFN_CHUNK * (c + 1)].astype(BF16))
        a = (gt * _sigmoid(gt) * up).astype(BF16)
        d = _dot(a, wd_ref[FFN_CHUNK * c:FFN_CHUNK * (c + 1), :].astype(BF16))
        if c == 0:
            acc_scr[...] = d
        else:
            acc_scr[...] += d
    y = x_scr[...] + acc_scr[...]
    if final:
        y = y * lax.rsqrt(jnp.mean(y * y, axis=-1, keepdims=True) + EPS) * fg_ref[...]
    o_ref[...] = y


def _resident(shape):
    nd = len(shape)
    return pl.BlockSpec(shape, lambda *_: (0,) * nd, pipeline_mode=pl.Buffered(1))


def _layer_slice(layer, shape):
    nd = len(shape)
    return pl.BlockSpec((None,) + tuple(shape[1:]), lambda *_: (layer,) + (0,) * (nd - 1), pipeline_mode=pl.Buffered(1))


def _mixer_call(layer, x, ng, win, hlb, hgn, rpbp, cw, cb, lng, lnb):
    batch = x.shape[0]
    stacked = (ng, win, None, hgn, rpbp, cw, cb, lng, lnb)
    return pl.pallas_call(
        functools.partial(_mixer_kernel, layer),
        grid=(batch, SEQ // X_TILE),
        in_specs=[pl.BlockSpec((None, X_TILE, D_MODEL), lambda b, t: (b, t, 0))]
        + [_resident(hlb.shape) if a is None else _layer_slice(layer, a.shape) for a in stacked],
        out_specs=pl.BlockSpec((None, SEQ, D_MODEL), lambda b, t: (b, 0, 0)),
        out_shape=jax.ShapeDtypeStruct(x.shape, BF16),
        scratch_shapes=[
            pltpu.VMEM((SEQ, D_MODEL), BF16),
            pltpu.VMEM((2, SEQ, P_COLS), F32),
            pltpu.VMEM((SEQ, HG_HEAD_DIM), F32),
            pltpu.VMEM((2, SEQ, HG_HEAD_DIM), BF16),
            pltpu.VMEM((2, HG_NCHUNK, HG_HEAD_DIM, HG_HEAD_DIM), F32),
            pltpu.VMEM((2, HG_NCHUNK, HG_HEAD_DIM, HG_HEAD_DIM), BF16),
            pltpu.VMEM((2, HG_NCHUNK, SUBLANES, LANES), F32),
            pltpu.VMEM((SEQ + 2 * CONV_PAD, CONV_DIM), F32),
            pltpu.VMEM((NA_HEADS // 2, NA_KH, 2 * GRID_W, NA_KEYS), F32),
            pltpu.VMEM((2, D_MODEL, HG_COLS), BF16),
            pltpu.VMEM((NA_ROWS_PER_STEP * NA_HEADS // 2, 2 * GRID_W, NA_KEYS), F32),
            pltpu.VMEM((2, 3, HG_TILE, HG_HEAD_DIM), BF16),
        ],
        compiler_params=pltpu.CompilerParams(dimension_semantics=("arbitrary", "arbitrary"),
                                             vmem_limit_bytes=VMEM_LIMIT_BYTES),
        name=f"mixer_l{layer}",
    )(x, ng, win, hlb, hgn, rpbp, cw, cb, lng, lnb)


def _ffn_call(layer, final, x2d, y2d, wout, ng, wgu, wd, fg):
    n_tok = x2d.shape[0]
    tile = pl.BlockSpec((FFN_TILE, D_MODEL), lambda i: (i, 0))
    return pl.pallas_call(
        functools.partial(_ffn_kernel, final),
        grid=(n_tok // FFN_TILE,),
        in_specs=[tile, tile] + [_layer_slice(layer, a.shape) for a in (wout, ng, wgu, wd)] + [_resident(fg.shape)],
        out_specs=tile,
        out_shape=jax.ShapeDtypeStruct(x2d.shape, x2d.dtype),
        scratch_shapes=[pltpu.VMEM((FFN_TILE, D_MODEL), BF16), pltpu.VMEM((FFN_TILE, D_MODEL), F32),
                        pltpu.VMEM((FFN_TILE, D_MODEL), F32)],
        compiler_params=pltpu.CompilerParams(dimension_semantics=("arbitrary",), vmem_limit_bytes=VMEM_LIMIT_BYTES),
        name=f"ffn_l{layer}",
    )(x2d, y2d, wout, ng, wgu, wd, fg)


def _pad_rpb(rpb):
    v = jnp.pad(rpb, ((0, 0), (0, 0), (0, RPB_ROWS - NA_NREL_H), (RPB_LEAD, LANES - RPB_LEAD - NA_NREL_W)))
    return v.reshape(DEPTH, NA_HEADS * RPB_ROWS, LANES)


def kernel(x, mix_norm_g, w_in, hg_lower_bounds, hg_norm_g, na_rpb, conv_w, conv_b, conv_ln_g, conv_ln_b, w_out,
           ffn_norm_g, w_gate_up, w_down, final_norm_g):
    batch, seq, d = x.shape
    assert (seq, d) == (SEQ, D_MODEL) and w_in.shape == (DEPTH, D_MODEL, D_IN)
    row = lambda p: p.astype(F32).reshape(DEPTH, 1, p.shape[-1])
    win = w_in.astype(BF16)
    wout, wgu, wd = (w.astype(F32) for w in (w_out, w_gate_up, w_down))
    hlb = hg_lower_bounds.astype(F32)
    rpbp = _pad_rpb(na_rpb.astype(F32))
    mix_g, hgn, cb, lng, lnb, ffn_g = (row(p) for p in (mix_norm_g, hg_norm_g, conv_b, conv_ln_g, conv_ln_b, ffn_norm_g))
    fg = final_norm_g.astype(F32).reshape(1, D_MODEL)
    for l in range(DEPTH):
        y = _mixer_call(l, x, mix_g, win, hlb, hgn, rpbp, conv_w.astype(F32), cb, lng, lnb)
        x2d = _ffn_call(l, l == DEPTH - 1, x.reshape(batch * seq, d), y.reshape(batch * seq, d), wout, ffn_g, wgu, wd, fg)
        x = x2d.reshape(batch, seq, d)
    return x
```
